```python
import math
import jax, jax.numpy as jnp
from jax import lax
import numpy as np

D_MODEL = 1024
BATCH = 8
SEQ = 2048
DEPTH = 4

N_MIXERS = 3
N_A = (DEPTH + 2) // 3
N_B = (DEPTH + 1) // 3
N_C = DEPTH // 3

RMS_EPS = 1e-6
LN_EPS = 1e-5
ROPE_THETA = 10000.0

D_FF = -(-8 * D_MODEL // (3 * 256)) * 256

GM_CHUNK = 128
GM_WIDTH = 2 * D_MODEL
GM_GROUPS = 8

NSA_HEADS = 16
NSA_KV_GROUPS = 4
NSA_HPG = NSA_HEADS // NSA_KV_GROUPS
NSA_HEAD_DIM = D_MODEL // NSA_HEADS
NSA_KV = NSA_KV_GROUPS * NSA_HEAD_DIM
CMP_LEN = 32
CMP_STRIDE = 16
CMP_HIDDEN = 4 * NSA_HEAD_DIM
SEL_LEN = 64
N_SELECT = 8
WIN = 512
NSA_Q_BLOCK = 64
NSA_IN = NSA_HEADS * NSA_HEAD_DIM + 6 * NSA_KV + 3 * NSA_HEADS

RET_HEADS = 4
RET_QK_DIM = D_MODEL // RET_HEADS
RET_V_DIM = 2 * RET_QK_DIM
RET_CHUNK = 128
RET_IN = 2 * RET_HEADS * RET_QK_DIM + 2 * RET_HEADS * RET_V_DIM

kernel_name = "hybrid_gmlp_nsa_retention_trunk"


def rms_norm(x, g):
    x32 = x.astype(jnp.float32)
    y = x32 * lax.rsqrt(jnp.mean(x32 * x32, axis=-1, keepdims=True) + RMS_EPS)
    return (y * g.astype(jnp.float32)).astype(x.dtype)


def layer_norm(x, g, b):
    x32 = x.astype(jnp.float32)
    mu = jnp.mean(x32, axis=-1, keepdims=True)
    var = jnp.mean(jnp.square(x32 - mu), axis=-1, keepdims=True)
    y = (x32 - mu) * lax.rsqrt(var + LN_EPS) * g.astype(jnp.float32) + b.astype(jnp.float32)
    return y.astype(x.dtype)


def apply_rope(x, pos):
    half = x.shape[-1] // 2
    inv = ROPE_THETA ** (-jnp.arange(half, dtype=jnp.float32) / half)
    ang = pos.astype(jnp.float32)[..., None] * inv
    cos = jnp.cos(ang)[:, :, None, :]
    sin = jnp.sin(ang)[:, :, None, :]
    x32 = x.astype(jnp.float32)
    x1, x2 = x32[..., :half], x32[..., half:]
    return jnp.concatenate([x1 * cos - x2 * sin, x2 * cos + x1 * sin], axis=-1).astype(x.dtype)


def masked_softmax(s, mask):
    s = s.astype(jnp.float32)
    m = jnp.max(jnp.where(mask, s, -jnp.inf), axis=-1, keepdims=True)
    m = jnp.where(jnp.isfinite(m), m, 0.0)
    e = jnp.where(mask, jnp.exp(s - m), 0.0)
    return e / jnp.maximum(jnp.sum(e, axis=-1, keepdims=True), 1e-30)


def swiglu_ffn(h, w_gate, w_up, w_down):
    return (jax.nn.silu(h @ w_gate) * (h @ w_up)) @ w_down


def gmlp_mixer(h, w_in, ln_g, ln_b, w_s, b_s, w_out):
    B, S, _ = h.shape
    uv = jax.nn.gelu(h @ w_in)
    u, v = jnp.split(uv, 2, axis=-1)
    v = layer_norm(v, ln_g, ln_b)
    v = v.reshape(B, S // GM_CHUNK, GM_CHUNK, GM_GROUPS, GM_WIDTH // GM_GROUPS)
    causal = jnp.tril(jnp.ones((GM_CHUNK, GM_CHUNK), dtype=bool))
    w_mix = jnp.where(causal, w_s, 0.0).astype(v.dtype)
    v = jnp.einsum('gts,bnsgc->bntgc', w_mix, v) + b_s.T[:, :, None].astype(v.dtype)
    return (u * v.reshape(B, S, GM_WIDTH)) @ w_out


def nsa_mixer(h, positions, w_in, cmp_pe, cmp_w1, cmp_w2, w_out):
    B, S, _ = h.shape
    H, G, HPG, dk = NSA_HEADS, NSA_KV_GROUPS, NSA_HPG, NSA_HEAD_DIM
    dt = h.dtype
    splits = np.cumsum([H * dk] + [NSA_KV] * 6).tolist()
    q, kc, vc, ks, vs, kw, vw, gates = jnp.split(h @ w_in, splits, axis=-1)
    q = apply_rope(q.reshape(B, S, H, dk), positions).reshape(B, S, G, HPG, dk)
    kc = kc.reshape(B, S, G, dk)
    vc = vc.reshape(B, S, G, dk)
    ks = apply_rope(ks.reshape(B, S, G, dk), positions)
    vs = vs.reshape(B, S, G, dk)
    kw = apply_rope(kw.reshape(B, S, G, dk), positions)
    vw = vw.reshape(B, S, G, dk)
    gates = jax.nn.sigmoid(gates.astype(jnp.float32)).reshape(B, S, G, HPG, 3)

    n_cmp = (S - CMP_LEN) // CMP_STRIDE + 1
    blk_idx = jnp.arange(n_cmp)[:, None] * CMP_STRIDE + jnp.arange(CMP_LEN)[None, :]
    cmp_end = blk_idx[:, -1]

    def compress(t, pe, w1, w2):
        blocks = t[:, blk_idx] + pe[:, None, :]
        blocks = jnp.moveaxis(blocks, 2, 3).reshape(B, n_cmp, G, CMP_LEN * dk)
        return jax.nn.gelu(blocks @ w1) @ w2

    k_cmp = apply_rope(compress(kc, cmp_pe[0], cmp_w1[0], cmp_w2[0]), positions[:, cmp_end])
    v_cmp = compress(vc, cmp_pe[1], cmp_w1[1], cmp_w2[1])

    n_sel = S // SEL_LEN
    k_top = min(N_SELECT, n_sel)
    ratio = SEL_LEN // CMP_STRIDE
    pad_c = n_sel * ratio - n_cmp
    ks_blk = ks.reshape(B, n_sel, SEL_LEN, G, dk).transpose(0, 3, 1, 2, 4)
    vs_blk = vs.reshape(B, n_sel, SEL_LEN, G, dk).transpose(0, 3, 1, 2, 4)
    sel_blocks = jnp.arange(n_sel)
    b_ix = jnp.arange(B)[:, None, None, None]
    g_ix = jnp.arange(G)[None, :, None, None]

    kw_pad = jnp.pad(kw, ((0, 0), (WIN, 0), (0, 0), (0, 0)))
    vw_pad = jnp.pad(vw, ((0, 0), (WIN, 0), (0, 0), (0, 0)))
    scale = dk ** -0.5
    T = NSA_Q_BLOCK

    def q_block(qi):
        t0 = qi * T
        tq = t0 + jnp.arange(T)
        qb = lax.dynamic_slice_in_dim(q, t0, T, axis=1)
        gb = lax.dynamic_slice_in_dim(gates, t0, T, axis=1)

        s_c = jnp.einsum('btghd,bngd->bghtn', qb, k_cmp) * scale
        p_c = masked_softmax(s_c, cmp_end[None, :] <= tq[:, None])
        o_c = jnp.einsum('bghtn,bngd->btghd', p_c.astype(dt), v_cmp)

        imp = jnp.pad(p_c.sum(axis=2), ((0, 0), (0, 0), (0, 0), (0, pad_c)))
        imp = imp.reshape(B, G, T, n_sel, ratio).sum(axis=-1)
        cur = tq // SEL_LEN
        forced = ((sel_blocks[None, :] == 0) | (sel_blocks[None, :] == cur[:, None])
                  | (sel_blocks[None, :] == cur[:, None] - 1))
        valid = sel_blocks[None, :] <= cur[:, None]
        imp = jnp.where(valid, jnp.where(forced, jnp.inf, imp), -jnp.inf)
        _, top = lax.top_k(imp, k_top)
        k_sel = ks_blk[b_ix, g_ix, top]
        v_sel = vs_blk[b_ix, g_ix, top]
        key_pos = top[..., None] * SEL_LEN + jnp.arange(SEL_LEN)
        m_s = (key_pos <= tq[None, None, :, None, None]).reshape(B, G, 1, T, k_top * SEL_LEN)
        s_s = jnp.einsum('btghd,bgtnkd->bghtnk', qb, k_sel) * scale
        p_s = masked_softmax(s_s.reshape(B, G, HPG, T, k_top * SEL_LEN), m_s)
        o_s = jnp.einsum('bghtnk,bgtnkd->btghd',
                         p_s.reshape(B, G, HPG, T, k_top, SEL_LEN).astype(dt), v_sel)

        kwb = lax.dynamic_slice_in_dim(kw_pad, t0, WIN + T, axis=1)
        vwb = lax.dynamic_slice_in_dim(vw_pad, t0, WIN + T, axis=1)
        kpos = t0 - WIN + jnp.arange(WIN + T)
        m_w = ((kpos[None, :] <= tq[:, None]) & (kpos[None, :] > tq[:, None] - WIN)
               & (kpos[None, :] >= 0))
        s_w = jnp.einsum('btghd,bsgd->bghts', qb, kwb) * scale
        p_w = masked_softmax(s_w, m_w)
        o_w = jnp.einsum('bghts,bsgd->btghd', p_w.astype(dt), vwb)

        o = gb[..., 0:1] * o_c + gb[..., 1:2] * o_s + gb[..., 2:3] * o_w
        return o.astype(dt).reshape(B, T, H * dk)

    out = lax.map(q_block, jnp.arange(S // T))
    out = jnp.moveaxis(out, 0, 1).reshape(B, S, H * dk)
    return out @ w_out


def retention_mixer(h, positions, w_in, gn_g, w_out):
    B, S, _ = h.shape
    H, dk, dv, C = RET_HEADS, RET_QK_DIM, RET_V_DIM, RET_CHUNK
    f32 = jnp.float32
    q, k, v, g = jnp.split(h @ w_in, [H * dk, 2 * H * dk, 2 * H * dk + H * dv], axis=-1)
    q = apply_rope(q.reshape(B, S, H, dk), positions).astype(f32)
    k = apply_rope(k.reshape(B, S, H, dk), positions).astype(f32) * (dk ** -0.5)
    v = v.reshape(B, S, H, dv).astype(f32)
    n_ch = S // C

    def chunks(t):
        return t.reshape(B, n_ch, C, H, t.shape[-1]).transpose(1, 0, 3, 2, 4)

    log_g = jnp.log1p(-(2.0 ** (-5.0 - jnp.arange(H, dtype=f32))))
    ix = jnp.arange(C, dtype=f32)
    rel = ix[:, None] - ix[None, :]
    decay = jnp.where(rel >= 0, jnp.exp(log_g[:, None, None] * jnp.maximum(rel, 0.0)), 0.0)
    xi = jnp.exp(log_g[:, None] * (ix + 1.0))
    zeta = jnp.exp(log_g[:, None] * (C - 1.0 - ix))
    g_chunk = jnp.exp(log_g * C)

    def step(state, qkv):
        qc, kc, vc = qkv
        inner = jnp.einsum('bhnm,bhme->bhne', jnp.einsum('bhnd,bhmd->bhnm', qc, kc) * decay, vc)
        cross = jnp.einsum('bhnd,bhde->bhne', qc, state) * xi[:, :, None]
        state = g_chunk[:, None, None] * state + jnp.einsum('bhmd,bhme->bhde', kc * zeta[:, :, None], vc)
        return state, inner + cross

    state0 = jnp.zeros((B, H, dk, dv), f32)
    _, o = lax.scan(step, state0, (chunks(q), chunks(k), chunks(v)))
    o = o.transpose(1, 0, 3, 2, 4).reshape(B, S, H, dv)
    mu = jnp.mean(o, axis=-1, keepdims=True)
    var = jnp.mean(jnp.square(o - mu), axis=-1, keepdims=True)
    o = ((o - mu) * lax.rsqrt(var + LN_EPS)).reshape(B, S, H * dv) * gn_g.astype(f32)
    y = (jax.nn.silu(g.astype(f32)) * o).astype(h.dtype)
    return y @ w_out


def setup_inputs(seed: int = 0) -> dict:
    key = jax.random.key(seed)
    ks = jax.random.split(key, 24)
    f32 = jnp.float32

    def dense(k, shape, fan_in):
        return jax.random.normal(k, shape, f32) * (fan_in ** -0.5)

    def gain(k, shape):
        return 1.0 + 0.05 * jax.random.normal(k, shape, f32)

    x = jax.random.normal(ks[0], (BATCH, SEQ, D_MODEL), f32)
    offs = jax.random.randint(ks[1], (BATCH, 1), 0, 4096, dtype=jnp.int32)
    positions = (offs + jnp.arange(SEQ, dtype=jnp.int32)[None, :]).astype(jnp.int32)
    return {
        "x": x,
        "positions": positions,
        "norm_g": gain(ks[2], (DEPTH, 4, D_MODEL)),
        "ffn_w_gate": dense(ks[3], (DEPTH, D_MODEL, D_FF), D_MODEL),
        "ffn_w_up": dense(ks[4], (DEPTH, D_MODEL, D_FF), D_MODEL),
        "ffn_w_down": dense(ks[5], (DEPTH, D_FF, D_MODEL), D_FF),
        "gm_w_in": dense(ks[6], (N_A, D_MODEL, 2 * GM_WIDTH), D_MODEL),
        "gm_ln_g": gain(ks[7], (N_A, GM_WIDTH)),
        "gm_ln_b": 0.02 * jax.random.normal(ks[8], (N_A, GM_WIDTH), f32),
        "gm_w_s": dense(ks[9], (N_A, GM_GROUPS, GM_CHUNK, GM_CHUNK), GM_CHUNK),
        "gm_b_s": 1.0 + 0.1 * jax.random.normal(ks[10], (N_A, GM_GROUPS, GM_CHUNK), f32),
        "gm_w_out": dense(ks[11], (N_A, GM_WIDTH, D_MODEL), GM_WIDTH),
        "nsa_w_in": dense(ks[12], (N_B, D_MODEL, NSA_IN), D_MODEL),
        "nsa_cmp_pe": 0.1 * jax.random.normal(ks[13], (N_B, 2, CMP_LEN, NSA_HEAD_DIM), f32),
        "nsa_cmp_w1": dense(ks[14], (N_B, 2, CMP_LEN * NSA_HEAD_DIM, CMP_HIDDEN), CMP_LEN * NSA_HEAD_DIM),
        "nsa_cmp_w2": dense(ks[15], (N_B, 2, CMP_HIDDEN, NSA_HEAD_DIM), CMP_HIDDEN),
        "nsa_w_out": dense(ks[16], (N_B, NSA_HEADS * NSA_HEAD_DIM, D_MODEL), NSA_HEADS * NSA_HEAD_DIM),
        "ret_w_in": dense(ks[17], (N_C, D_MODEL, RET_IN), D_MODEL),
        "ret_gn_g": gain(ks[18], (N_C, RET_HEADS * RET_V_DIM)),
        "ret_w_out": dense(ks[19], (N_C, RET_HEADS * RET_V_DIM, D_MODEL), RET_HEADS * RET_V_DIM),
    }


def reference(x, positions, norm_g, ffn_w_gate, ffn_w_up, ffn_w_down,
              gm_w_in, gm_ln_g, gm_ln_b, gm_w_s, gm_b_s, gm_w_out,
              nsa_w_in, nsa_cmp_pe, nsa_cmp_w1, nsa_cmp_w2, nsa_w_out,
              ret_w_in, ret_gn_g, ret_w_out):
    h = x
    for i in range(DEPTH):
        kind, j = i % N_MIXERS, i // N_MIXERS
        a = rms_norm(h, norm_g[i, 0])
        if kind == 0:
            m = gmlp_mixer(a, gm_w_in[j], gm_ln_g[j], gm_ln_b[j], gm_w_s[j], gm_b_s[j], gm_w_out[j])
        elif kind == 1:
            m = nsa_mixer(a, positions, nsa_w_in[j], nsa_cmp_pe[j], nsa_cmp_w1[j],
                          nsa_cmp_w2[j], nsa_w_out[j])
        else:
            m = retention_mixer(a, positions, ret_w_in[j], ret_gn_g[j], ret_w_out[j])
        h = h + rms_norm(m, norm_g[i, 1])
        f = swiglu_ffn(rms_norm(h, norm_g[i, 2]), ffn_w_gate[i], ffn_w_up[i], ffn_w_down[i])
        h = h + rms_norm(f, norm_g[i, 3])
    return h
```

```python
import functools
import math

import jax
import jax.numpy as jnp
from jax import lax
from jax.experimental import pallas as pl
from jax.experimental.pallas import tpu as pltpu

F32 = jnp.float32
BF16 = jnp.bfloat16

D_MODEL = 1024
DEPTH = 4
N_MIXERS = 3
RMS_EPS = 1e-6
LN_EPS = 1e-5
ROPE_THETA = 10000.0
D_FF = 2816

GM_CHUNK = 128
GM_WIDTH = 2 * D_MODEL
GM_GROUPS = 8
GM_GROUP_W = GM_WIDTH // GM_GROUPS

NSA_HEADS = 16
NSA_KV_GROUPS = 4
NSA_HPG = NSA_HEADS // NSA_KV_GROUPS
NSA_HEAD_DIM = 64
NSA_KV = NSA_KV_GROUPS * NSA_HEAD_DIM
CMP_LEN = 32
CMP_STRIDE = 16
CMP_HIDDEN = 4 * NSA_HEAD_DIM
SEL_LEN = 64
N_SELECT = 8
WIN = 512

RET_HEADS = 4
RET_QK_DIM = 256
RET_V_DIM = 512
RET_CHUNK = 128

LANES = 128
VMEM_LIMIT = 48 * 1024 * 1024
NEG_BIG = -1e30

ROW_TILE = 512
GM_ROW_TILE = 256
FFN_COL_CHUNK = 256
ATT_TQ = 128
ATT_TK = 128


def _cparams(*sem):
    return pltpu.CompilerParams(dimension_semantics=sem, vmem_limit_bytes=VMEM_LIMIT)


def _resident(shape):
    nd = len(shape)
    return pl.BlockSpec(shape, lambda *_: (0,) * nd, pipeline_mode=pl.Buffered(1))


def _rms(x32, g):
    ms = jnp.mean(x32 * x32, axis=-1, keepdims=True)
    return x32 * lax.rsqrt(ms + RMS_EPS) * g


def _gelu_tanh(x):
    c = math.sqrt(2.0 / math.pi)
    return x * (0.5 * (1.0 + jnp.tanh(c * (x + 0.044715 * (x * x * x)))))


def _sigmoid(x):
    return 1.0 / (1.0 + jnp.exp(-x))


def _dot(a, b):
    return jnp.dot(a, b, preferred_element_type=F32)


def _proj_kernel(*refs, epilogue, n_tables, n_q_tiles):
    x_ref, g_ref, w_ref = refs[:3]
    tables = refs[3:3 + n_tables]
    o_ref, a_scr = refs[3 + n_tables:]
    j = pl.program_id(1)

    @pl.when(j == 0)
    def _():
        a_scr[...] = _rms(x_ref[...], g_ref[...]).astype(BF16)

    y = _dot(a_scr[...], w_ref[...])
    tn = y.shape[1]
    if epilogue == "plain":
        o_ref[...] = y.astype(o_ref.dtype)
    elif epilogue == "sigmoid":
        o_ref[...] = _sigmoid(y).astype(o_ref.dtype)
    elif epilogue == "rope64":
        cos, sin_lo, sin_hi = (t[...] for t in tables)
        for c in range(tn // LANES):
            yc = y[:, c * LANES:(c + 1) * LANES]
            out = (yc * cos + pltpu.roll(yc, LANES - 32, 1) * sin_lo
                   + pltpu.roll(yc, 32, 1) * sin_hi)
            o_ref[:, c * LANES:(c + 1) * LANES] = out.astype(o_ref.dtype)
    elif epilogue == "rope256":
        cos, sin = (t[...] for t in tables)
        scale = jnp.where(j >= n_q_tiles, RET_QK_DIM ** -0.5, 1.0).astype(F32)
        for c in range(tn // RET_QK_DIM):
            y1 = y[:, c * 256:c * 256 + 128]
            y2 = y[:, c * 256 + 128:(c + 1) * 256]
            o_ref[:, c * 256:c * 256 + 128] = ((y1 * cos - y2 * sin) * scale).astype(o_ref.dtype)
            o_ref[:, c * 256 + 128:(c + 1) * 256] = ((y2 * cos + y1 * sin) * scale).astype(o_ref.dtype)
    else:
        raise ValueError(epilogue)


def _norm_proj(x2d, g, w, *, tn, out_dtype, epilogue="plain", tables=(), n_q_tiles=0):
    n, d = x2d.shape
    n_out = w.shape[1]
    tm = ROW_TILE
    assert n % tm == 0 and n_out % tn == 0
    table_specs = [pl.BlockSpec((tm, t.shape[1]), lambda i, j: (i, 0)) for t in tables]
    return pl.pallas_call(
        functools.partial(_proj_kernel, epilogue=epilogue, n_tables=len(tables), n_q_tiles=n_q_tiles),
        grid=(n // tm, n_out // tn),
        in_specs=[pl.BlockSpec((tm, d), lambda i, j: (i, 0)),
                  pl.BlockSpec((1, d), lambda i, j: (0, 0)),
                  pl.BlockSpec((d, tn), lambda i, j: (0, j))] + table_specs,
        out_specs=pl.BlockSpec((tm, tn), lambda i, j: (i, j)),
        out_shape=jax.ShapeDtypeStruct((n, n_out), out_dtype),
        scratch_shapes=[pltpu.VMEM((tm, d), BF16)],
        compiler_params=_cparams("parallel", "arbitrary"),
        name="norm_proj_" + epilogue,
    )(x2d, g.reshape(1, d), w, *tables)


def _out_kernel(m_ref, w_ref, g_ref, h_ref, o_ref):
    y = _dot(m_ref[...], w_ref[...])
    o_ref[...] = h_ref[...] + _rms(y, g_ref[...])


def _proj_norm_residual(m2d, w, g, h2d):
    n, k = m2d.shape
    d = w.shape[1]
    tm = ROW_TILE
    return pl.pallas_call(
        _out_kernel,
        grid=(n // tm,),
        in_specs=[pl.BlockSpec((tm, k), lambda i: (i, 0)),
                  _resident((k, d)),
                  _resident((1, d)),
                  pl.BlockSpec((tm, d), lambda i: (i, 0))],
        out_specs=pl.BlockSpec((tm, d), lambda i: (i, 0)),
        out_shape=jax.ShapeDtypeStruct((n, d), F32),
        compiler_params=_cparams("parallel"),
        name="proj_norm_residual",
    )(m2d, w, g.reshape(1, d), h2d)


def _ffn_kernel(h_ref, g_in_ref, wg_ref, wu_ref, wd_ref, g_out_ref, o_ref, acc_ref):
    x = h_ref[...]
    a = _rms(x, g_in_ref[...]).astype(BF16)
    for c in range(D_FF // FFN_COL_CHUNK):
        cols = slice(c * FFN_COL_CHUNK, (c + 1) * FFN_COL_CHUNK)
        gate = _dot(a, wg_ref[:, cols])
        up = _dot(a, wu_ref[:, cols])
        act = (gate * _sigmoid(gate) * up).astype(BF16)
        part = _dot(act, wd_ref[cols, :])
        if c == 0:
            acc_ref[...] = part
        else:
            acc_ref[...] += part
    o_ref[...] = x + _rms(acc_ref[...], g_out_ref[...])


def _ffn(h2d, g_in, wg, wu, wd, g_out):
    n, d = h2d.shape
    tm = ROW_TILE
    return pl.pallas_call(
        _ffn_kernel,
        grid=(n // tm,),
        in_specs=[pl.BlockSpec((tm, d), lambda i: (i, 0)),
                  _resident((1, d)),
                  _resident((d, D_FF)),
                  _resident((d, D_FF)),
                  _resident((D_FF, d)),
                  _resident((1, d))],
        out_specs=pl.BlockSpec((tm, d), lambda i: (i, 0)),
        out_shape=jax.ShapeDtypeStruct((n, d), F32),
        scratch_shapes=[pltpu.VMEM((tm, d), F32)],
        compiler_params=_cparams("parallel"),
        name="swiglu_ffn",
    )(h2d, g_in.reshape(1, d), wg, wu, wd, g_out.reshape(1, d))


def _gmlp_kernel(h_ref, g_in_ref, win_ref, lng_ref, lnb_ref, ws_ref, bst_ref, wout_ref,
                 g_out_ref, o_ref, vn_scr, y_scr):
    x = h_ref[...]
    tm = x.shape[0]
    a = _rms(x, g_in_ref[...]).astype(BF16)

    v = _gelu_tanh(_dot(a, win_ref[:, GM_WIDTH:]))
    mu = jnp.mean(v, axis=-1, keepdims=True)
    vc = v - mu
    var = jnp.mean(vc * vc, axis=-1, keepdims=True)
    vn_scr[...] = (vc * lax.rsqrt(var + LN_EPS) * lng_ref[...] + lnb_ref[...]).astype(BF16)

    t_out = lax.broadcasted_iota(jnp.int32, (GM_CHUNK, GM_CHUNK), 0)
    t_in = lax.broadcasted_iota(jnp.int32, (GM_CHUNK, GM_CHUNK), 1)
    causal = t_in <= t_out
    bst = bst_ref[...]
    for grp in range(GM_GROUPS):
        cols = slice(grp * GM_GROUP_W, (grp + 1) * GM_GROUP_W)
        w_mix = jnp.where(causal, ws_ref[grp], 0.0).astype(BF16)
        bias = bst[:, grp:grp + 1]
        for c in range(tm // GM_CHUNK):
            rows = slice(c * GM_CHUNK, (c + 1) * GM_CHUNK)
            mixed = _dot(w_mix, vn_scr[rows, cols]) + bias
            u = _gelu_tanh(_dot(a[rows], win_ref[:, cols]))
            y_scr[rows, cols] = (u * mixed).astype(BF16)

    m = _dot(y_scr[...], wout_ref[...])
    o_ref[...] = x + _rms(m, g_out_ref[...])


def _gmlp_layer(h2d, g_in, w_in, ln_g, ln_b, w_s, b_s, w_out, g_out):
    n, d = h2d.shape
    tm = GM_ROW_TILE
    return pl.pallas_call(
        _gmlp_kernel,
        grid=(n // tm,),
        in_specs=[pl.BlockSpec((tm, d), lambda i: (i, 0)),
                  _resident((1, d)),
                  _resident((d, 2 * GM_WIDTH)),
                  _resident((1, GM_WIDTH)),
                  _resident((1, GM_WIDTH)),
                  _resident((GM_GROUPS, GM_CHUNK, GM_CHUNK)),
                  _resident((GM_CHUNK, GM_GROUPS)),
                  _resident((GM_WIDTH, d)),
                  _resident((1, d))],
        out_specs=pl.BlockSpec((tm, d), lambda i: (i, 0)),
        out_shape=jax.ShapeDtypeStruct((n, d), F32),
        scratch_shapes=[pltpu.VMEM((tm, GM_WIDTH), BF16), pltpu.VMEM((tm, GM_WIDTH), BF16)],
        compiler_params=_cparams("parallel"),
        name="gmlp_layer",
    )(h2d, g_in.reshape(1, d), w_in.astype(BF16), ln_g.reshape(1, -1), ln_b.reshape(1, -1),
      w_s, b_s.T, w_out.astype(BF16), g_out.reshape(1, d))


def _rope_tables(positions, half):
    inv = ROPE_THETA ** (-jnp.arange(half, dtype=F32) / half)
    ang = positions.astype(F32)[..., None] * inv
    return jnp.cos(ang), jnp.sin(ang)


def _cmp_kernel(f_ref, pe_ref, w1_ref, w2_ref, cos_ref, sin_ref, o_ref):
    half = CMP_LEN * NSA_HEAD_DIM // 2
    x = f_ref[0, 0, 0].astype(F32)
    pe = pe_ref[0]
    first = _dot((x + pe[:, :half]).astype(BF16), w1_ref[0, :half, :])
    second = _dot((x + pe[:, half:]).astype(BF16), w1_ref[0, half:, :])
    hid = _gelu_tanh(first + pltpu.roll(second, second.shape[0] - 1, 0))
    y = _dot(hid.astype(BF16), w2_ref[0])
    out = y[:, :NSA_HEAD_DIM] * cos_ref[0, 0] + y[:, NSA_HEAD_DIM:] * sin_ref[0, 0]
    o_ref[0, 0, 0] = out.astype(o_ref.dtype)


def _compress(f, pe_flat, w1, w2cat, cos_t, sin_t):
    _, b, g, n_grp, width = f.shape
    dk = NSA_HEAD_DIM
    return pl.pallas_call(
        _cmp_kernel,
        grid=(2, b, g),
        in_specs=[pl.BlockSpec((1, 1, 1, n_grp, width), lambda s, i, j: (s, i, j, 0, 0)),
                  pl.BlockSpec((1, 1, pe_flat.shape[-1]), lambda s, i, j: (s, 0, 0)),
                  pl.BlockSpec((1,) + w1.shape[1:], lambda s, i, j: (s, 0, 0)),
                  pl.BlockSpec((1,) + w2cat.shape[1:], lambda s, i, j: (s, 0, 0)),
                  pl.BlockSpec((1, 1, n_grp, dk), lambda s, i, j: (s, i, 0, 0)),
                  pl.BlockSpec((1, 1, n_grp, dk), lambda s, i, j: (s, i, 0, 0))],
        out_specs=pl.BlockSpec((1, 1, 1, n_grp, dk), lambda s, i, j: (s, i, j, 0, 0)),
        out_shape=jax.ShapeDtypeStruct((2, b, g, n_grp, dk), BF16),
        compiler_params=_cparams("parallel", "parallel", "parallel"),
        name="nsa_compress",
    )(f, pe_flat, w1, w2cat, cos_t, sin_t)


def _nsa_attn_kernel(q_ref, kst_ref, vs_ref, kwt_ref, vw_ref, kct_ref, vc_ref, gate_ref,
                     o_ref, m_scr, l_scr, acc_scr):
    tq, tk, hpg = ATT_TQ, ATT_TK, NSA_HPG
    rows = hpg * tq
    i = pl.program_id(2)
    t0 = i * tq
    q4 = q_ref[0, 0].reshape(rows, NSA_HEAD_DIM)
    q4 = (q4.astype(F32) * NSA_HEAD_DIM ** -0.5).astype(BF16)

    tok4 = t0 + (lax.broadcasted_iota(jnp.int32, (rows, tk), 0) & (tq - 1))
    col4 = lax.broadcasted_iota(jnp.int32, (rows, tk), 1)
    tok1 = t0 + lax.broadcasted_iota(jnp.int32, (tq, LANES), 0)
    lane1 = lax.broadcasted_iota(jnp.int32, (tq, LANES), 1)

    s_c = _dot(q4, kct_ref[0, 0])
    mask_c = (col4 * CMP_STRIDE + (CMP_LEN - 1)) <= tok4
    m_c = jnp.max(jnp.where(mask_c, s_c, NEG_BIG), axis=-1, keepdims=True)
    e_c = jnp.where(mask_c, jnp.exp(s_c - m_c), 0.0)
    p_c = e_c / jnp.maximum(jnp.sum(e_c, axis=-1, keepdims=True), 1e-30)
    o_c = _dot(p_c.astype(BF16), vc_ref[0, 0])

    p_sum = p_c[0:tq]
    for h in range(1, hpg):
        p_sum = p_sum + p_c[h * tq:(h + 1) * tq]
    pair = p_sum + pltpu.roll(p_sum, LANES - 1, 1)
    imp = pair + pltpu.roll(pair, LANES - 2, 1)
    ratio = SEL_LEN // CMP_STRIDE
    ratio_shift = ratio.bit_length() - 1
    sel_shift = SEL_LEN.bit_length() - 1
    n_sel = LANES // ratio
    blk = lane1 >> ratio_shift
    cur = tok1 >> sel_shift
    forced = (blk == 0) | (blk == cur) | (blk == cur - 1)
    valid = blk <= cur
    score = jnp.where(valid, jnp.where(forced, jnp.inf, imp), -jnp.inf)
    rank = jnp.zeros((tq, LANES), F32)
    for k in range(1, n_sel):
        other = pltpu.roll(score, ratio * k, 1)
        wins_tie = blk >= k
        beats = (other > score) | ((other == score) & wins_tie)
        rank = rank + jnp.where(beats, 1.0, 0.0)
    selected = (rank < float(N_SELECT)) & ((lane1 & (ratio - 1)) == 0)
    sel_bf = jnp.where(selected, 1.0, 0.0).astype(BF16)

    exp_row = lax.broadcasted_iota(jnp.int32, (LANES, tk), 0)
    exp_col = lax.broadcasted_iota(jnp.int32, (LANES, tk), 1)
    exp_is_blk = (exp_row & (ratio - 1)) == 0
    exp_blk = exp_row >> ratio_shift

    def attend(kt_ref, v_ref, j_lo, j_hi, mask_fn):
        m_scr[...] = jnp.full(m_scr.shape, NEG_BIG, F32)
        l_scr[...] = jnp.zeros(l_scr.shape, F32)
        acc_scr[...] = jnp.zeros(acc_scr.shape, F32)

        def body(j, carry):
            k0 = pl.multiple_of(j * tk, tk)
            s = _dot(q4, kt_ref[0, 0, :, pl.ds(k0, tk)])
            mask = mask_fn(j, k0)
            m_old = m_scr[...]
            m_new = jnp.maximum(m_old, jnp.max(jnp.where(mask, s, NEG_BIG), axis=-1, keepdims=True))
            p = jnp.where(mask, jnp.exp(s - m_new), 0.0)
            alpha = jnp.exp(m_old - m_new)
            l_scr[...] = alpha * l_scr[...] + jnp.sum(p, axis=-1, keepdims=True)
            acc_scr[...] = alpha * acc_scr[...] + _dot(p.astype(BF16), v_ref[0, 0, pl.ds(k0, tk), :])
            m_scr[...] = m_new
            return carry

        lax.fori_loop(j_lo, j_hi, body, 0)
        return acc_scr[...] / jnp.maximum(l_scr[...], 1e-30)

    def sel_mask(j, k0):
        expand = exp_is_blk & (exp_blk == ((k0 + exp_col) >> sel_shift))
        sel_keys = _dot(sel_bf, jnp.where(expand, 1.0, 0.0).astype(BF16))
        sel4 = jnp.concatenate([sel_keys] * hpg, axis=0)
        return (sel4 > 0.5) & ((k0 + col4) <= tok4)

    def win_mask(j, k0):
        kpos = k0 + col4
        return (kpos <= tok4) & (kpos > tok4 - WIN)

    o_s = attend(kst_ref, vs_ref, 0, i + 1, sel_mask)
    o_w = attend(kwt_ref, vw_ref, jnp.maximum(i - WIN // tk, 0), i + 1, win_mask)

    gates = gate_ref[0, 0]
    for h in range(hpg):
        r = slice(h * tq, (h + 1) * tq)
        out = (gates[:, 3 * h:3 * h + 1] * o_c[r] + gates[:, 3 * h + 1:3 * h + 2] * o_s[r]
               + gates[:, 3 * h + 2:3 * h + 3] * o_w[r])
        o_ref[0, 0, h] = out.astype(o_ref.dtype)


def _nsa_attention(q, kst, vs, kwt, vw, kct, vc, gates):
    b, g, hpg, s, dk = q.shape
    n_cmp = kct.shape[-1]
    tq = ATT_TQ
    rows = hpg * tq
    kt_spec = pl.BlockSpec((1, 1, dk, s), lambda bi, gi, i: (bi, gi, 0, 0))
    v_spec = pl.BlockSpec((1, 1, s, dk), lambda bi, gi, i: (bi, gi, 0, 0))
    return pl.pallas_call(
        _nsa_attn_kernel,
        grid=(b, g, s // tq),
        in_specs=[pl.BlockSpec((1, 1, hpg, tq, dk), lambda bi, gi, i: (bi, gi, 0, i, 0)),
                  kt_spec, v_spec, kt_spec, v_spec,
                  pl.BlockSpec((1, 1, dk, n_cmp), lambda bi, gi, i: (bi, gi, 0, 0)),
                  pl.BlockSpec((1, 1, n_cmp, dk), lambda bi, gi, i: (bi, gi, 0, 0)),
                  pl.BlockSpec((1, 1, tq, 3 * hpg), lambda bi, gi, i: (bi, gi, i, 0))],
        out_specs=pl.BlockSpec((1, 1, hpg, tq, dk), lambda bi, gi, i: (bi, gi, 0, i, 0)),
        out_shape=jax.ShapeDtypeStruct((b, g, hpg, s, dk), BF16),
        scratch_shapes=[pltpu.VMEM((rows, 1), F32), pltpu.VMEM((rows, 1), F32),
                        pltpu.VMEM((rows, dk), F32)],
        compiler_params=_cparams("parallel", "parallel", "parallel"),
        name="nsa_attention",
    )(q, kst, vs, kwt, vw, kct, vc, gates)


def _nsa_layer(h2d, positions, g_in, w_in, cmp_pe, cmp_w1, cmp_w2, w_out, g_out):
    b, s = positions.shape
    n, d = h2d.shape
    heads, grp, hpg, dk = NSA_HEADS, NSA_KV_GROUPS, NSA_HPG, NSA_HEAD_DIM
    half = dk // 2
    qw = heads * dk
    col = lambda k: w_in[:, qw + k * NSA_KV: qw + (k + 1) * NSA_KV]
    w_rope = jnp.concatenate([w_in[:, :qw], col(2), col(4)], axis=1).astype(BF16)
    w_plain = jnp.concatenate([col(0), col(1), col(3), col(5)], axis=1).astype(BF16)
    w_gate = jnp.pad(w_in[:, qw + 6 * NSA_KV:], ((0, 0), (0, LANES - 3 * heads))).astype(BF16)

    cos, sin = _rope_tables(positions, half)
    zeros = jnp.zeros_like(sin)
    lane_tile = lambda t: jnp.tile(t, (1, 1, LANES // t.shape[-1])).reshape(n, LANES)
    cos_t = lane_tile(cos)
    sin_lo = lane_tile(jnp.concatenate([-sin, zeros], axis=-1))
    sin_hi = lane_tile(jnp.concatenate([zeros, sin], axis=-1))

    roped = _norm_proj(h2d, g_in, w_rope, tn=512, out_dtype=BF16, epilogue="rope64",
                       tables=(cos_t, sin_lo, sin_hi))
    plain = _norm_proj(h2d, g_in, w_plain, tn=512, out_dtype=BF16)
    gates = _norm_proj(h2d, g_in, w_gate, tn=LANES, out_dtype=F32, epilogue="sigmoid")

    q = roped[:, :qw].reshape(b, s, grp, hpg, dk).transpose(0, 2, 3, 1, 4)
    to_kt = lambda t: t.reshape(b, s, grp, dk).transpose(0, 2, 3, 1)
    to_v = lambda t: t.reshape(b, s, grp, dk).transpose(0, 2, 1, 3)
    kst = to_kt(roped[:, qw:qw + NSA_KV])
    kwt = to_kt(roped[:, qw + NSA_KV:])
    kc, vc, vs, vw = (to_v(plain[:, k * NSA_KV:(k + 1) * NSA_KV]) for k in range(4))
    gates = gates[:, :3 * heads].reshape(b, s, grp, 3 * hpg).transpose(0, 2, 1, 3)

    n_grp = s // CMP_STRIDE
    f = jnp.stack([kc, vc]).reshape(2, b, grp, n_grp, CMP_STRIDE * dk)
    pe_flat = cmp_pe.reshape(2, 1, CMP_LEN * dk)
    rot = lambda w: jnp.concatenate([-w[..., half:], w[..., :half]], axis=-1)
    w2cat = jnp.concatenate([cmp_w2, rot(cmp_w2)], axis=-1).astype(BF16)
    n_cmp = (s - CMP_LEN) // CMP_STRIDE + 1
    pad = ((0, 0), (0, n_grp - n_cmp), (0, 0))
    cos_e = jnp.pad(cos[:, CMP_LEN - 1::CMP_STRIDE], pad, constant_values=1.0)
    sin_e = jnp.pad(sin[:, CMP_LEN - 1::CMP_STRIDE], pad)
    cos_c = jnp.stack([jnp.concatenate([cos_e, cos_e], -1), jnp.ones((b, n_grp, dk), F32)])
    sin_c = jnp.stack([jnp.concatenate([sin_e, sin_e], -1), jnp.zeros((b, n_grp, dk), F32)])
    cmp = _compress(f, pe_flat, cmp_w1.astype(BF16), w2cat, cos_c, sin_c)
    kct = cmp[0].transpose(0, 1, 3, 2)

    o = _nsa_attention(q, kst, vs, kwt, vw, kct, cmp[1], gates)
    o2d = o.transpose(0, 3, 1, 2, 4).reshape(n, heads * dk)
    return _proj_norm_residual(o2d, w_out.astype(BF16), g_out, h2d)


def _ret_kernel(q_ref, k_ref, v_ref, g_ref, gn_ref, decay_ref, xi_ref, zeta_ref, gc_ref,
                o_ref, state_scr):
    @pl.when(pl.program_id(2) == 0)
    def _():
        state_scr[...] = jnp.zeros(state_scr.shape, F32)

    q = q_ref[...]
    k = k_ref[...]
    v = v_ref[...]
    s = lax.dot_general(q, k, (((1,), (1,)), ((), ())), preferred_element_type=F32)
    inner = _dot((s * decay_ref[0]).astype(BF16), v)
    state = state_scr[...]
    cross = _dot(q, state.astype(BF16)) * xi_ref[0]
    kz = (k.astype(F32) * zeta_ref[0]).T.astype(BF16)
    state_scr[...] = gc_ref[0] * state + _dot(kz, v)

    o = inner + cross
    mu = jnp.mean(o, axis=-1, keepdims=True)
    oc = o - mu
    var = jnp.mean(oc * oc, axis=-1, keepdims=True)
    on = oc * lax.rsqrt(var + LN_EPS) * gn_ref[...]
    gate = g_ref[...].astype(F32)
    o_ref[...] = (gate * _sigmoid(gate) * on).astype(o_ref.dtype)


def _retention_core(qk, vg, gn_g, b, s):
    heads, dk, dv, c = RET_HEADS, RET_QK_DIM, RET_V_DIM, RET_CHUNK
    n_ch = s // c
    log_g = jnp.log1p(-(2.0 ** (-5.0 - jnp.arange(heads, dtype=F32))))
    ix = jnp.arange(c, dtype=F32)
    rel = ix[:, None] - ix[None, :]
    decay = jnp.where(rel >= 0, jnp.exp(log_g[:, None, None] * jnp.maximum(rel, 0.0)), 0.0)
    xi = jnp.exp(log_g[:, None] * (ix + 1.0))[:, :, None]
    zeta = jnp.exp(log_g[:, None] * (c - 1.0 - ix))[:, :, None]
    g_chunk = jnp.exp(log_g * c)[:, None, None]
    tok = lambda bi, hi, ci: bi * n_ch + ci
    return pl.pallas_call(
        _ret_kernel,
        grid=(b, heads, n_ch),
        in_specs=[pl.BlockSpec((c, dk), lambda bi, hi, ci: (tok(bi, hi, ci), hi)),
                  pl.BlockSpec((c, dk), lambda bi, hi, ci: (tok(bi, hi, ci), heads + hi)),
                  pl.BlockSpec((c, dv), lambda bi, hi, ci: (tok(bi, hi, ci), hi)),
                  pl.BlockSpec((c, dv), lambda bi, hi, ci: (tok(bi, hi, ci), heads + hi)),
                  pl.BlockSpec((1, dv), lambda bi, hi, ci: (0, hi)),
                  pl.BlockSpec((1, c, c), lambda bi, hi, ci: (hi, 0, 0)),
                  pl.BlockSpec((1, c, 1), lambda bi, hi, ci: (hi, 0, 0)),
                  pl.BlockSpec((1, c, 1), lambda bi, hi, ci: (hi, 0, 0)),
                  pl.BlockSpec((1, 1, 1), lambda bi, hi, ci: (hi, 0, 0))],
        out_specs=pl.BlockSpec((c, dv), lambda bi, hi, ci: (tok(bi, hi, ci), hi)),
        out_shape=jax.ShapeDtypeStruct((b * s, heads * dv), BF16),
        scratch_shapes=[pltpu.VMEM((dk, dv), F32)],
        compiler_params=_cparams("parallel", "parallel", "arbitrary"),
        name="retention_core",
    )(qk, qk, vg, vg, gn_g.reshape(1, -1), decay, xi, zeta, g_chunk)


def _retention_layer(h2d, positions, g_in, w_in, gn_g, w_out, g_out):
    b, s = positions.shape
    n = b * s
    heads, dk, dv = RET_HEADS, RET_QK_DIM, RET_V_DIM
    w = w_in.astype(BF16)
    cos, sin = _rope_tables(positions, dk // 2)
    qk = _norm_proj(h2d, g_in, w[:, :2 * heads * dk], tn=512, out_dtype=BF16, epilogue="rope256",
                    tables=(cos.reshape(n, -1), sin.reshape(n, -1)), n_q_tiles=heads * dk // 512)
    vg = _norm_proj(h2d, g_in, w[:, 2 * heads * dk:], tn=1024, out_dtype=BF16)
    y = _retention_core(qk, vg, gn_g, b, s)
    return _proj_norm_residual(y, w_out.astype(BF16), g_out, h2d)


def kernel(x, positions, norm_g, ffn_w_gate, ffn_w_up, ffn_w_down, gm_w_in, gm_ln_g, gm_ln_b, gm_w_s, gm_b_s, gm_w_out, nsa_w_in, nsa_cmp_pe, nsa_cmp_w1, nsa_cmp_w2, nsa_w_out, ret_w_in, ret_gn_g, ret_w_out):
    b, s, d = x.shape
    h = x.reshape(b * s, d)
    for i in range(DEPTH):
        kind, j = i % N_MIXERS, i // N_MIXERS
        if kind == 0:
            h = _gmlp_layer(h, norm_g[i, 0], gm_w_in[j], gm_ln_g[j], gm_ln_b[j], gm_w_s[j],
                            gm_b_s[j], gm_w_out[j], norm_g[i, 1])
        elif kind == 1:
            h = _nsa_layer(h, positions, norm_g[i, 0], nsa_w_in[j], nsa_cmp_pe[j], nsa_cmp_w1[j],
                           nsa_cmp_w2[j], nsa_w_out[j], norm_g[i, 1])
        else:
            h = _retention_layer(h, positions, norm_g[i, 0], ret_w_in[j], ret_gn_g[j],
                                 ret_w_out[j], norm_g[i, 1])
        h = _ffn(h, norm_g[i, 2], ffn_w_gate[i].astype(BF16), ffn_w_up[i].astype(BF16),
                 ffn_w_down[i].astype(BF16), norm_g[i, 3])
    return h.reshape(b, s, d)
```

```python
import functools
import math

import jax
import jax.numpy as jnp
from jax import lax
from jax.experimental import pallas as pl
from jax.experimental.pallas import tpu as pltpu

F32 = jnp.float32
BF16 = jnp.bfloat16

D_MODEL = 1024
DEPTH = 4
N_MIXERS = 3
RMS_EPS = 1e-6
LN_EPS = 1e-5
ROPE_THETA = 10000.0
D_FF = 2816

GM_CHUNK = 128
GM_WIDTH = 2 * D_MODEL
GM_GROUPS = 8
GM_GROUP_W = GM_WIDTH // GM_GROUPS

NSA_HEADS = 16
NSA_KV_GROUPS = 4
NSA_HPG = NSA_HEADS // NSA_KV_GROUPS
NSA_HEAD_DIM = 64
NSA_KV = NSA_KV_GROUPS * NSA_HEAD_DIM
CMP_LEN = 32
CMP_STRIDE = 16
CMP_HIDDEN = 4 * NSA_HEAD_DIM
SEL_LEN = 64
N_SELECT = 8
WIN = 512

RET_HEADS = 4
RET_QK_DIM = 256
RET_V_DIM = 512
RET_CHUNK = 128

LANES = 128
VMEM_LIMIT = 48 * 1024 * 1024
NEG_BIG = -1e30

ROW_TILE = 512
GM_ROW_TILE = 256
FFN_COL_CHUNK = 256
ATT_TQ = 128
ATT_TK = 128
SEL_CHUNK = 512


def _cparams(*sem):
    return pltpu.CompilerParams(dimension_semantics=sem, vmem_limit_bytes=VMEM_LIMIT)


def _resident(shape):
    nd = len(shape)
    return pl.BlockSpec(shape, lambda *_: (0,) * nd, pipeline_mode=pl.Buffered(1))


def _rms(x32, g):
    ms = jnp.mean(x32 * x32, axis=-1, keepdims=True)
    return x32 * lax.rsqrt(ms + RMS_EPS) * g


def _gelu_tanh(x):
    c = math.sqrt(2.0 / math.pi)
    return x * (0.5 * (1.0 + jnp.tanh(c * (x + 0.044715 * (x * x * x)))))


def _sigmoid(x):
    return 1.0 / (1.0 + jnp.exp(-x))


def _dot(a, b):
    return jnp.dot(a, b, preferred_element_type=F32)


def _proj_kernel(*refs, epilogue, n_tables, n_q_tiles):
    x_ref, g_ref, w_ref = refs[:3]
    tables = refs[3:3 + n_tables]
    o_ref, a_scr = refs[3 + n_tables:]
    j = pl.program_id(1)

    @pl.when(j == 0)
    def _():
        a_scr[...] = _rms(x_ref[...], g_ref[...]).astype(BF16)

    y = _dot(a_scr[...], w_ref[...])
    tn = y.shape[1]
    if epilogue == "plain":
        o_ref[...] = y.astype(o_ref.dtype)
    elif epilogue == "sigmoid":
        o_ref[...] = _sigmoid(y).astype(o_ref.dtype)
    elif epilogue == "rope64":
        cos, sin_lo, sin_hi = (t[...] for t in tables)
        for c in range(tn // LANES):
            yc = y[:, c * LANES:(c + 1) * LANES]
            out = (yc * cos + pltpu.roll(yc, LANES - 32, 1) * sin_lo
                   + pltpu.roll(yc, 32, 1) * sin_hi)
            o_ref[:, c * LANES:(c + 1) * LANES] = out.astype(o_ref.dtype)
    elif epilogue == "rope256":
        cos, sin = (t[...] for t in tables)
        scale = jnp.where(j >= n_q_tiles, RET_QK_DIM ** -0.5, 1.0).astype(F32)
        for c in range(tn // RET_QK_DIM):
            y1 = y[:, c * 256:c * 256 + 128]
            y2 = y[:, c * 256 + 128:(c + 1) * 256]
            o_ref[:, c * 256:c * 256 + 128] = ((y1 * cos - y2 * sin) * scale).astype(o_ref.dtype)
            o_ref[:, c * 256 + 128:(c + 1) * 256] = ((y2 * cos + y1 * sin) * scale).astype(o_ref.dtype)
    else:
        raise ValueError(epilogue)


def _norm_proj(x2d, g, w, *, tn, out_dtype, epilogue="plain", tables=(), n_q_tiles=0):
    n, d = x2d.shape
    n_out = w.shape[1]
    tm = ROW_TILE
    assert n % tm == 0 and n_out % tn == 0
    table_specs = [pl.BlockSpec((tm, t.shape[1]), lambda i, j: (i, 0)) for t in tables]
    return pl.pallas_call(
        functools.partial(_proj_kernel, epilogue=epilogue, n_tables=len(tables), n_q_tiles=n_q_tiles),
        grid=(n // tm, n_out // tn),
        in_specs=[pl.BlockSpec((tm, d), lambda i, j: (i, 0)),
                  pl.BlockSpec((1, d), lambda i, j: (0, 0)),
                  pl.BlockSpec((d, tn), lambda i, j: (0, j))] + table_specs,
        out_specs=pl.BlockSpec((tm, tn), lambda i, j: (i, j)),
        out_shape=jax.ShapeDtypeStruct((n, n_out), out_dtype),
        scratch_shapes=[pltpu.VMEM((tm, d), BF16)],
        compiler_params=_cparams("parallel", "arbitrary"),
        name="norm_proj_" + epilogue,
    )(x2d, g.reshape(1, d), w, *tables)


def _out_kernel(m_ref, w_ref, g_ref, h_ref, o_ref):
    y = _dot(m_ref[...], w_ref[...])
    o_ref[...] = h_ref[...] + _rms(y, g_ref[...])


def _proj_norm_residual(m2d, w, g, h2d):
    n, k = m2d.shape
    d = w.shape[1]
    tm = ROW_TILE
    return pl.pallas_call(
        _out_kernel,
        grid=(n // tm,),
        in_specs=[pl.BlockSpec((tm, k), lambda i: (i, 0)),
                  _resident((k, d)),
                  _resident((1, d)),
                  pl.BlockSpec((tm, d), lambda i: (i, 0))],
        out_specs=pl.BlockSpec((tm, d), lambda i: (i, 0)),
        out_shape=jax.ShapeDtypeStruct((n, d), F32),
        compiler_params=_cparams("parallel"),
        name="proj_norm_residual",
    )(m2d, w, g.reshape(1, d), h2d)


def _ffn_kernel(h_ref, g_in_ref, wg_ref, wu_ref, wd_ref, g_out_ref, o_ref, acc_ref):
    x = h_ref[...]
    a = _rms(x, g_in_ref[...]).astype(BF16)
    for c in range(D_FF // FFN_COL_CHUNK):
        cols = slice(c * FFN_COL_CHUNK, (c + 1) * FFN_COL_CHUNK)
        gate = _dot(a, wg_ref[:, cols])
        up = _dot(a, wu_ref[:, cols])
        act = (gate * _sigmoid(gate) * up).astype(BF16)
        part = _dot(act, wd_ref[cols, :])
        if c == 0:
            acc_ref[...] = part
        else:
            acc_ref[...] += part
    o_ref[...] = x + _rms(acc_ref[...], g_out_ref[...])


def _ffn(h2d, g_in, wg, wu, wd, g_out):
    n, d = h2d.shape
    tm = ROW_TILE
    return pl.pallas_call(
        _ffn_kernel,
        grid=(n // tm,),
        in_specs=[pl.BlockSpec((tm, d), lambda i: (i, 0)),
                  _resident((1, d)),
                  _resident((d, D_FF)),
                  _resident((d, D_FF)),
                  _resident((D_FF, d)),
                  _resident((1, d))],
        out_specs=pl.BlockSpec((tm, d), lambda i: (i, 0)),
        out_shape=jax.ShapeDtypeStruct((n, d), F32),
        scratch_shapes=[pltpu.VMEM((tm, d), F32)],
        compiler_params=_cparams("parallel"),
        name="swiglu_ffn",
    )(h2d, g_in.reshape(1, d), wg, wu, wd, g_out.reshape(1, d))


def _gmlp_kernel(h_ref, g_in_ref, win_ref, lng_ref, lnb_ref, ws_ref, bst_ref, wout_ref,
                 g_out_ref, o_ref, vn_scr, y_scr):
    x = h_ref[...]
    tm = x.shape[0]
    a = _rms(x, g_in_ref[...]).astype(BF16)

    v = _gelu_tanh(_dot(a, win_ref[:, GM_WIDTH:]))
    mu = jnp.mean(v, axis=-1, keepdims=True)
    vc = v - mu
    var = jnp.mean(vc * vc, axis=-1, keepdims=True)
    vn_scr[...] = (vc * lax.rsqrt(var + LN_EPS) * lng_ref[...] + lnb_ref[...]).astype(BF16)

    t_out = lax.broadcasted_iota(jnp.int32, (GM_CHUNK, GM_CHUNK), 0)
    t_in = lax.broadcasted_iota(jnp.int32, (GM_CHUNK, GM_CHUNK), 1)
    causal = t_in <= t_out
    bst = bst_ref[...]
    for grp in range(GM_GROUPS):
        cols = slice(grp * GM_GROUP_W, (grp + 1) * GM_GROUP_W)
        w_mix = jnp.where(causal, ws_ref[grp], 0.0).astype(BF16)
        bias = bst[:, grp:grp + 1]
        for c in range(tm // GM_CHUNK):
            rows = slice(c * GM_CHUNK, (c + 1) * GM_CHUNK)
            mixed = _dot(w_mix, vn_scr[rows, cols]) + bias
            u = _gelu_tanh(_dot(a[rows], win_ref[:, cols]))
            y_scr[rows, cols] = (u * mixed).astype(BF16)

    m = _dot(y_scr[...], wout_ref[...])
    o_ref[...] = x + _rms(m, g_out_ref[...])


def _gmlp_layer(h2d, g_in, w_in, ln_g, ln_b, w_s, b_s, w_out, g_out):
    n, d = h2d.shape
    tm = GM_ROW_TILE
    return pl.pallas_call(
        _gmlp_kernel,
        grid=(n // tm,),
        in_specs=[pl.BlockSpec((tm, d), lambda i: (i, 0)),
                  _resident((1, d)),
                  _resident((d, 2 * GM_WIDTH)),
                  _resident((1, GM_WIDTH)),
                  _resident((1, GM_WIDTH)),
                  _resident((GM_GROUPS, GM_CHUNK, GM_CHUNK)),
                  _resident((GM_CHUNK, GM_GROUPS)),
                  _resident((GM_WIDTH, d)),
                  _resident((1, d))],
        out_specs=pl.BlockSpec((tm, d), lambda i: (i, 0)),
        out_shape=jax.ShapeDtypeStruct((n, d), F32),
        scratch_shapes=[pltpu.VMEM((tm, GM_WIDTH), BF16), pltpu.VMEM((tm, GM_WIDTH), BF16)],
        compiler_params=_cparams("parallel"),
        name="gmlp_layer",
    )(h2d, g_in.reshape(1, d), w_in.astype(BF16), ln_g.reshape(1, -1), ln_b.reshape(1, -1),
      w_s, b_s.T, w_out.astype(BF16), g_out.reshape(1, d))


def _rope_tables(positions, half):
    inv = ROPE_THETA ** (-jnp.arange(half, dtype=F32) / half)
    ang = positions.astype(F32)[..., None] * inv
    return jnp.cos(ang), jnp.sin(ang)


def _cmp_kernel(f_ref, pe_ref, w1_ref, w2_ref, cos_ref, sin_ref, o_ref):
    half = CMP_LEN * NSA_HEAD_DIM // 2
    x = f_ref[0, 0, 0].astype(F32)
    pe = pe_ref[0]
    first = _dot((x + pe[:, :half]).astype(BF16), w1_ref[0, :half, :])
    second = _dot((x + pe[:, half:]).astype(BF16), w1_ref[0, half:, :])
    hid = _gelu_tanh(first + pltpu.roll(second, second.shape[0] - 1, 0))
    y = _dot(hid.astype(BF16), w2_ref[0])
    out = y[:, :NSA_HEAD_DIM] * cos_ref[0, 0] + y[:, NSA_HEAD_DIM:] * sin_ref[0, 0]
    o_ref[0, 0, 0] = out.astype(o_ref.dtype)


def _compress(f, pe_flat, w1, w2cat, cos_t, sin_t):
    _, b, g, n_grp, width = f.shape
    dk = NSA_HEAD_DIM
    return pl.pallas_call(
        _cmp_kernel,
        grid=(2, b, g),
        in_specs=[pl.BlockSpec((1, 1, 1, n_grp, width), lambda s, i, j: (s, i, j, 0, 0)),
                  pl.BlockSpec((1, 1, pe_flat.shape[-1]), lambda s, i, j: (s, 0, 0)),
                  pl.BlockSpec((1,) + w1.shape[1:], lambda s, i, j: (s, 0, 0)),
                  pl.BlockSpec((1,) + w2cat.shape[1:], lambda s, i, j: (s, 0, 0)),
                  pl.BlockSpec((1, 1, n_grp, dk), lambda s, i, j: (s, i, 0, 0)),
                  pl.BlockSpec((1, 1, n_grp, dk), lambda s, i, j: (s, i, 0, 0))],
        out_specs=pl.BlockSpec((1, 1, 1, n_grp, dk), lambda s, i, j: (s, i, j, 0, 0)),
        out_shape=jax.ShapeDtypeStruct((2, b, g, n_grp, dk), BF16),
        compiler_params=_cparams("parallel", "parallel", "parallel"),
        name="nsa_compress",
    )(f, pe_flat, w1, w2cat, cos_t, sin_t)


def _nsa_attn_kernel(q_ref, ks_ref, vst_ref, kw_ref, vwt_ref, kc_ref, vct_ref, gate_ref,
                     o_ref, os_scr):
    tq, tk, hpg, dk = ATT_TQ, ATT_TK, NSA_HPG, NSA_HEAD_DIM
    ratio = SEL_LEN // CMP_STRIDE
    ratio_shift = ratio.bit_length() - 1
    sel_shift = SEL_LEN.bit_length() - 1
    n_sel = LANES // ratio
    i = pl.program_id(2)
    t0 = i * tq

    def tile_heads(x):
        return jnp.concatenate([x] * hpg, axis=1)

    q_t = q_ref[...].astype(F32).T
    qt = jnp.concatenate([q_t[h * dk:(h + 1) * dk] for h in range(hpg)], axis=1)
    qt = (qt * dk ** -0.5).astype(BF16)

    def softmax_pv(parts):
        m = functools.reduce(jnp.maximum, [jnp.max(s, axis=0, keepdims=True) for s, _ in parts])
        acc, l = 0.0, 0.0
        for s, vt in parts:
            p = jnp.exp(s - m)
            l = l + jnp.sum(p, axis=0, keepdims=True)
            acc = acc + _dot(vt, p.astype(BF16))
        return acc * (1.0 / jnp.maximum(l, 1e-30))

    n_win = WIN + tq
    k_lo = pl.multiple_of(jnp.maximum(i - WIN // tk, 0) * tk, tk)
    kpos_w = k_lo + lax.broadcasted_iota(jnp.int32, (n_win, tq), 0)
    tok_w = t0 + lax.broadcasted_iota(jnp.int32, (n_win, tq), 1)
    bias_w = jnp.where((kpos_w <= tok_w) & (kpos_w > tok_w - WIN), 0.0, NEG_BIG)
    s_w = _dot(kw_ref[0, 0, pl.ds(k_lo, n_win), :], qt) + tile_heads(bias_w)
    o_w = softmax_pv([(s_w, vwt_ref[0, 0, :, pl.ds(k_lo, n_win)])])

    row4 = lax.broadcasted_iota(jnp.int32, (tk, hpg * tq), 0)
    tok4 = t0 + (lax.broadcasted_iota(jnp.int32, (tk, hpg * tq), 1) & (tq - 1))
    n_idx = ((row4 & (n_sel - 1)) << ratio_shift) + (row4 >> (n_sel.bit_length() - 1))
    mask_c = (n_idx * CMP_STRIDE + (CMP_LEN - 1)) <= tok4
    s_c = _dot(kc_ref[0, 0], qt)
    m_c = jnp.max(jnp.where(mask_c, s_c, NEG_BIG), axis=0, keepdims=True)
    e_c = jnp.where(mask_c, jnp.exp(s_c - m_c), 0.0)
    p_c = e_c * (1.0 / jnp.maximum(jnp.sum(e_c, axis=0, keepdims=True), 1e-30))
    o_c = _dot(vct_ref[0, 0], p_c.astype(BF16))

    p_sum = p_c[:, 0:tq]
    for h in range(1, hpg):
        p_sum = p_sum + p_c[:, h * tq:(h + 1) * tq]
    imp = p_sum[0:n_sel]
    for r in range(1, ratio):
        imp = imp + p_sum[r * n_sel:(r + 1) * n_sel]

    blk = lax.broadcasted_iota(jnp.int32, (n_sel, tq), 0)
    cur = (t0 + lax.broadcasted_iota(jnp.int32, (n_sel, tq), 1)) >> sel_shift
    forced = (blk == 0) | (blk == cur) | (blk == cur - 1)
    score = jnp.where(blk <= cur, jnp.where(forced, jnp.inf, imp), -jnp.inf)
    rank = jnp.zeros((n_sel, tq), F32)
    for k in range(1, n_sel):
        other = pltpu.roll(score, k, 0)
        beats = (other > score) | ((other == score) & (blk >= k))
        rank = rank + jnp.where(beats, 1.0, 0.0)
    sel_bias = jnp.where(rank < float(N_SELECT), 0.0, NEG_BIG).astype(BF16)

    pad_rows = ks_ref.shape[3] - dk - n_sel
    qt_aug = jnp.concatenate([qt, tile_heads(sel_bias), jnp.zeros((pad_rows, hpg * tq), BF16)], axis=0)
    tok_c = t0 + lax.broadcasted_iota(jnp.int32, (SEL_CHUNK, tq), 1)
    key_c = lax.broadcasted_iota(jnp.int32, (SEL_CHUNK, tq), 0)
    n_variants = ks_ref.shape[2] // SEL_CHUNK
    for v in range(n_variants):
        @pl.when((t0 >> (SEL_CHUNK.bit_length() - 1)) == v)
        def _():
            parts = []
            for c in range(v + 1):
                rows = slice(c * SEL_CHUNK, (c + 1) * SEL_CHUNK)
                s = _dot(ks_ref[0, 0, rows, :], qt_aug)
                if c == v:
                    s = s + tile_heads(jnp.where(c * SEL_CHUNK + key_c <= tok_c, 0.0, NEG_BIG))
                parts.append((s, vst_ref[0, 0, :, rows]))
            os_scr[...] = softmax_pv(parts)
    o_s = os_scr[...]

    gates = gate_ref[0, 0, 0]
    out_t = gates[0:1] * o_c + gates[1:2] * o_s + gates[2:3] * o_w
    stacked = jnp.concatenate([out_t[:, h * tq:(h + 1) * tq] for h in range(hpg)], axis=0)
    o_ref[...] = stacked.T.astype(o_ref.dtype)


def _nsa_attention(roped, ks_aug, vst, kw, vwt, kc, vct, gates):
    b, g, s, dk = kw.shape
    hpg = NSA_HPG
    n_cmp = kc.shape[2]
    tq = ATT_TQ
    nt = s // tq
    assert s % SEL_CHUNK == 0 and SEL_CHUNK % tq == 0 and s >= WIN + tq
    k_spec = pl.BlockSpec((1, 1, s, dk), lambda bi, gi, i: (bi, gi, 0, 0))
    vt_spec = pl.BlockSpec((1, 1, dk, s), lambda bi, gi, i: (bi, gi, 0, 0))
    return pl.pallas_call(
        _nsa_attn_kernel,
        grid=(b, g, nt),
        in_specs=[pl.BlockSpec((tq, hpg * dk), lambda bi, gi, i: (bi * nt + i, gi)),
                  pl.BlockSpec((1, 1, s, ks_aug.shape[3]), lambda bi, gi, i: (bi, gi, 0, 0)),
                  vt_spec, k_spec, vt_spec,
                  pl.BlockSpec((1, 1, n_cmp, dk), lambda bi, gi, i: (bi, gi, 0, 0)),
                  pl.BlockSpec((1, 1, dk, n_cmp), lambda bi, gi, i: (bi, gi, 0, 0)),
                  pl.BlockSpec((1, 1, 1, 3, hpg * tq), lambda bi, gi, i: (bi, gi, i, 0, 0))],
        out_specs=pl.BlockSpec((tq, hpg * dk), lambda bi, gi, i: (bi * nt + i, gi)),
        out_shape=jax.ShapeDtypeStruct((b * s, g * hpg * dk), BF16),
        scratch_shapes=[pltpu.VMEM((dk, hpg * tq), F32)],
        compiler_params=_cparams("parallel", "parallel", "parallel"),
        name="nsa_attention",
    )(roped, ks_aug, vst, kw, vwt, kc, vct, gates)


def _nsa_layer(h2d, positions, g_in, w_in, cmp_pe, cmp_w1, cmp_w2, w_out, g_out):
    b, s = positions.shape
    n, d = h2d.shape
    heads, grp, hpg, dk = NSA_HEADS, NSA_KV_GROUPS, NSA_HPG, NSA_HEAD_DIM
    half = dk // 2
    qw = heads * dk
    col = lambda k: w_in[:, qw + k * NSA_KV: qw + (k + 1) * NSA_KV]
    w_rope = jnp.concatenate([w_in[:, :qw], col(2), col(4)], axis=1).astype(BF16)
    w_plain = jnp.concatenate([col(0), col(1), col(3), col(5)], axis=1).astype(BF16)
    w_gate = jnp.pad(w_in[:, qw + 6 * NSA_KV:], ((0, 0), (0, LANES - 3 * heads))).astype(BF16)

    cos, sin = _rope_tables(positions, half)
    zeros = jnp.zeros_like(sin)
    lane_tile = lambda t: jnp.tile(t, (1, 1, LANES // t.shape[-1])).reshape(n, LANES)
    cos_t = lane_tile(cos)
    sin_lo = lane_tile(jnp.concatenate([-sin, zeros], axis=-1))
    sin_hi = lane_tile(jnp.concatenate([zeros, sin], axis=-1))

    roped = _norm_proj(h2d, g_in, w_rope, tn=512, out_dtype=BF16, epilogue="rope64",
                       tables=(cos_t, sin_lo, sin_hi))
    plain = _norm_proj(h2d, g_in, w_plain, tn=512, out_dtype=BF16)
    gates = _norm_proj(h2d, g_in, w_gate, tn=LANES, out_dtype=F32, epilogue="sigmoid")

    to_k = lambda t: t.reshape(b, s, grp, dk).transpose(0, 2, 1, 3)
    to_vt = lambda t: t.reshape(b, s, grp, dk).transpose(0, 2, 3, 1)
    kw = to_k(roped[:, qw + NSA_KV:])
    onehot = (jnp.arange(s)[:, None] // SEL_LEN == jnp.arange(LANES - dk)[None, :]).astype(BF16)
    ks_aug = jnp.concatenate([to_k(roped[:, qw:qw + NSA_KV]),
                              jnp.broadcast_to(onehot, (b, grp, s, LANES - dk))], axis=-1)
    kc, vc = (to_k(plain[:, k * NSA_KV:(k + 1) * NSA_KV]) for k in range(2))
    vst, vwt = (to_vt(plain[:, k * NSA_KV:(k + 1) * NSA_KV]) for k in range(2, 4))
    nt = s // ATT_TQ
    gates = (gates[:, :3 * heads].reshape(b, nt, ATT_TQ, grp, hpg, 3)
             .transpose(0, 3, 1, 5, 4, 2).reshape(b, grp, nt, 3, hpg * ATT_TQ))

    n_grp = s // CMP_STRIDE
    f = jnp.stack([kc, vc]).reshape(2, b, grp, n_grp, CMP_STRIDE * dk)
    pe_flat = cmp_pe.reshape(2, 1, CMP_LEN * dk)
    rot = lambda w: jnp.concatenate([-w[..., half:], w[..., :half]], axis=-1)
    w2cat = jnp.concatenate([cmp_w2, rot(cmp_w2)], axis=-1).astype(BF16)
    n_cmp = (s - CMP_LEN) // CMP_STRIDE + 1
    pad = ((0, 0), (0, n_grp - n_cmp), (0, 0))
    cos_e = jnp.pad(cos[:, CMP_LEN - 1::CMP_STRIDE], pad, constant_values=1.0)
    sin_e = jnp.pad(sin[:, CMP_LEN - 1::CMP_STRIDE], pad)
    cos_c = jnp.stack([jnp.concatenate([cos_e, cos_e], -1), jnp.ones((b, n_grp, dk), F32)])
    sin_c = jnp.stack([jnp.concatenate([sin_e, sin_e], -1), jnp.zeros((b, n_grp, dk), F32)])
    cmp = _compress(f, pe_flat, cmp_w1.astype(BF16), w2cat, cos_c, sin_c)
    ratio = SEL_LEN // CMP_STRIDE
    cmp = (cmp.reshape(2, b, grp, n_grp // ratio, ratio, dk).transpose(0, 1, 2, 4, 3, 5)
           .reshape(2, b, grp, n_grp, dk))

    o2d = _nsa_attention(roped, ks_aug, vst, kw, vwt, cmp[0], cmp[1].transpose(0, 1, 3, 2), gates)
    return _proj_norm_residual(o2d, w_out.astype(BF16), g_out, h2d)


def _ret_kernel(q_ref, k_ref, v_ref, g_ref, gn_ref, decay_ref, xi_ref, zeta_ref, gc_ref,
                o_ref, state_scr):
    @pl.when(pl.program_id(2) == 0)
    def _():
        state_scr[...] = jnp.zeros(state_scr.shape, F32)

    q = q_ref[...]
    k = k_ref[...]
    v = v_ref[...]
    s = lax.dot_general(q, k, (((1,), (1,)), ((), ())), preferred_element_type=F32)
    inner = _dot((s * decay_ref[0]).astype(BF16), v)
    state = state_scr[...]
    cross = _dot(q, state.astype(BF16)) * xi_ref[0]
    kz = (k.astype(F32) * zeta_ref[0]).T.astype(BF16)
    state_scr[...] = gc_ref[0] * state + _dot(kz, v)

    o = inner + cross
    mu = jnp.mean(o, axis=-1, keepdims=True)
    oc = o - mu
    var = jnp.mean(oc * oc, axis=-1, keepdims=True)
    on = oc * lax.rsqrt(var + LN_EPS) * gn_ref[...]
    gate = g_ref[...].astype(F32)
    o_ref[...] = (gate * _sigmoid(gate) * on).astype(o_ref.dtype)


def _retention_core(qk, vg, gn_g, b, s):
    heads, dk, dv, c = RET_HEADS, RET_QK_DIM, RET_V_DIM, RET_CHUNK
    n_ch = s // c
    log_g = jnp.log1p(-(2.0 ** (-5.0 - jnp.arange(heads, dtype=F32))))
    ix = jnp.arange(c, dtype=F32)
    rel = ix[:, None] - ix[None, :]
    decay = jnp.where(rel >= 0, jnp.exp(log_g[:, None, None] * jnp.maximum(rel, 0.0)), 0.0)
    xi = jnp.exp(log_g[:, None] * (ix + 1.0))[:, :, None]
    zeta = jnp.exp(log_g[:, None] * (c - 1.0 - ix))[:, :, None]
    g_chunk = jnp.exp(log_g * c)[:, None, None]
    tok = lambda bi, hi, ci: bi * n_ch + ci
    return pl.pallas_call(
        _ret_kernel,
        grid=(b, heads, n_ch),
        in_specs=[pl.BlockSpec((c, dk), lambda bi, hi, ci: (tok(bi, hi, ci), hi)),
                  pl.BlockSpec((c, dk), lambda bi, hi, ci: (tok(bi, hi, ci), heads + hi)),
                  pl.BlockSpec((c, dv), lambda bi, hi, ci: (tok(bi, hi, ci), hi)),
                  pl.BlockSpec((c, dv), lambda bi, hi, ci: (tok(bi, hi, ci), heads + hi)),
                  pl.BlockSpec((1, dv), lambda bi, hi, ci: (0, hi)),
                  pl.BlockSpec((1, c, c), lambda bi, hi, ci: (hi, 0, 0)),
                  pl.BlockSpec((1, c, 1), lambda bi, hi, ci: (hi, 0, 0)),
                  pl.BlockSpec((1, c, 1), lambda bi, hi, ci: (hi, 0, 0)),
                  pl.BlockSpec((1, 1, 1), lambda bi, hi, ci: (hi, 0, 0))],
        out_specs=pl.BlockSpec((c, dv), lambda bi, hi, ci: (tok(bi, hi, ci), hi)),
        out_shape=jax.ShapeDtypeStruct((b * s, heads * dv), BF16),
        scratch_shapes=[pltpu.VMEM((dk, dv), F32)],
        compiler_params=_cparams("parallel", "parallel", "arbitrary"),
        name="retention_core",
    )(qk, qk, vg, vg, gn_g.reshape(1, -1), decay, xi, zeta, g_chunk)


def _retention_layer(h2d, positions, g_in, w_in, gn_g, w_out, g_out):
    b, s = positions.shape
    n = b * s
    heads, dk, dv = RET_HEADS, RET_QK_DIM, RET_V_DIM
    w = w_in.astype(BF16)
    cos, sin = _rope_tables(positions, dk // 2)
    qk = _norm_proj(h2d, g_in, w[:, :2 * heads * dk], tn=512, out_dtype=BF16, epilogue="rope256",
                    tables=(cos.reshape(n, -1), sin.reshape(n, -1)), n_q_tiles=heads * dk // 512)
    vg = _norm_proj(h2d, g_in, w[:, 2 * heads * dk:], tn=1024, out_dtype=BF16)
    y = _retention_core(qk, vg, gn_g, b, s)
    return _proj_norm_residual(y, w_out.astype(BF16), g_out, h2d)


def kernel(x, positions, norm_g, ffn_w_gate, ffn_w_up, ffn_w_down, gm_w_in, gm_ln_g, gm_ln_b, gm_w_s, gm_b_s, gm_w_out, nsa_w_in, nsa_cmp_pe, nsa_cmp_w1, nsa_cmp_w2, nsa_w_out, ret_w_in, ret_gn_g, ret_w_out):
    b, s, d = x.shape
    h = x.reshape(b * s, d)
    for i in range(DEPTH):
        kind, j = i % N_MIXERS, i // N_MIXERS
        if kind == 0:
            h = _gmlp_layer(h, norm_g[i, 0], gm_w_in[j], gm_ln_g[j], gm_ln_b[j], gm_w_s[j],
                            gm_b_s[j], gm_w_out[j], norm_g[i, 1])
        elif kind == 1:
            h = _nsa_layer(h, positions, norm_g[i, 0], nsa_w_in[j], nsa_cmp_pe[j], nsa_cmp_w1[j],
                           nsa_cmp_w2[j], nsa_w_out[j], norm_g[i, 1])
        else:
            h = _retention_layer(h, positions, norm_g[i, 0], ret_w_in[j], ret_gn_g[j],
                                 ret_w_out[j], norm_g[i, 1])
        h = _ffn(h, norm_g[i, 2], ffn_w_gate[i].astype(BF16), ffn_w_up[i].astype(BF16),
                 ffn_w_down[i].astype(BF16), norm_g[i, 3])
    return h.reshape(b, s, d)
```

```python
import functools
import math

import jax
import jax.numpy as jnp
from jax import lax
from jax.experimental import pallas as pl
from jax.experimental.pallas import tpu as pltpu

F32 = jnp.float32
BF16 = jnp.bfloat16

D_MODEL = 1024
DEPTH = 4
N_MIXERS = 3
RMS_EPS = 1e-6
LN_EPS = 1e-5
ROPE_THETA = 10000.0
D_FF = 2816

GM_CHUNK = 128
GM_WIDTH = 2 * D_MODEL
GM_GROUPS = 8
GM_GROUP_W = GM_WIDTH // GM_GROUPS

NSA_HEADS = 16
NSA_KV_GROUPS = 4
NSA_HPG = NSA_HEADS // NSA_KV_GROUPS
NSA_HEAD_DIM = 64
NSA_KV = NSA_KV_GROUPS * NSA_HEAD_DIM
CMP_LEN = 32
CMP_STRIDE = 16
CMP_HIDDEN = 4 * NSA_HEAD_DIM
SEL_LEN = 64
N_SELECT = 8
WIN = 512

RET_HEADS = 4
RET_QK_DIM = 256
RET_V_DIM = 512
RET_CHUNK = 128

LANES = 128
BF16_SUBLANES = 16
LOG2_E = math.log2(math.e)
VMEM_LIMIT = 48 * 1024 * 1024
NEG_BIG = -1e30

ROW_TILE = 512
GM_ROW_TILE = 256
FFN_COL_CHUNK = 256
ATT_TQ = 256
ATT_TK = 128
SEL_CHUNK = 512


def _cparams(*sem):
    return pltpu.CompilerParams(dimension_semantics=sem, vmem_limit_bytes=VMEM_LIMIT)


def _resident(shape):
    nd = len(shape)
    return pl.BlockSpec(shape, lambda *_: (0,) * nd, pipeline_mode=pl.Buffered(1))


def _rms(x32, g):
    ms = jnp.mean(x32 * x32, axis=-1, keepdims=True)
    return x32 * lax.rsqrt(ms + RMS_EPS) * g


def _gelu_tanh(x):
    c = math.sqrt(2.0 / math.pi)
    return x * (0.5 * (1.0 + jnp.tanh(c * (x + 0.044715 * (x * x * x)))))


def _sigmoid(x):
    return 1.0 / (1.0 + jnp.exp(-x))


def _dot(a, b):
    return jnp.dot(a, b, preferred_element_type=F32)


def _proj_kernel(*refs, epilogue, n_tables, n_q_tiles):
    x_ref, g_ref, w_ref = refs[:3]
    tables = refs[3:3 + n_tables]
    o_ref, a_scr = refs[3 + n_tables:]
    j = pl.program_id(1)

    @pl.when(j == 0)
    def _():
        a_scr[...] = _rms(x_ref[...], g_ref[...]).astype(BF16)

    y = _dot(a_scr[...], w_ref[...])
    tn = y.shape[1]
    if epilogue == "plain":
        o_ref[...] = y.astype(o_ref.dtype)
    elif epilogue == "sigmoid":
        o_ref[...] = _sigmoid(y).astype(o_ref.dtype)
    elif epilogue == "rope64":
        cos, sin_lo, sin_hi = (t[...] for t in tables)
        for c in range(tn // LANES):
            yc = y[:, c * LANES:(c + 1) * LANES]
            out = (yc * cos + pltpu.roll(yc, LANES - 32, 1) * sin_lo
                   + pltpu.roll(yc, 32, 1) * sin_hi)
            o_ref[:, c * LANES:(c + 1) * LANES] = out.astype(o_ref.dtype)
    elif epilogue == "rope256":
        cos, sin = (t[...] for t in tables)
        scale = jnp.where(j >= n_q_tiles, RET_QK_DIM ** -0.5, 1.0).astype(F32)
        for c in range(tn // RET_QK_DIM):
            y1 = y[:, c * 256:c * 256 + 128]
            y2 = y[:, c * 256 + 128:(c + 1) * 256]
            o_ref[:, c * 256:c * 256 + 128] = ((y1 * cos - y2 * sin) * scale).astype(o_ref.dtype)
            o_ref[:, c * 256 + 128:(c + 1) * 256] = ((y2 * cos + y1 * sin) * scale).astype(o_ref.dtype)
    else:
        raise ValueError(epilogue)


def _norm_proj(x2d, g, w, *, tn, out_dtype, epilogue="plain", tables=(), n_q_tiles=0):
    n, d = x2d.shape
    n_out = w.shape[1]
    tm = ROW_TILE
    assert n % tm == 0 and n_out % tn == 0
    table_specs = [pl.BlockSpec((tm, t.shape[1]), lambda i, j: (i, 0)) for t in tables]
    return pl.pallas_call(
        functools.partial(_proj_kernel, epilogue=epilogue, n_tables=len(tables), n_q_tiles=n_q_tiles),
        grid=(n // tm, n_out // tn),
        in_specs=[pl.BlockSpec((tm, d), lambda i, j: (i, 0)),
                  pl.BlockSpec((1, d), lambda i, j: (0, 0)),
                  pl.BlockSpec((d, tn), lambda i, j: (0, j))] + table_specs,
        out_specs=pl.BlockSpec((tm, tn), lambda i, j: (i, j)),
        out_shape=jax.ShapeDtypeStruct((n, n_out), out_dtype),
        scratch_shapes=[pltpu.VMEM((tm, d), BF16)],
        compiler_params=_cparams("parallel", "arbitrary"),
        name="norm_proj_" + epilogue,
    )(x2d, g.reshape(1, d), w, *tables)


def _out_kernel(m_ref, w_ref, g_ref, h_ref, o_ref):
    y = _dot(m_ref[...], w_ref[...])
    o_ref[...] = h_ref[...] + _rms(y, g_ref[...])


def _proj_norm_residual(m2d, w, g, h2d):
    n, k = m2d.shape
    d = w.shape[1]
    tm = ROW_TILE
    return pl.pallas_call(
        _out_kernel,
        grid=(n // tm,),
        in_specs=[pl.BlockSpec((tm, k), lambda i: (i, 0)),
                  _resident((k, d)),
                  _resident((1, d)),
                  pl.BlockSpec((tm, d), lambda i: (i, 0))],
        out_specs=pl.BlockSpec((tm, d), lambda i: (i, 0)),
        out_shape=jax.ShapeDtypeStruct((n, d), F32),
        compiler_params=_cparams("parallel"),
        name="proj_norm_residual",
    )(m2d, w, g.reshape(1, d), h2d)


def _ffn_kernel(h_ref, g_in_ref, wg_ref, wu_ref, wd_ref, g_out_ref, o_ref, acc_ref):
    x = h_ref[...]
    a = _rms(x, g_in_ref[...]).astype(BF16)
    for c in range(D_FF // FFN_COL_CHUNK):
        cols = slice(c * FFN_COL_CHUNK, (c + 1) * FFN_COL_CHUNK)
        gate = _dot(a, wg_ref[:, cols])
        up = _dot(a, wu_ref[:, cols])
        act = (gate * _sigmoid(gate) * up).astype(BF16)
        part = _dot(act, wd_ref[cols, :])
        if c == 0:
            acc_ref[...] = part
        else:
            acc_ref[...] += part
    o_ref[...] = x + _rms(acc_ref[...], g_out_ref[...])


def _ffn(h2d, g_in, wg, wu, wd, g_out):
    n, d = h2d.shape
    tm = ROW_TILE
    return pl.pallas_call(
        _ffn_kernel,
        grid=(n // tm,),
        in_specs=[pl.BlockSpec((tm, d), lambda i: (i, 0)),
                  _resident((1, d)),
                  _resident((d, D_FF)),
                  _resident((d, D_FF)),
                  _resident((D_FF, d)),
                  _resident((1, d))],
        out_specs=pl.BlockSpec((tm, d), lambda i: (i, 0)),
        out_shape=jax.ShapeDtypeStruct((n, d), F32),
        scratch_shapes=[pltpu.VMEM((tm, d), F32)],
        compiler_params=_cparams("parallel"),
        name="swiglu_ffn",
    )(h2d, g_in.reshape(1, d), wg, wu, wd, g_out.reshape(1, d))


def _gmlp_kernel(h_ref, g_in_ref, win_ref, lng_ref, lnb_ref, ws_ref, bst_ref, wout_ref,
                 g_out_ref, o_ref, vn_scr, y_scr):
    x = h_ref[...]
    tm = x.shape[0]
    a = _rms(x, g_in_ref[...]).astype(BF16)

    v = _gelu_tanh(_dot(a, win_ref[:, GM_WIDTH:]))
    mu = jnp.mean(v, axis=-1, keepdims=True)
    vc = v - mu
    var = jnp.mean(vc * vc, axis=-1, keepdims=True)
    vn_scr[...] = (vc * lax.rsqrt(var + LN_EPS) * lng_ref[...] + lnb_ref[...]).astype(BF16)

    t_out = lax.broadcasted_iota(jnp.int32, (GM_CHUNK, GM_CHUNK), 0)
    t_in = lax.broadcasted_iota(jnp.int32, (GM_CHUNK, GM_CHUNK), 1)
    causal = t_in <= t_out
    bst = bst_ref[...]
    for grp in range(GM_GROUPS):
        cols = slice(grp * GM_GROUP_W, (grp + 1) * GM_GROUP_W)
        w_mix = jnp.where(causal, ws_ref[grp], 0.0).astype(BF16)
        bias = bst[:, grp:grp + 1]
        for c in range(tm // GM_CHUNK):
            rows = slice(c * GM_CHUNK, (c + 1) * GM_CHUNK)
            mixed = _dot(w_mix, vn_scr[rows, cols]) + bias
            u = _gelu_tanh(_dot(a[rows], win_ref[:, cols]))
            y_scr[rows, cols] = (u * mixed).astype(BF16)

    m = _dot(y_scr[...], wout_ref[...])
    o_ref[...] = x + _rms(m, g_out_ref[...])


def _gmlp_layer(h2d, g_in, w_in, ln_g, ln_b, w_s, b_s, w_out, g_out):
    n, d = h2d.shape
    tm = GM_ROW_TILE
    return pl.pallas_call(
        _gmlp_kernel,
        grid=(n // tm,),
        in_specs=[pl.BlockSpec((tm, d), lambda i: (i, 0)),
                  _resident((1, d)),
                  _resident((d, 2 * GM_WIDTH)),
                  _resident((1, GM_WIDTH)),
                  _resident((1, GM_WIDTH)),
                  _resident((GM_GROUPS, GM_CHUNK, GM_CHUNK)),
                  _resident((GM_CHUNK, GM_GROUPS)),
                  _resident((GM_WIDTH, d)),
                  _resident((1, d))],
        out_specs=pl.BlockSpec((tm, d), lambda i: (i, 0)),
        out_shape=jax.ShapeDtypeStruct((n, d), F32),
        scratch_shapes=[pltpu.VMEM((tm, GM_WIDTH), BF16), pltpu.VMEM((tm, GM_WIDTH), BF16)],
        compiler_params=_cparams("parallel"),
        name="gmlp_layer",
    )(h2d, g_in.reshape(1, d), w_in.astype(BF16), ln_g.reshape(1, -1), ln_b.reshape(1, -1),
      w_s, b_s.T, w_out.astype(BF16), g_out.reshape(1, d))


def _rope_tables(positions, half):
    inv = ROPE_THETA ** (-jnp.arange(half, dtype=F32) / half)
    ang = positions.astype(F32)[..., None] * inv
    return jnp.cos(ang), jnp.sin(ang)


def _cmp_kernel(f_ref, pe_ref, w1_ref, w2_ref, cos_ref, sin_ref, o_ref):
    half = CMP_LEN * NSA_HEAD_DIM // 2
    x = f_ref[0, 0, 0].astype(F32)
    pe = pe_ref[0]
    first = _dot((x + pe[:, :half]).astype(BF16), w1_ref[0, :half, :])
    second = _dot((x + pe[:, half:]).astype(BF16), w1_ref[0, half:, :])
    hid = _gelu_tanh(first + pltpu.roll(second, second.shape[0] - 1, 0))
    y = _dot(hid.astype(BF16), w2_ref[0])
    out = y[:, :NSA_HEAD_DIM] * cos_ref[0, 0] + y[:, NSA_HEAD_DIM:] * sin_ref[0, 0]
    o_ref[0, 0, 0] = out.astype(o_ref.dtype)


def _compress(f, pe_flat, w1, w2cat, cos_t, sin_t):
    _, b, g, n_grp, width = f.shape
    dk = NSA_HEAD_DIM
    return pl.pallas_call(
        _cmp_kernel,
        grid=(2, b, g),
        in_specs=[pl.BlockSpec((1, 1, 1, n_grp, width), lambda s, i, j: (s, i, j, 0, 0)),
                  pl.BlockSpec((1, 1, pe_flat.shape[-1]), lambda s, i, j: (s, 0, 0)),
                  pl.BlockSpec((1,) + w1.shape[1:], lambda s, i, j: (s, 0, 0)),
                  pl.BlockSpec((1,) + w2cat.shape[1:], lambda s, i, j: (s, 0, 0)),
                  pl.BlockSpec((1, 1, n_grp, dk), lambda s, i, j: (s, i, 0, 0)),
                  pl.BlockSpec((1, 1, n_grp, dk), lambda s, i, j: (s, i, 0, 0))],
        out_specs=pl.BlockSpec((1, 1, 1, n_grp, dk), lambda s, i, j: (s, i, j, 0, 0)),
        out_shape=jax.ShapeDtypeStruct((2, b, g, n_grp, dk), BF16),
        compiler_params=_cparams("parallel", "parallel", "parallel"),
        name="nsa_compress",
    )(f, pe_flat, w1, w2cat, cos_t, sin_t)


def _nsa_attn_kernel(q_ref, ks_ref, vst_ref, kw_ref, vwt_ref, kc_ref, vct_ref, gate_ref,
                     o_ref, os_scr, score_scr):
    tq, tk, hpg, dk = ATT_TQ, ATT_TK, NSA_HPG, NSA_HEAD_DIM
    ratio = SEL_LEN // CMP_STRIDE
    ratio_shift = ratio.bit_length() - 1
    sel_shift = SEL_LEN.bit_length() - 1
    n_sel = LANES // ratio
    i = pl.program_id(2)
    t0 = i * tq

    def tile_heads(x):
        return jnp.concatenate([x] * hpg, axis=1)

    q_t = q_ref[...].astype(F32).T
    qt = jnp.concatenate([q_t[h * dk:(h + 1) * dk] for h in range(hpg)], axis=1)
    qt = (qt * (dk ** -0.5 * LOG2_E)).astype(BF16)

    def softmax_pv(parts):
        maxes, accs = [], []
        for s, vt in parts:
            m_part = jnp.max(s, axis=0, keepdims=True)
            p = jnp.exp2(s - m_part).astype(BF16)
            vt_ones = jnp.concatenate([vt, jnp.ones((BF16_SUBLANES, vt.shape[1]), BF16)], axis=0)
            maxes.append(m_part)
            accs.append(_dot(vt_ones, p))
        m = functools.reduce(jnp.maximum, maxes)
        acc = sum(jnp.exp2(m_part - m) * a for m_part, a in zip(maxes, accs))
        return acc[:dk] * (1.0 / jnp.maximum(acc[dk:dk + 1], 1e-30))

    n_win = WIN + tq
    k_lo = pl.multiple_of(jnp.maximum(t0 - WIN, 0), tk)
    n_half = n_win // 2
    row_minus_col = (lax.broadcasted_iota(jnp.int32, (n_half, tq), 0)
                     - lax.broadcasted_iota(jnp.int32, (n_half, tq), 1))
    parts_w = []
    for c in range(2):
        k0 = pl.multiple_of(k_lo + c * n_half, tk)
        back = lax.bitcast_convert_type((t0 - k0) - row_minus_col, jnp.uint32)
        bias_w = jnp.where(back < jnp.uint32(WIN), 0.0, NEG_BIG)
        s_w = _dot(kw_ref[0, 0, pl.ds(k0, n_half), :], qt) + tile_heads(bias_w)
        parts_w.append((s_w, vwt_ref[0, 0, :, pl.ds(k0, n_half)]))
    o_w = softmax_pv(parts_w)

    n_cmp = kc_ref.shape[2]
    row_c = lax.broadcasted_iota(jnp.int32, (n_cmp, tq), 0)
    tok_c = t0 + lax.broadcasted_iota(jnp.int32, (n_cmp, tq), 1)
    n_idx = ((row_c & (n_sel - 1)) << ratio_shift) + (row_c >> (n_sel.bit_length() - 1))
    bias_c = jnp.where(n_idx * CMP_STRIDE + (CMP_LEN - 1) <= tok_c, 0.0, NEG_BIG)
    s_c = _dot(kc_ref[0, 0], qt) + tile_heads(bias_c)
    e_c = jnp.exp2(s_c - jnp.max(s_c, axis=0, keepdims=True))
    tok_row = t0 + (lax.broadcasted_iota(jnp.int32, (1, hpg * tq), 1) & (tq - 1))
    inv_l = jnp.where(tok_row >= CMP_LEN - 1,
                      1.0 / jnp.maximum(jnp.sum(e_c, axis=0, keepdims=True), 1e-30), 0.0)
    p_c = e_c * inv_l
    o_c = _dot(vct_ref[0, 0], p_c.astype(BF16))

    p_sum = p_c[:, 0:tq]
    for h in range(1, hpg):
        p_sum = p_sum + p_c[:, h * tq:(h + 1) * tq]
    imp = p_sum[0:n_sel]
    for r in range(1, ratio):
        imp = imp + p_sum[r * n_sel:(r + 1) * n_sel]

    blk = lax.broadcasted_iota(jnp.int32, (n_sel, tq), 0)
    cur = (t0 + lax.broadcasted_iota(jnp.int32, (n_sel, tq), 1)) >> sel_shift
    forced = (blk == 0) | (blk == cur) | (blk == cur - 1)
    score = jnp.where(blk <= cur, jnp.where(forced, jnp.inf, imp), -jnp.inf)
    score_scr[...] = score
    grp_rows = 8
    row_in_grp = lax.broadcasted_iota(jnp.int32, (grp_rows, tq), 0)
    groups = [score[a:a + grp_rows] for a in range(0, n_sel, grp_rows)]
    ranks = [jnp.zeros((grp_rows, tq), F32) for _ in groups]
    for rival in range(n_sel):
        other = score_scr[rival:rival + 1, :]
        for a, mine in enumerate(groups):
            lo = a * grp_rows
            if rival < lo:
                beats = other >= mine
            elif rival >= lo + grp_rows:
                beats = other > mine
            else:
                beats = (other > mine) | ((other == mine) & (row_in_grp + lo > rival))
            ranks[a] = ranks[a] + jnp.where(beats, 1.0, 0.0)
    rank = jnp.concatenate(ranks, axis=0)
    sel_bias = jnp.where(rank < float(N_SELECT), 0.0, NEG_BIG).astype(BF16)

    pad_rows = ks_ref.shape[3] - dk - n_sel
    qt_aug = jnp.concatenate([qt, tile_heads(sel_bias), jnp.zeros((pad_rows, hpg * tq), BF16)], axis=0)
    tok_s = t0 + lax.broadcasted_iota(jnp.int32, (SEL_CHUNK, tq), 1)
    key_s = lax.broadcasted_iota(jnp.int32, (SEL_CHUNK, tq), 0)
    n_variants = ks_ref.shape[2] // SEL_CHUNK
    for v in range(n_variants):
        @pl.when((t0 >> (SEL_CHUNK.bit_length() - 1)) == v)
        def _():
            parts = []
            for c in range(v + 1):
                rows = slice(c * SEL_CHUNK, (c + 1) * SEL_CHUNK)
                s = _dot(ks_ref[0, 0, rows, :], qt_aug)
                if c == v:
                    s = s + tile_heads(jnp.where(c * SEL_CHUNK + key_s <= tok_s, 0.0, NEG_BIG))
                parts.append((s, vst_ref[0, 0, :, rows]))
            os_scr[...] = softmax_pv(parts)
    o_s = os_scr[...]

    gates = gate_ref[0, 0, 0]
    out_t = gates[0:1] * o_c + gates[1:2] * o_s + gates[2:3] * o_w
    stacked = jnp.concatenate([out_t[:, h * tq:(h + 1) * tq] for h in range(hpg)], axis=0)
    o_ref[...] = stacked.T.astype(o_ref.dtype)


def _nsa_attention(roped, ks_aug, vst, kw, vwt, kc, vct, gates):
    b, g, s, dk = kw.shape
    hpg = NSA_HPG
    n_cmp = kc.shape[2]
    tq = ATT_TQ
    nt = s // tq
    assert s % SEL_CHUNK == 0 and SEL_CHUNK % tq == 0 and s >= WIN + tq
    k_spec = pl.BlockSpec((1, 1, s, dk), lambda bi, gi, i: (bi, gi, 0, 0))
    vt_spec = pl.BlockSpec((1, 1, dk, s), lambda bi, gi, i: (bi, gi, 0, 0))
    return pl.pallas_call(
        _nsa_attn_kernel,
        grid=(b, g, nt),
        in_specs=[pl.BlockSpec((tq, hpg * dk), lambda bi, gi, i: (bi * nt + i, gi)),
                  pl.BlockSpec((1, 1, s, ks_aug.shape[3]), lambda bi, gi, i: (bi, gi, 0, 0)),
                  vt_spec, k_spec, vt_spec,
                  pl.BlockSpec((1, 1, n_cmp, dk), lambda bi, gi, i: (bi, gi, 0, 0)),
                  pl.BlockSpec((1, 1, dk, n_cmp), lambda bi, gi, i: (bi, gi, 0, 0)),
                  pl.BlockSpec((1, 1, 1, 3, hpg * tq), lambda bi, gi, i: (bi, gi, i, 0, 0))],
        out_specs=pl.BlockSpec((tq, hpg * dk), lambda bi, gi, i: (bi * nt + i, gi)),
        out_shape=jax.ShapeDtypeStruct((b * s, g * hpg * dk), BF16),
        scratch_shapes=[pltpu.VMEM((dk, hpg * tq), F32), pltpu.VMEM((n_cmp * CMP_STRIDE // SEL_LEN, tq), F32)],
        compiler_params=_cparams("parallel", "parallel", "parallel"),
        name="nsa_attention",
    )(roped, ks_aug, vst, kw, vwt, kc, vct, gates)


def _nsa_layer(h2d, positions, g_in, w_in, cmp_pe, cmp_w1, cmp_w2, w_out, g_out):
    b, s = positions.shape
    n, d = h2d.shape
    heads, grp, hpg, dk = NSA_HEADS, NSA_KV_GROUPS, NSA_HPG, NSA_HEAD_DIM
    half = dk // 2
    qw = heads * dk
    col = lambda k: w_in[:, qw + k * NSA_KV: qw + (k + 1) * NSA_KV]
    w_rope = jnp.concatenate([w_in[:, :qw], col(2), col(4)], axis=1).astype(BF16)
    w_plain = jnp.concatenate([col(0), col(1), col(3), col(5)], axis=1).astype(BF16)
    w_gate = jnp.pad(w_in[:, qw + 6 * NSA_KV:], ((0, 0), (0, LANES - 3 * heads))).astype(BF16)

    cos, sin = _rope_tables(positions, half)
    zeros = jnp.zeros_like(sin)
    lane_tile = lambda t: jnp.tile(t, (1, 1, LANES // t.shape[-1])).reshape(n, LANES)
    cos_t = lane_tile(cos)
    sin_lo = lane_tile(jnp.concatenate([-sin, zeros], axis=-1))
    sin_hi = lane_tile(jnp.concatenate([zeros, sin], axis=-1))

    roped = _norm_proj(h2d, g_in, w_rope, tn=512, out_dtype=BF16, epilogue="rope64",
                       tables=(cos_t, sin_lo, sin_hi))
    plain = _norm_proj(h2d, g_in, w_plain, tn=512, out_dtype=BF16)
    gates = _norm_proj(h2d, g_in, w_gate, tn=LANES, out_dtype=F32, epilogue="sigmoid")

    to_k = lambda t: t.reshape(b, s, grp, dk).transpose(0, 2, 1, 3)
    to_vt = lambda t: t.reshape(b, s, grp, dk).transpose(0, 2, 3, 1)
    kw = to_k(roped[:, qw + NSA_KV:])
    onehot = (jnp.arange(s)[:, None] // SEL_LEN == jnp.arange(LANES - dk)[None, :]).astype(BF16)
    ks_aug = jnp.concatenate([to_k(roped[:, qw:qw + NSA_KV]),
                              jnp.broadcast_to(onehot, (b, grp, s, LANES - dk))], axis=-1)
    kc, vc = (to_k(plain[:, k * NSA_KV:(k + 1) * NSA_KV]) for k in range(2))
    vst, vwt = (to_vt(plain[:, k * NSA_KV:(k + 1) * NSA_KV]) for k in range(2, 4))
    nt = s // ATT_TQ
    gates = (gates[:, :3 * heads].reshape(b, nt, ATT_TQ, grp, hpg, 3)
             .transpose(0, 3, 1, 5, 4, 2).reshape(b, grp, nt, 3, hpg * ATT_TQ))

    n_grp = s // CMP_STRIDE
    f = jnp.stack([kc, vc]).reshape(2, b, grp, n_grp, CMP_STRIDE * dk)
    pe_flat = cmp_pe.reshape(2, 1, CMP_LEN * dk)
    rot = lambda w: jnp.concatenate([-w[..., half:], w[..., :half]], axis=-1)
    w2cat = jnp.concatenate([cmp_w2, rot(cmp_w2)], axis=-1).astype(BF16)
    n_cmp = (s - CMP_LEN) // CMP_STRIDE + 1
    pad = ((0, 0), (0, n_grp - n_cmp), (0, 0))
    cos_e = jnp.pad(cos[:, CMP_LEN - 1::CMP_STRIDE], pad, constant_values=1.0)
    sin_e = jnp.pad(sin[:, CMP_LEN - 1::CMP_STRIDE], pad)
    cos_c = jnp.stack([jnp.concatenate([cos_e, cos_e], -1), jnp.ones((b, n_grp, dk), F32)])
    sin_c = jnp.stack([jnp.concatenate([sin_e, sin_e], -1), jnp.zeros((b, n_grp, dk), F32)])
    cmp = _compress(f, pe_flat, cmp_w1.astype(BF16), w2cat, cos_c, sin_c)
    ratio = SEL_LEN // CMP_STRIDE
    cmp = (cmp.reshape(2, b, grp, n_grp // ratio, ratio, dk).transpose(0, 1, 2, 4, 3, 5)
           .reshape(2, b, grp, n_grp, dk))

    o2d = _nsa_attention(roped, ks_aug, vst, kw, vwt, cmp[0], cmp[1].transpose(0, 1, 3, 2), gates)
    return _proj_norm_residual(o2d, w_out.astype(BF16), g_out, h2d)


def _ret_kernel(q_ref, k_ref, v_ref, g_ref, gn_ref, decay_ref, xi_ref, zeta_ref, gc_ref,
                o_ref, state_scr):
    @pl.when(pl.program_id(2) == 0)
    def _():
        state_scr[...] = jnp.zeros(state_scr.shape, F32)

    q = q_ref[...]
    k = k_ref[...]
    v = v_ref[...]
    s = lax.dot_general(q, k, (((1,), (1,)), ((), ())), preferred_element_type=F32)
    inner = _dot((s * decay_ref[0]).astype(BF16), v)
    state = state_scr[...]
    cross = _dot(q, state.astype(BF16)) * xi_ref[0]
    kz = (k.astype(F32) * zeta_ref[0]).T.astype(BF16)
    state_scr[...] = gc_ref[0] * state + _dot(kz, v)

    o = inner + cross
    mu = jnp.mean(o, axis=-1, keepdims=True)
    oc = o - mu
    var = jnp.mean(oc * oc, axis=-1, keepdims=True)
    on = oc * lax.rsqrt(var + LN_EPS) * gn_ref[...]
    gate = g_ref[...].astype(F32)
    o_ref[...] = (gate * _sigmoid(gate) * on).astype(o_ref.dtype)


def _retention_core(qk, vg, gn_g, b, s):
    heads, dk, dv, c = RET_HEADS, RET_QK_DIM, RET_V_DIM, RET_CHUNK
    n_ch = s // c
    log_g = jnp.log1p(-(2.0 ** (-5.0 - jnp.arange(heads, dtype=F32))))
    ix = jnp.arange(c, dtype=F32)
    rel = ix[:, None] - ix[None, :]
    decay = jnp.where(rel >= 0, jnp.exp(log_g[:, None, None] * jnp.maximum(rel, 0.0)), 0.0)
    xi = jnp.exp(log_g[:, None] * (ix + 1.0))[:, :, None]
    zeta = jnp.exp(log_g[:, None] * (c - 1.0 - ix))[:, :, None]
    g_chunk = jnp.exp(log_g * c)[:, None, None]
    tok = lambda bi, hi, ci: bi * n_ch + ci
    return pl.pallas_call(
        _ret_kernel,
        grid=(b, heads, n_ch),
        in_specs=[pl.BlockSpec((c, dk), lambda bi, hi, ci: (tok(bi, hi, ci), hi)),
                  pl.BlockSpec((c, dk), lambda bi, hi, ci: (tok(bi, hi, ci), heads + hi)),
                  pl.BlockSpec((c, dv), lambda bi, hi, ci: (tok(bi, hi, ci), hi)),
                  pl.BlockSpec((c, dv), lambda bi, hi, ci: (tok(bi, hi, ci), heads + hi)),
                  pl.BlockSpec((1, dv), lambda bi, hi, ci: (0, hi)),
                  pl.BlockSpec((1, c, c), lambda bi, hi, ci: (hi, 0, 0)),
                  pl.BlockSpec((1, c, 1), lambda bi, hi, ci: (hi, 0, 0)),
                  pl.BlockSpec((1, c, 1), lambda bi, hi, ci: (hi, 0, 0)),
                  pl.BlockSpec((1, 1, 1), lambda bi, hi, ci: (hi, 0, 0))],
        out_specs=pl.BlockSpec((c, dv), lambda bi, hi, ci: (tok(bi, hi, ci), hi)),
        out_shape=jax.ShapeDtypeStruct((b * s, heads * dv), BF16),
        scratch_shapes=[pltpu.VMEM((dk, dv), F32)],
        compiler_params=_cparams("parallel", "parallel", "arbitrary"),
        name="retention_core",
    )(qk, qk, vg, vg, gn_g.reshape(1, -1), decay, xi, zeta, g_chunk)


def _retention_layer(h2d, positions, g_in, w_in, gn_g, w_out, g_out):
    b, s = positions.shape
    n = b * s
    heads, dk, dv = RET_HEADS, RET_QK_DIM, RET_V_DIM
    w = w_in.astype(BF16)
    cos, sin = _rope_tables(positions, dk // 2)
    qk = _norm_proj(h2d, g_in, w[:, :2 * heads * dk], tn=512, out_dtype=BF16, epilogue="rope256",
                    tables=(cos.reshape(n, -1), sin.reshape(n, -1)), n_q_tiles=heads * dk // 512)
    vg = _norm_proj(h2d, g_in, w[:, 2 * heads * dk:], tn=1024, out_dtype=BF16)
    y = _retention_core(qk, vg, gn_g, b, s)
    return _proj_norm_residual(y, w_out.astype(BF16), g_out, h2d)


def kernel(x, positions, norm_g, ffn_w_gate, ffn_w_up, ffn_w_down, gm_w_in, gm_ln_g, gm_ln_b, gm_w_s, gm_b_s, gm_w_out, nsa_w_in, nsa_cmp_pe, nsa_cmp_w1, nsa_cmp_w2, nsa_w_out, ret_w_in, ret_gn_g, ret_w_out):
    b, s, d = x.shape
    h = x.reshape(b * s, d)
    for i in range(DEPTH):
        kind, j = i % N_MIXERS, i // N_MIXERS
        if kind == 0:
            h = _gmlp_layer(h, norm_g[i, 0], gm_w_in[j], gm_ln_g[j], gm_ln_b[j], gm_w_s[j],
                            gm_b_s[j], gm_w_out[j], norm_g[i, 1])
        elif kind == 1:
            h = _nsa_layer(h, positions, norm_g[i, 0], nsa_w_in[j], nsa_cmp_pe[j], nsa_cmp_w1[j],
                           nsa_cmp_w2[j], nsa_w_out[j], norm_g[i, 1])
        else:
            h = _retention_layer(h, positions, norm_g[i, 0], ret_w_in[j], ret_gn_g[j],
                                 ret_w_out[j], norm_g[i, 1])
        h = _ffn(h, norm_g[i, 2], ffn_w_gate[i].astype(BF16), ffn_w_up[i].astype(BF16),
                 ffn_w_down[i].astype(BF16), norm_g[i, 3])
    return h.reshape(b, s, d)
```

```python
import functools
import math

import jax
import jax.numpy as jnp
from jax import lax
from jax.experimental import pallas as pl
from jax.experimental.pallas import tpu as pltpu

F32 = jnp.float32
BF16 = jnp.bfloat16

D_MODEL = 1024
DEPTH = 4
N_MIXERS = 3
RMS_EPS = 1e-6
LN_EPS = 1e-5
ROPE_THETA = 10000.0
D_FF = 2816

GM_CHUNK = 128
GM_WIDTH = 2 * D_MODEL
GM_GROUPS = 8
GM_GROUP_W = GM_WIDTH // GM_GROUPS

NSA_HEADS = 16
NSA_KV_GROUPS = 4
NSA_HPG = NSA_HEADS // NSA_KV_GROUPS
NSA_HEAD_DIM = 64
NSA_KV = NSA_KV_GROUPS * NSA_HEAD_DIM
CMP_LEN = 32
CMP_STRIDE = 16
CMP_HIDDEN = 4 * NSA_HEAD_DIM
SEL_LEN = 64
N_SELECT = 8
WIN = 512

RET_HEADS = 4
RET_QK_DIM = 256
RET_V_DIM = 512
RET_CHUNK = 128

LANES = 128
BF16_SUBLANES = 16
LOG2_E = math.log2(math.e)
VMEM_LIMIT = 48 * 1024 * 1024
NEG_BIG = -1e30

ROW_TILE = 512
GM_ROW_TILE = 256
FFN_COL_CHUNK = 256
ATT_TQ = 256
ATT_TK = 128
SEL_CHUNK = 512


def _cparams(*sem):
    return pltpu.CompilerParams(dimension_semantics=sem, vmem_limit_bytes=VMEM_LIMIT)


def _resident(shape):
    nd = len(shape)
    return pl.BlockSpec(shape, lambda *_: (0,) * nd, pipeline_mode=pl.Buffered(1))


def _rms(x32, g):
    ms = jnp.mean(x32 * x32, axis=-1, keepdims=True)
    return x32 * lax.rsqrt(ms + RMS_EPS) * g


def _gelu_tanh(x):
    c = math.sqrt(2.0 / math.pi)
    return x * (0.5 * (1.0 + jnp.tanh(c * (x + 0.044715 * (x * x * x)))))


def _sigmoid(x):
    return 1.0 / (1.0 + jnp.exp(-x))


def _dot(a, b):
    return jnp.dot(a, b, preferred_element_type=F32)


def _proj_kernel(*refs, epilogue, n_tables):
    x_ref, g_ref, w_ref = refs[:3]
    tables = refs[3:3 + n_tables]
    o_ref, a_scr = refs[3 + n_tables:]
    j = pl.program_id(1)

    @pl.when(j == 0)
    def _():
        a_scr[...] = _rms(x_ref[...], g_ref[...]).astype(BF16)

    y = _dot(a_scr[...], w_ref[...])
    tn = y.shape[1]
    if epilogue == "plain":
        o_ref[...] = y.astype(o_ref.dtype)
    elif epilogue == "sigmoid":
        o_ref[...] = _sigmoid(y).astype(o_ref.dtype)
    elif epilogue == "rope64":
        cos, sin_lo, sin_hi = (t[...] for t in tables)
        for c in range(tn // LANES):
            yc = y[:, c * LANES:(c + 1) * LANES]
            out = (yc * cos + pltpu.roll(yc, LANES - 32, 1) * sin_lo
                   + pltpu.roll(yc, 32, 1) * sin_hi)
            o_ref[:, c * LANES:(c + 1) * LANES] = out.astype(o_ref.dtype)
    else:
        raise ValueError(epilogue)


def _norm_proj(x2d, g, w, *, tn, out_dtype, epilogue="plain", tables=()):
    n, d = x2d.shape
    n_out = w.shape[1]
    tm = ROW_TILE
    assert n % tm == 0 and n_out % tn == 0
    table_specs = [pl.BlockSpec((tm, t.shape[1]), lambda i, j: (i, 0)) for t in tables]
    return pl.pallas_call(
        functools.partial(_proj_kernel, epilogue=epilogue, n_tables=len(tables)),
        grid=(n // tm, n_out // tn),
        in_specs=[pl.BlockSpec((tm, d), lambda i, j: (i, 0)),
                  pl.BlockSpec((1, d), lambda i, j: (0, 0)),
                  pl.BlockSpec((d, tn), lambda i, j: (0, j))] + table_specs,
        out_specs=pl.BlockSpec((tm, tn), lambda i, j: (i, j)),
        out_shape=jax.ShapeDtypeStruct((n, n_out), out_dtype),
        scratch_shapes=[pltpu.VMEM((tm, d), BF16)],
        compiler_params=_cparams("parallel", "arbitrary"),
        name="norm_proj_" + epilogue,
    )(x2d, g.reshape(1, d), w, *tables)


def _out_kernel(m_ref, w_ref, g_ref, h_ref, o_ref):
    y = _dot(m_ref[...], w_ref[...])
    o_ref[...] = h_ref[...] + _rms(y, g_ref[...])


def _proj_norm_residual(m2d, w, g, h2d):
    n, k = m2d.shape
    d = w.shape[1]
    tm = ROW_TILE
    return pl.pallas_call(
        _out_kernel,
        grid=(n // tm,),
        in_specs=[pl.BlockSpec((tm, k), lambda i: (i, 0)),
                  _resident((k, d)),
                  _resident((1, d)),
                  pl.BlockSpec((tm, d), lambda i: (i, 0))],
        out_specs=pl.BlockSpec((tm, d), lambda i: (i, 0)),
        out_shape=jax.ShapeDtypeStruct((n, d), F32),
        compiler_params=_cparams("parallel"),
        name="proj_norm_residual",
    )(m2d, w, g.reshape(1, d), h2d)


def _ffn_kernel(h_ref, g_in_ref, wg_ref, wu_ref, wd_ref, g_out_ref, o_ref, acc_ref):
    x = h_ref[...]
    a = _rms(x, g_in_ref[...]).astype(BF16)
    for c in range(D_FF // FFN_COL_CHUNK):
        cols = slice(c * FFN_COL_CHUNK, (c + 1) * FFN_COL_CHUNK)
        gate = _dot(a, wg_ref[:, cols])
        up = _dot(a, wu_ref[:, cols])
        act = (gate * _sigmoid(gate) * up).astype(BF16)
        part = _dot(act, wd_ref[cols, :])
        if c == 0:
            acc_ref[...] = part
        else:
            acc_ref[...] += part
    o_ref[...] = x + _rms(acc_ref[...], g_out_ref[...])


def _ffn(h2d, g_in, wg, wu, wd, g_out):
    n, d = h2d.shape
    tm = ROW_TILE
    return pl.pallas_call(
        _ffn_kernel,
        grid=(n // tm,),
        in_specs=[pl.BlockSpec((tm, d), lambda i: (i, 0)),
                  _resident((1, d)),
                  _resident((d, D_FF)),
                  _resident((d, D_FF)),
                  _resident((D_FF, d)),
                  _resident((1, d))],
        out_specs=pl.BlockSpec((tm, d), lambda i: (i, 0)),
        out_shape=jax.ShapeDtypeStruct((n, d), F32),
        scratch_shapes=[pltpu.VMEM((tm, d), F32)],
        compiler_params=_cparams("parallel"),
        name="swiglu_ffn",
    )(h2d, g_in.reshape(1, d), wg, wu, wd, g_out.reshape(1, d))


def _gmlp_kernel(h_ref, g_in_ref, win_ref, lng_ref, lnb_ref, ws_ref, bst_ref, wout_ref,
                 g_out_ref, o_ref, vn_scr, y_scr):
    x = h_ref[...]
    tm = x.shape[0]
    a = _rms(x, g_in_ref[...]).astype(BF16)

    v = _gelu_tanh(_dot(a, win_ref[:, GM_WIDTH:]))
    mu = jnp.mean(v, axis=-1, keepdims=True)
    vc = v - mu
    var = jnp.mean(vc * vc, axis=-1, keepdims=True)
    vn_scr[...] = (vc * lax.rsqrt(var + LN_EPS) * lng_ref[...] + lnb_ref[...]).astype(BF16)

    t_out = lax.broadcasted_iota(jnp.int32, (GM_CHUNK, GM_CHUNK), 0)
    t_in = lax.broadcasted_iota(jnp.int32, (GM_CHUNK, GM_CHUNK), 1)
    causal = t_in <= t_out
    bst = bst_ref[...]
    for grp in range(GM_GROUPS):
        cols = slice(grp * GM_GROUP_W, (grp + 1) * GM_GROUP_W)
        w_mix = jnp.where(causal, ws_ref[grp], 0.0).astype(BF16)
        bias = bst[:, grp:grp + 1]
        for c in range(tm // GM_CHUNK):
            rows = slice(c * GM_CHUNK, (c + 1) * GM_CHUNK)
            mixed = _dot(w_mix, vn_scr[rows, cols]) + bias
            u = _gelu_tanh(_dot(a[rows], win_ref[:, cols]))
            y_scr[rows, cols] = (u * mixed).astype(BF16)

    m = _dot(y_scr[...], wout_ref[...])
    o_ref[...] = x + _rms(m, g_out_ref[...])


def _gmlp_layer(h2d, g_in, w_in, ln_g, ln_b, w_s, b_s, w_out, g_out):
    n, d = h2d.shape
    tm = GM_ROW_TILE
    return pl.pallas_call(
        _gmlp_kernel,
        grid=(n // tm,),
        in_specs=[pl.BlockSpec((tm, d), lambda i: (i, 0)),
                  _resident((1, d)),
                  _resident((d, 2 * GM_WIDTH)),
                  _resident((1, GM_WIDTH)),
                  _resident((1, GM_WIDTH)),
                  _resident((GM_GROUPS, GM_CHUNK, GM_CHUNK)),
                  _resident((GM_CHUNK, GM_GROUPS)),
                  _resident((GM_WIDTH, d)),
                  _resident((1, d))],
        out_specs=pl.BlockSpec((tm, d), lambda i: (i, 0)),
        out_shape=jax.ShapeDtypeStruct((n, d), F32),
        scratch_shapes=[pltpu.VMEM((tm, GM_WIDTH), BF16), pltpu.VMEM((tm, GM_WIDTH), BF16)],
        compiler_params=_cparams("parallel"),
        name="gmlp_layer",
    )(h2d, g_in.reshape(1, d), w_in.astype(BF16), ln_g.reshape(1, -1), ln_b.reshape(1, -1),
      w_s, b_s.T, w_out.astype(BF16), g_out.reshape(1, d))


def _rope_tables(positions, half):
    inv = ROPE_THETA ** (-jnp.arange(half, dtype=F32) / half)
    ang = positions.astype(F32)[..., None] * inv
    return jnp.cos(ang), jnp.sin(ang)


def _cmp_kernel(f_ref, pe_ref, w1_ref, w2_ref, cos_ref, sin_ref, o_ref):
    half = CMP_LEN * NSA_HEAD_DIM // 2
    x = f_ref[0, 0, 0].astype(F32)
    pe = pe_ref[0]
    first = _dot((x + pe[:, :half]).astype(BF16), w1_ref[0, :half, :])
    second = _dot((x + pe[:, half:]).astype(BF16), w1_ref[0, half:, :])
    hid = _gelu_tanh(first + pltpu.roll(second, second.shape[0] - 1, 0))
    y = _dot(hid.astype(BF16), w2_ref[0])
    out = y[:, :NSA_HEAD_DIM] * cos_ref[0, 0] + y[:, NSA_HEAD_DIM:] * sin_ref[0, 0]
    o_ref[0, 0, 0] = out.astype(o_ref.dtype)


def _compress(f, pe_flat, w1, w2cat, cos_t, sin_t):
    _, b, g, n_grp, width = f.shape
    dk = NSA_HEAD_DIM
    return pl.pallas_call(
        _cmp_kernel,
        grid=(2, b, g),
        in_specs=[pl.BlockSpec((1, 1, 1, n_grp, width), lambda s, i, j: (s, i, j, 0, 0)),
                  pl.BlockSpec((1, 1, pe_flat.shape[-1]), lambda s, i, j: (s, 0, 0)),
                  pl.BlockSpec((1,) + w1.shape[1:], lambda s, i, j: (s, 0, 0)),
                  pl.BlockSpec((1,) + w2cat.shape[1:], lambda s, i, j: (s, 0, 0)),
                  pl.BlockSpec((1, 1, n_grp, dk), lambda s, i, j: (s, i, 0, 0)),
                  pl.BlockSpec((1, 1, n_grp, dk), lambda s, i, j: (s, i, 0, 0))],
        out_specs=pl.BlockSpec((1, 1, 1, n_grp, dk), lambda s, i, j: (s, i, j, 0, 0)),
        out_shape=jax.ShapeDtypeStruct((2, b, g, n_grp, dk), BF16),
        compiler_params=_cparams("parallel", "parallel", "parallel"),
        name="nsa_compress",
    )(f, pe_flat, w1, w2cat, cos_t, sin_t)


def _nsa_attn_kernel(q_ref, ks_ref, vst_ref, kw_ref, vwt_ref, kc_ref, vct_ref, gate_ref,
                     o_ref, os_scr, score_scr):
    tq, tk, hpg, dk = ATT_TQ, ATT_TK, NSA_HPG, NSA_HEAD_DIM
    ratio = SEL_LEN // CMP_STRIDE
    ratio_shift = ratio.bit_length() - 1
    sel_shift = SEL_LEN.bit_length() - 1
    n_sel = LANES // ratio
    i = pl.program_id(2)
    t0 = i * tq

    def tile_heads(x):
        return jnp.concatenate([x] * hpg, axis=1)

    q_t = q_ref[...].astype(F32).T
    qt = jnp.concatenate([q_t[h * dk:(h + 1) * dk] for h in range(hpg)], axis=1)
    qt = (qt * (dk ** -0.5 * LOG2_E)).astype(BF16)

    def softmax_pv(parts):
        maxes, accs = [], []
        for s, vt in parts:
            m_part = jnp.max(s, axis=0, keepdims=True)
            p = jnp.exp2(s - m_part).astype(BF16)
            vt_ones = jnp.concatenate([vt, jnp.ones((BF16_SUBLANES, vt.shape[1]), BF16)], axis=0)
            maxes.append(m_part)
            accs.append(_dot(vt_ones, p))
        m = functools.reduce(jnp.maximum, maxes)
        acc = sum(jnp.exp2(m_part - m) * a for m_part, a in zip(maxes, accs))
        return acc[:dk] * (1.0 / jnp.maximum(acc[dk:dk + 1], 1e-30))

    n_win = WIN + tq
    k_lo = pl.multiple_of(jnp.maximum(t0 - WIN, 0), tk)
    n_half = n_win // 2
    row_minus_col = (lax.broadcasted_iota(jnp.int32, (n_half, tq), 0)
                     - lax.broadcasted_iota(jnp.int32, (n_half, tq), 1))
    parts_w = []
    for c in range(2):
        k0 = pl.multiple_of(k_lo + c * n_half, tk)
        back = lax.bitcast_convert_type((t0 - k0) - row_minus_col, jnp.uint32)
        bias_w = jnp.where(back < jnp.uint32(WIN), 0.0, NEG_BIG)
        s_w = _dot(kw_ref[0, 0, pl.ds(k0, n_half), :], qt) + tile_heads(bias_w)
        parts_w.append((s_w, vwt_ref[0, 0, :, pl.ds(k0, n_half)]))
    o_w = softmax_pv(parts_w)

    n_cmp = kc_ref.shape[2]
    row_c = lax.broadcasted_iota(jnp.int32, (n_cmp, tq), 0)
    tok_c = t0 + lax.broadcasted_iota(jnp.int32, (n_cmp, tq), 1)
    n_idx = ((row_c & (n_sel - 1)) << ratio_shift) + (row_c >> (n_sel.bit_length() - 1))
    bias_c = jnp.where(n_idx * CMP_STRIDE + (CMP_LEN - 1) <= tok_c, 0.0, NEG_BIG)
    s_c = _dot(kc_ref[0, 0], qt) + tile_heads(bias_c)
    e_c = jnp.exp2(s_c - jnp.max(s_c, axis=0, keepdims=True))
    tok_row = t0 + (lax.broadcasted_iota(jnp.int32, (1, hpg * tq), 1) & (tq - 1))
    inv_l = jnp.where(tok_row >= CMP_LEN - 1,
                      1.0 / jnp.maximum(jnp.sum(e_c, axis=0, keepdims=True), 1e-30), 0.0)
    p_c = e_c * inv_l
    o_c = _dot(vct_ref[0, 0], p_c.astype(BF16))

    p_sum = p_c[:, 0:tq]
    for h in range(1, hpg):
        p_sum = p_sum + p_c[:, h * tq:(h + 1) * tq]
    imp = p_sum[0:n_sel]
    for r in range(1, ratio):
        imp = imp + p_sum[r * n_sel:(r + 1) * n_sel]

    blk = lax.broadcasted_iota(jnp.int32, (n_sel, tq), 0)
    cur = (t0 + lax.broadcasted_iota(jnp.int32, (n_sel, tq), 1)) >> sel_shift
    forced = (blk == 0) | (blk == cur) | (blk == cur - 1)
    score = jnp.where(blk <= cur, jnp.where(forced, jnp.inf, imp), -jnp.inf)
    score_scr[...] = score
    grp_rows = 8
    row_in_grp = lax.broadcasted_iota(jnp.int32, (grp_rows, tq), 0)
    groups = [score[a:a + grp_rows] for a in range(0, n_sel, grp_rows)]
    ranks = [jnp.zeros((grp_rows, tq), F32) for _ in groups]
    for rival in range(n_sel):
        other = score_scr[rival:rival + 1, :]
        for a, mine in enumerate(groups):
            lo = a * grp_rows
            if rival < lo:
                beats = other >= mine
            elif rival >= lo + grp_rows:
                beats = other > mine
            else:
                beats = (other > mine) | ((other == mine) & (row_in_grp + lo > rival))
            ranks[a] = ranks[a] + jnp.where(beats, 1.0, 0.0)
    rank = jnp.concatenate(ranks, axis=0)
    sel_bias = jnp.where(rank < float(N_SELECT), 0.0, NEG_BIG).astype(BF16)

    pad_rows = ks_ref.shape[3] - dk - n_sel
    qt_aug = jnp.concatenate([qt, tile_heads(sel_bias), jnp.zeros((pad_rows, hpg * tq), BF16)], axis=0)
    tok_s = t0 + lax.broadcasted_iota(jnp.int32, (SEL_CHUNK, tq), 1)
    key_s = lax.broadcasted_iota(jnp.int32, (SEL_CHUNK, tq), 0)
    n_variants = ks_ref.shape[2] // SEL_CHUNK
    for v in range(n_variants):
        @pl.when((t0 >> (SEL_CHUNK.bit_length() - 1)) == v)
        def _():
            parts = []
            for c in range(v + 1):
                rows = slice(c * SEL_CHUNK, (c + 1) * SEL_CHUNK)
                s = _dot(ks_ref[0, 0, rows, :], qt_aug)
                if c == v:
                    s = s + tile_heads(jnp.where(c * SEL_CHUNK + key_s <= tok_s, 0.0, NEG_BIG))
                parts.append((s, vst_ref[0, 0, :, rows]))
            os_scr[...] = softmax_pv(parts)
    o_s = os_scr[...]

    gates = gate_ref[0, 0, 0]
    out_t = gates[0:1] * o_c + gates[1:2] * o_s + gates[2:3] * o_w
    stacked = jnp.concatenate([out_t[:, h * tq:(h + 1) * tq] for h in range(hpg)], axis=0)
    o_ref[...] = stacked.T.astype(o_ref.dtype)


def _nsa_attention(roped, ks_aug, vst, kw, vwt, kc, vct, gates):
    b, g, s, dk = kw.shape
    hpg = NSA_HPG
    n_cmp = kc.shape[2]
    tq = ATT_TQ
    nt = s // tq
    assert s % SEL_CHUNK == 0 and SEL_CHUNK % tq == 0 and s >= WIN + tq
    k_spec = pl.BlockSpec((1, 1, s, dk), lambda bi, gi, i: (bi, gi, 0, 0))
    vt_spec = pl.BlockSpec((1, 1, dk, s), lambda bi, gi, i: (bi, gi, 0, 0))
    return pl.pallas_call(
        _nsa_attn_kernel,
        grid=(b, g, nt),
        in_specs=[pl.BlockSpec((tq, hpg * dk), lambda bi, gi, i: (bi * nt + i, gi)),
                  pl.BlockSpec((1, 1, s, ks_aug.shape[3]), lambda bi, gi, i: (bi, gi, 0, 0)),
                  vt_spec, k_spec, vt_spec,
                  pl.BlockSpec((1, 1, n_cmp, dk), lambda bi, gi, i: (bi, gi, 0, 0)),
                  pl.BlockSpec((1, 1, dk, n_cmp), lambda bi, gi, i: (bi, gi, 0, 0)),
                  pl.BlockSpec((1, 1, 1, 3, hpg * tq), lambda bi, gi, i: (bi, gi, i, 0, 0))],
        out_specs=pl.BlockSpec((tq, hpg * dk), lambda bi, gi, i: (bi * nt + i, gi)),
        out_shape=jax.ShapeDtypeStruct((b * s, g * hpg * dk), BF16),
        scratch_shapes=[pltpu.VMEM((dk, hpg * tq), F32), pltpu.VMEM((n_cmp * CMP_STRIDE // SEL_LEN, tq), F32)],
        compiler_params=_cparams("parallel", "parallel", "parallel"),
        name="nsa_attention",
    )(roped, ks_aug, vst, kw, vwt, kc, vct, gates)


def _nsa_layer(h2d, cos, sin, g_in, w_in, cmp_pe, cmp_w1, cmp_w2, w_out, g_out):
    b, s, _ = cos.shape
    n, d = h2d.shape
    heads, grp, hpg, dk = NSA_HEADS, NSA_KV_GROUPS, NSA_HPG, NSA_HEAD_DIM
    half = dk // 2
    qw = heads * dk
    col = lambda k: w_in[:, qw + k * NSA_KV: qw + (k + 1) * NSA_KV]
    w_rope = jnp.concatenate([w_in[:, :qw], col(2), col(4)], axis=1).astype(BF16)
    w_plain = jnp.concatenate([col(0), col(1), col(3), col(5)], axis=1).astype(BF16)
    w_gate = jnp.pad(w_in[:, qw + 6 * NSA_KV:], ((0, 0), (0, LANES - 3 * heads))).astype(BF16)

    zeros = jnp.zeros_like(sin)
    lane_tile = lambda t: jnp.tile(t, (1, 1, LANES // t.shape[-1])).reshape(n, LANES)
    cos_t = lane_tile(cos)
    sin_lo = lane_tile(jnp.concatenate([-sin, zeros], axis=-1))
    sin_hi = lane_tile(jnp.concatenate([zeros, sin], axis=-1))

    roped = _norm_proj(h2d, g_in, w_rope, tn=512, out_dtype=BF16, epilogue="rope64",
                       tables=(cos_t, sin_lo, sin_hi))
    plain = _norm_proj(h2d, g_in, w_plain, tn=512, out_dtype=BF16)
    gates = _norm_proj(h2d, g_in, w_gate, tn=LANES, out_dtype=F32, epilogue="sigmoid")

    to_k = lambda t: t.reshape(b, s, grp, dk).transpose(0, 2, 1, 3)
    to_vt = lambda t: t.reshape(b, s, grp, dk).transpose(0, 2, 3, 1)
    kw = to_k(roped[:, qw + NSA_KV:])
    onehot = (jnp.arange(s)[:, None] // SEL_LEN == jnp.arange(LANES - dk)[None, :]).astype(BF16)
    ks_aug = jnp.concatenate([to_k(roped[:, qw:qw + NSA_KV]),
                              jnp.broadcast_to(onehot, (b, grp, s, LANES - dk))], axis=-1)
    kc, vc = (to_k(plain[:, k * NSA_KV:(k + 1) * NSA_KV]) for k in range(2))
    vst, vwt = (to_vt(plain[:, k * NSA_KV:(k + 1) * NSA_KV]) for k in range(2, 4))
    nt = s // ATT_TQ
    gates = (gates[:, :3 * heads].reshape(b, nt, ATT_TQ, grp, hpg, 3)
             .transpose(0, 3, 1, 5, 4, 2).reshape(b, grp, nt, 3, hpg * ATT_TQ))

    n_grp = s // CMP_STRIDE
    f = jnp.stack([kc, vc]).reshape(2, b, grp, n_grp, CMP_STRIDE * dk)
    pe_flat = cmp_pe.reshape(2, 1, CMP_LEN * dk)
    rot = lambda w: jnp.concatenate([-w[..., half:], w[..., :half]], axis=-1)
    w2cat = jnp.concatenate([cmp_w2, rot(cmp_w2)], axis=-1).astype(BF16)
    n_cmp = (s - CMP_LEN) // CMP_STRIDE + 1
    pad = ((0, 0), (0, n_grp - n_cmp), (0, 0))
    cos_e = jnp.pad(cos[:, CMP_LEN - 1::CMP_STRIDE], pad, constant_values=1.0)
    sin_e = jnp.pad(sin[:, CMP_LEN - 1::CMP_STRIDE], pad)
    cos_c = jnp.stack([jnp.concatenate([cos_e, cos_e], -1), jnp.ones((b, n_grp, dk), F32)])
    sin_c = jnp.stack([jnp.concatenate([sin_e, sin_e], -1), jnp.zeros((b, n_grp, dk), F32)])
    cmp = _compress(f, pe_flat, cmp_w1.astype(BF16), w2cat, cos_c, sin_c)
    ratio = SEL_LEN // CMP_STRIDE
    cmp = (cmp.reshape(2, b, grp, n_grp // ratio, ratio, dk).transpose(0, 1, 2, 4, 3, 5)
           .reshape(2, b, grp, n_grp, dk))

    o2d = _nsa_attention(roped, ks_aug, vst, kw, vwt, cmp[0], cmp[1].transpose(0, 1, 3, 2), gates)
    return _proj_norm_residual(o2d, w_out.astype(BF16), g_out, h2d)


def _ret_proj_kernel(x_ref, g_ref, w_ref, cos_ref, sin_ref, qk_ref, vg_ref):
    heads, dk = RET_HEADS, RET_QK_DIM
    half = dk // 2
    a = _rms(x_ref[...], g_ref[...]).astype(BF16)
    cos, sin = cos_ref[...], sin_ref[...]
    for c in range(2 * heads):
        y = _dot(a, w_ref[:, c * dk:(c + 1) * dk])
        y1, y2 = y[:, :half], y[:, half:]
        scale = 1.0 if c < heads else dk ** -0.5
        qk_ref[:, c * dk:c * dk + half] = ((y1 * cos - y2 * sin) * scale).astype(qk_ref.dtype)
        qk_ref[:, c * dk + half:(c + 1) * dk] = ((y2 * cos + y1 * sin) * scale).astype(qk_ref.dtype)
    n_qk = 2 * heads * dk
    for c in range(vg_ref.shape[1] // RET_V_DIM):
        cols = slice(c * RET_V_DIM, (c + 1) * RET_V_DIM)
        vg_ref[:, cols] = _dot(a, w_ref[:, n_qk + c * RET_V_DIM:n_qk + (c + 1) * RET_V_DIM]).astype(vg_ref.dtype)


def _retention_proj(h2d, g_in, w, cos, sin):
    n, d = h2d.shape
    n_qk = 2 * RET_HEADS * RET_QK_DIM
    n_vg = w.shape[1] - n_qk
    tm = ROW_TILE
    return pl.pallas_call(
        _ret_proj_kernel,
        grid=(n // tm,),
        in_specs=[pl.BlockSpec((tm, d), lambda i: (i, 0)),
                  _resident((1, d)),
                  _resident(w.shape),
                  pl.BlockSpec((tm, cos.shape[1]), lambda i: (i, 0)),
                  pl.BlockSpec((tm, sin.shape[1]), lambda i: (i, 0))],
        out_specs=[pl.BlockSpec((tm, n_qk), lambda i: (i, 0)),
                   pl.BlockSpec((tm, n_vg), lambda i: (i, 0))],
        out_shape=[jax.ShapeDtypeStruct((n, n_qk), BF16), jax.ShapeDtypeStruct((n, n_vg), BF16)],
        compiler_params=_cparams("parallel"),
        name="retention_proj",
    )(h2d, g_in.reshape(1, d), w, cos, sin)


def _ret_kernel(qk_ref, vg_ref, gn_ref, decay_ref, xi_ref, zeta_ref, gc_ref, o_ref, state_scr):
    heads, dk, dv = RET_HEADS, RET_QK_DIM, RET_V_DIM

    @pl.when(pl.program_id(1) == 0)
    def _():
        state_scr[...] = jnp.zeros(state_scr.shape, F32)

    for h in range(heads):
        q = qk_ref[:, h * dk:(h + 1) * dk]
        k = qk_ref[:, (heads + h) * dk:(heads + h + 1) * dk]
        v = vg_ref[:, h * dv:(h + 1) * dv]
        s = lax.dot_general(q, k, (((1,), (1,)), ((), ())), preferred_element_type=F32)
        inner = _dot((s * decay_ref[h]).astype(BF16), v)
        state = state_scr[h]
        cross = _dot(q, state.astype(BF16)) * xi_ref[h]
        kz = (k.astype(F32) * zeta_ref[h]).T.astype(BF16)
        state_scr[h] = gc_ref[h] * state + _dot(kz, v)

        o = inner + cross
        mu = jnp.mean(o, axis=-1, keepdims=True)
        oc = o - mu
        var = jnp.mean(oc * oc, axis=-1, keepdims=True)
        on = oc * lax.rsqrt(var + LN_EPS) * gn_ref[:, h * dv:(h + 1) * dv]
        gate = vg_ref[:, (heads + h) * dv:(heads + h + 1) * dv].astype(F32)
        o_ref[:, h * dv:(h + 1) * dv] = (gate * _sigmoid(gate) * on).astype(o_ref.dtype)


def _retention_core(qk, vg, gn_g, b, s):
    heads, dk, dv, c = RET_HEADS, RET_QK_DIM, RET_V_DIM, RET_CHUNK
    n_ch = s // c
    log_g = jnp.log1p(-(2.0 ** (-5.0 - jnp.arange(heads, dtype=F32))))
    ix = jnp.arange(c, dtype=F32)
    rel = ix[:, None] - ix[None, :]
    decay = jnp.where(rel >= 0, jnp.exp(log_g[:, None, None] * jnp.maximum(rel, 0.0)), 0.0)
    xi = jnp.exp(log_g[:, None] * (ix + 1.0))[:, :, None]
    zeta = jnp.exp(log_g[:, None] * (c - 1.0 - ix))[:, :, None]
    g_chunk = jnp.exp(log_g * c)[:, None, None]
    return pl.pallas_call(
        _ret_kernel,
        grid=(b, n_ch),
        in_specs=[pl.BlockSpec((c, qk.shape[1]), lambda bi, ci: (bi * n_ch + ci, 0)),
                  pl.BlockSpec((c, vg.shape[1]), lambda bi, ci: (bi * n_ch + ci, 0)),
                  _resident((1, heads * dv)),
                  _resident(decay.shape), _resident(xi.shape), _resident(zeta.shape),
                  _resident(g_chunk.shape)],
        out_specs=pl.BlockSpec((c, heads * dv), lambda bi, ci: (bi * n_ch + ci, 0)),
        out_shape=jax.ShapeDtypeStruct((b * s, heads * dv), BF16),
        scratch_shapes=[pltpu.VMEM((heads, dk, dv), F32)],
        compiler_params=_cparams("parallel", "arbitrary"),
        name="retention_core",
    )(qk, vg, gn_g.reshape(1, -1), decay, xi, zeta, g_chunk)


def _retention_layer(h2d, cos, sin, g_in, w_in, gn_g, w_out, g_out, b, s):
    qk, vg = _retention_proj(h2d, g_in, w_in.astype(BF16), cos, sin)
    y = _retention_core(qk, vg, gn_g, b, s)
    return _proj_norm_residual(y, w_out.astype(BF16), g_out, h2d)


def kernel(x, positions, norm_g, ffn_w_gate, ffn_w_up, ffn_w_down, gm_w_in, gm_ln_g, gm_ln_b, gm_w_s, gm_b_s, gm_w_out, nsa_w_in, nsa_cmp_pe, nsa_cmp_w1, nsa_cmp_w2, nsa_w_out, ret_w_in, ret_gn_g, ret_w_out):
    b, s, d = x.shape
    h = x.reshape(b * s, d)
    ret_half = RET_QK_DIM // 2
    nsa_step = ret_half // (NSA_HEAD_DIM // 2)
    cos, sin = _rope_tables(positions, ret_half)
    for i in range(DEPTH):
        kind, j = i % N_MIXERS, i // N_MIXERS
        if kind == 0:
            h = _gmlp_layer(h, norm_g[i, 0], gm_w_in[j], gm_ln_g[j], gm_ln_b[j], gm_w_s[j],
                            gm_b_s[j], gm_w_out[j], norm_g[i, 1])
        elif kind == 1:
            h = _nsa_layer(h, cos[..., ::nsa_step], sin[..., ::nsa_step], norm_g[i, 0], nsa_w_in[j],
                           nsa_cmp_pe[j], nsa_cmp_w1[j], nsa_cmp_w2[j], nsa_w_out[j], norm_g[i, 1])
        else:
            h = _retention_layer(h, cos.reshape(b * s, ret_half), sin.reshape(b * s, ret_half),
                                 norm_g[i, 0], ret_w_in[j], ret_gn_g[j], ret_w_out[j], norm_g[i, 1], b, s)
        h = _ffn(h, norm_g[i, 2], ffn_w_gate[i].astype(BF16), ffn_w_up[i].astype(BF16),
                 ffn_w_down[i].astype(BF16), norm_g[i, 3])
    return h.reshape(b, s, d)
```

```python
import functools
import math

import jax
import jax.numpy as jnp
from jax import lax
from jax.experimental import pallas as pl
from jax.experimental.pallas import tpu as pltpu

F32 = jnp.float32
BF16 = jnp.bfloat16

D_MODEL = 1024
DEPTH = 4
N_MIXERS = 3
RMS_EPS = 1e-6
LN_EPS = 1e-5
ROPE_THETA = 10000.0
D_FF = 2816

GM_CHUNK = 128
GM_WIDTH = 2 * D_MODEL
GM_GROUPS = 8
GM_GROUP_W = GM_WIDTH // GM_GROUPS

NSA_HEADS = 16
NSA_KV_GROUPS = 4
NSA_HPG = NSA_HEADS // NSA_KV_GROUPS
NSA_HEAD_DIM = 64
NSA_KV = NSA_KV_GROUPS * NSA_HEAD_DIM
CMP_LEN = 32
CMP_STRIDE = 16
CMP_HIDDEN = 4 * NSA_HEAD_DIM
SEL_LEN = 64
N_SELECT = 8
WIN = 512

RET_HEADS = 4
RET_QK_DIM = 256
RET_V_DIM = 512
RET_CHUNK = 128

LANES = 128
BF16_SUBLANES = 16
LOG2_E = math.log2(math.e)
VMEM_LIMIT = 48 * 1024 * 1024
NEG_BIG = -1e30

ROW_TILE = 512
GM_ROW_TILE = 256
FFN_COL_CHUNK = 256
ATT_TQ = 256
ATT_TK = 128
SEL_CHUNK = 512


def _cparams(*sem):
    return pltpu.CompilerParams(dimension_semantics=sem, vmem_limit_bytes=VMEM_LIMIT)


def _resident(shape):
    nd = len(shape)
    return pl.BlockSpec(shape, lambda *_: (0,) * nd, pipeline_mode=pl.Buffered(1))


def _rms(x32, g):
    ms = jnp.mean(x32 * x32, axis=-1, keepdims=True)
    return x32 * lax.rsqrt(ms + RMS_EPS) * g


def _gelu_tanh(x):
    c = math.sqrt(2.0 / math.pi)
    return x * (0.5 * (1.0 + jnp.tanh(c * (x + 0.044715 * (x * x * x)))))


def _sigmoid(x):
    return 1.0 / (1.0 + jnp.exp(-x))


def _dot(a, b):
    return jnp.dot(a, b, preferred_element_type=F32)


def _out_kernel(m_ref, w_ref, g_ref, h_ref, o_ref):
    y = _dot(m_ref[...], w_ref[...])
    o_ref[...] = h_ref[...] + _rms(y, g_ref[...])


def _proj_norm_residual(m2d, w, g, h2d):
    n, k = m2d.shape
    d = w.shape[1]
    tm = ROW_TILE
    return pl.pallas_call(
        _out_kernel,
        grid=(n // tm,),
        in_specs=[pl.BlockSpec((tm, k), lambda i: (i, 0)),
                  _resident((k, d)),
                  _resident((1, d)),
                  pl.BlockSpec((tm, d), lambda i: (i, 0))],
        out_specs=pl.BlockSpec((tm, d), lambda i: (i, 0)),
        out_shape=jax.ShapeDtypeStruct((n, d), F32),
        compiler_params=_cparams("parallel"),
        name="proj_norm_residual",
    )(m2d, w, g.reshape(1, d), h2d)


def _ffn_kernel(h_ref, g_in_ref, wg_ref, wu_ref, wd_ref, g_out_ref, o_ref, acc_ref):
    x = h_ref[...]
    a = _rms(x, g_in_ref[...]).astype(BF16)
    for c in range(D_FF // FFN_COL_CHUNK):
        cols = slice(c * FFN_COL_CHUNK, (c + 1) * FFN_COL_CHUNK)
        gate = _dot(a, wg_ref[:, cols])
        up = _dot(a, wu_ref[:, cols])
        act = (gate * _sigmoid(gate) * up).astype(BF16)
        part = _dot(act, wd_ref[cols, :])
        if c == 0:
            acc_ref[...] = part
        else:
            acc_ref[...] += part
    o_ref[...] = x + _rms(acc_ref[...], g_out_ref[...])


def _ffn(h2d, g_in, wg, wu, wd, g_out):
    n, d = h2d.shape
    tm = ROW_TILE
    return pl.pallas_call(
        _ffn_kernel,
        grid=(n // tm,),
        in_specs=[pl.BlockSpec((tm, d), lambda i: (i, 0)),
                  _resident((1, d)),
                  _resident((d, D_FF)),
                  _resident((d, D_FF)),
                  _resident((D_FF, d)),
                  _resident((1, d))],
        out_specs=pl.BlockSpec((tm, d), lambda i: (i, 0)),
        out_shape=jax.ShapeDtypeStruct((n, d), F32),
        scratch_shapes=[pltpu.VMEM((tm, d), F32)],
        compiler_params=_cparams("parallel"),
        name="swiglu_ffn",
    )(h2d, g_in.reshape(1, d), wg, wu, wd, g_out.reshape(1, d))


def _gmlp_kernel(h_ref, g_in_ref, win_ref, lng_ref, lnb_ref, ws_ref, bst_ref, wout_ref,
                 g_out_ref, o_ref, vn_scr, y_scr):
    x = h_ref[...]
    tm = x.shape[0]
    a = _rms(x, g_in_ref[...]).astype(BF16)

    v = _gelu_tanh(_dot(a, win_ref[:, GM_WIDTH:]))
    mu = jnp.mean(v, axis=-1, keepdims=True)
    vc = v - mu
    var = jnp.mean(vc * vc, axis=-1, keepdims=True)
    vn_scr[...] = (vc * lax.rsqrt(var + LN_EPS) * lng_ref[...] + lnb_ref[...]).astype(BF16)

    t_out = lax.broadcasted_iota(jnp.int32, (GM_CHUNK, GM_CHUNK), 0)
    t_in = lax.broadcasted_iota(jnp.int32, (GM_CHUNK, GM_CHUNK), 1)
    causal = t_in <= t_out
    bst = bst_ref[...]
    for grp in range(GM_GROUPS):
        cols = slice(grp * GM_GROUP_W, (grp + 1) * GM_GROUP_W)
        w_mix = jnp.where(causal, ws_ref[grp], 0.0).astype(BF16)
        bias = bst[:, grp:grp + 1]
        for c in range(tm // GM_CHUNK):
            rows = slice(c * GM_CHUNK, (c + 1) * GM_CHUNK)
            mixed = _dot(w_mix, vn_scr[rows, cols]) + bias
            u = _gelu_tanh(_dot(a[rows], win_ref[:, cols]))
            y_scr[rows, cols] = (u * mixed).astype(BF16)

    m = _dot(y_scr[...], wout_ref[...])
    o_ref[...] = x + _rms(m, g_out_ref[...])


def _gmlp_layer(h2d, g_in, w_in, ln_g, ln_b, w_s, b_s, w_out, g_out):
    n, d = h2d.shape
    tm = GM_ROW_TILE
    return pl.pallas_call(
        _gmlp_kernel,
        grid=(n // tm,),
        in_specs=[pl.BlockSpec((tm, d), lambda i: (i, 0)),
                  _resident((1, d)),
                  _resident((d, 2 * GM_WIDTH)),
                  _resident((1, GM_WIDTH)),
                  _resident((1, GM_WIDTH)),
                  _resident((GM_GROUPS, GM_CHUNK, GM_CHUNK)),
                  _resident((GM_CHUNK, GM_GROUPS)),
                  _resident((GM_WIDTH, d)),
                  _resident((1, d))],
        out_specs=pl.BlockSpec((tm, d), lambda i: (i, 0)),
        out_shape=jax.ShapeDtypeStruct((n, d), F32),
        scratch_shapes=[pltpu.VMEM((tm, GM_WIDTH), BF16), pltpu.VMEM((tm, GM_WIDTH), BF16)],
        compiler_params=_cparams("parallel"),
        name="gmlp_layer",
    )(h2d, g_in.reshape(1, d), w_in.astype(BF16), ln_g.reshape(1, -1), ln_b.reshape(1, -1),
      w_s, b_s.T, w_out.astype(BF16), g_out.reshape(1, d))


def _rope_tables(positions, half):
    inv = ROPE_THETA ** (-jnp.arange(half, dtype=F32) / half)
    ang = positions.astype(F32)[..., None] * inv
    return jnp.cos(ang), jnp.sin(ang)


def _nsa_proj_kernel(x_ref, g_ref, w_ref, cos_ref, sin_lo_ref, sin_hi_ref,
                     q_ref, ks_ref, kw_ref, kvc_ref, vst_ref, vwt_ref, gate_ref, *, tiles_per_seq):
    grp, dk, kv = NSA_KV_GROUPS, NSA_HEAD_DIM, NSA_KV
    qw = NSA_HEADS * dk
    tm = x_ref.shape[0]
    a = _rms(x_ref[...], g_ref[...]).astype(BF16)
    cos, sin_lo, sin_hi = cos_ref[...], sin_lo_ref[...], sin_hi_ref[...]

    def rope_chunks(y):
        out = []
        for c in range(y.shape[1] // LANES):
            yc = y[:, c * LANES:(c + 1) * LANES]
            out.append(yc * cos + pltpu.roll(yc, LANES - dk // 2, 1) * sin_lo
                       + pltpu.roll(yc, dk // 2, 1) * sin_hi)
        return out

    def proj(k):
        return _dot(a, w_ref[:, qw + k * kv:qw + (k + 1) * kv])

    for c in range(qw // 512):
        for cc, chunk in enumerate(rope_chunks(_dot(a, w_ref[:, c * 512:(c + 1) * 512]))):
            q_ref[:, c * 512 + cc * LANES:c * 512 + (cc + 1) * LANES] = chunk.astype(q_ref.dtype)

    lane = lax.broadcasted_iota(jnp.int32, (tm, LANES), 1)
    tok = (pl.program_id(0) % tiles_per_seq) * tm + lax.broadcasted_iota(jnp.int32, (tm, LANES), 0)
    onehot = jnp.where((tok >> (SEL_LEN.bit_length() - 1)) == lane - dk, 1.0, 0.0)
    ks_chunks = rope_chunks(proj(0))
    for gi in range(grp):
        chunk = ks_chunks[gi * dk // LANES]
        if (gi * dk) % LANES:
            chunk = pltpu.roll(chunk, LANES - (gi * dk) % LANES, 1)
        ks_ref[0, gi] = jnp.where(lane < dk, chunk, onehot).astype(ks_ref.dtype)

    def store_groups(ref_at, y):
        for gi in range(grp):
            ref_at(gi)[...] = y[:, gi * dk:(gi + 1) * dk].astype(kw_ref.dtype)

    store_groups(lambda gi: kw_ref.at[0, gi], jnp.concatenate(rope_chunks(proj(1)), axis=1))
    store_groups(lambda gi: kvc_ref.at[0, 0, gi], proj(2))
    store_groups(lambda gi: kvc_ref.at[1, 0, gi], proj(3))
    for ref, k in ((vst_ref, 4), (vwt_ref, 5)):
        y_t = proj(k).T
        for gi in range(grp):
            ref[0, gi] = y_t[gi * dk:(gi + 1) * dk].astype(ref.dtype)
    gates_t = _sigmoid(_dot(a, w_ref[:, qw + 6 * kv:])).T
    gate_ref[0] = gates_t[:gate_ref.shape[1]]


def _nsa_proj(h2d, g_in, w, cos_t, sin_lo, sin_hi, b, s):
    n, d = h2d.shape
    grp, dk = NSA_KV_GROUPS, NSA_HEAD_DIM
    tm = ROW_TILE
    per_b = s // tm
    row = lambda i: (i, 0)
    tok_blk = lambda i: (i // per_b, 0, i % per_b, 0)
    tr_blk = lambda i: (i // per_b, 0, 0, i % per_b)
    return pl.pallas_call(
        functools.partial(_nsa_proj_kernel, tiles_per_seq=per_b),
        grid=(n // tm,),
        in_specs=[pl.BlockSpec((tm, d), row), _resident((1, d)), _resident(w.shape),
                  pl.BlockSpec((tm, LANES), row), pl.BlockSpec((tm, LANES), row),
                  pl.BlockSpec((tm, LANES), row)],
        out_specs=[pl.BlockSpec((tm, NSA_HEADS * dk), row),
                   pl.BlockSpec((1, grp, tm, LANES), tok_blk),
                   pl.BlockSpec((1, grp, tm, dk), tok_blk),
                   pl.BlockSpec((2, 1, grp, tm, dk), lambda i: (0, i // per_b, 0, i % per_b, 0)),
                   pl.BlockSpec((1, grp, dk, tm), tr_blk),
                   pl.BlockSpec((1, grp, dk, tm), tr_blk),
                   pl.BlockSpec((1, 3 * NSA_HEADS, tm), lambda i: (i // per_b, 0, i % per_b))],
        out_shape=[jax.ShapeDtypeStruct((n, NSA_HEADS * dk), BF16),
                   jax.ShapeDtypeStruct((b, grp, s, LANES), BF16),
                   jax.ShapeDtypeStruct((b, grp, s, dk), BF16),
                   jax.ShapeDtypeStruct((2, b, grp, s, dk), BF16),
                   jax.ShapeDtypeStruct((b, grp, dk, s), BF16),
                   jax.ShapeDtypeStruct((b, grp, dk, s), BF16),
                   jax.ShapeDtypeStruct((b, 3 * NSA_HEADS, s), F32)],
        compiler_params=_cparams("parallel"),
        name="nsa_proj",
    )(h2d, g_in.reshape(1, d), w, cos_t, sin_lo, sin_hi)


def _cmp_kernel(f_ref, pe_ref, w1_ref, w2_ref, cos_ref, sin_ref, o_ref):
    half = CMP_LEN * NSA_HEAD_DIM // 2
    x = f_ref[0, 0, 0].astype(F32)
    pe = pe_ref[0]
    first = _dot((x + pe[:, :half]).astype(BF16), w1_ref[0, :half, :])
    second = _dot((x + pe[:, half:]).astype(BF16), w1_ref[0, half:, :])
    hid = _gelu_tanh(first + pltpu.roll(second, second.shape[0] - 1, 0))
    y = _dot(hid.astype(BF16), w2_ref[0])
    out = y[:, :NSA_HEAD_DIM] * cos_ref[0, 0] + y[:, NSA_HEAD_DIM:] * sin_ref[0, 0]
    o_ref[0, 0, 0] = out.astype(o_ref.dtype)


def _compress(f, pe_flat, w1, w2cat, cos_t, sin_t):
    _, b, g, n_grp, width = f.shape
    dk = NSA_HEAD_DIM
    return pl.pallas_call(
        _cmp_kernel,
        grid=(2, b, g),
        in_specs=[pl.BlockSpec((1, 1, 1, n_grp, width), lambda s, i, j: (s, i, j, 0, 0)),
                  pl.BlockSpec((1, 1, pe_flat.shape[-1]), lambda s, i, j: (s, 0, 0)),
                  pl.BlockSpec((1,) + w1.shape[1:], lambda s, i, j: (s, 0, 0)),
                  pl.BlockSpec((1,) + w2cat.shape[1:], lambda s, i, j: (s, 0, 0)),
                  pl.BlockSpec((1, 1, n_grp, dk), lambda s, i, j: (s, i, 0, 0)),
                  pl.BlockSpec((1, 1, n_grp, dk), lambda s, i, j: (s, i, 0, 0))],
        out_specs=pl.BlockSpec((1, 1, 1, n_grp, dk), lambda s, i, j: (s, i, j, 0, 0)),
        out_shape=jax.ShapeDtypeStruct((2, b, g, n_grp, dk), BF16),
        compiler_params=_cparams("parallel", "parallel", "parallel"),
        name="nsa_compress",
    )(f, pe_flat, w1, w2cat, cos_t, sin_t)


def _nsa_attn_kernel(q_ref, ks_ref, vst_ref, kw_ref, vwt_ref, kc_ref, vct_ref, gate_ref,
                     o_ref, os_scr, score_scr):
    tq, tk, hpg, dk = ATT_TQ, ATT_TK, NSA_HPG, NSA_HEAD_DIM
    ratio = SEL_LEN // CMP_STRIDE
    ratio_shift = ratio.bit_length() - 1
    sel_shift = SEL_LEN.bit_length() - 1
    n_sel = LANES // ratio
    i = pl.program_id(2)
    t0 = i * tq

    def tile_heads(x):
        return jnp.concatenate([x] * hpg, axis=1)

    q_t = q_ref[...].astype(F32).T
    qt = jnp.concatenate([q_t[h * dk:(h + 1) * dk] for h in range(hpg)], axis=1)
    qt = (qt * (dk ** -0.5 * LOG2_E)).astype(BF16)

    def softmax_pv(parts):
        maxes, accs = [], []
        for s, vt in parts:
            m_part = jnp.max(s, axis=0, keepdims=True)
            p = jnp.exp2(s - m_part).astype(BF16)
            vt_ones = jnp.concatenate([vt, jnp.ones((BF16_SUBLANES, vt.shape[1]), BF16)], axis=0)
            maxes.append(m_part)
            accs.append(_dot(vt_ones, p))
        m = functools.reduce(jnp.maximum, maxes)
        acc = sum(jnp.exp2(m_part - m) * a for m_part, a in zip(maxes, accs))
        return acc[:dk] * (1.0 / jnp.maximum(acc[dk:dk + 1], 1e-30))

    n_win = WIN + tq
    k_lo = pl.multiple_of(jnp.maximum(t0 - WIN, 0), tk)
    n_half = n_win // 2
    row_minus_col = (lax.broadcasted_iota(jnp.int32, (n_half, tq), 0)
                     - lax.broadcasted_iota(jnp.int32, (n_half, tq), 1))
    parts_w = []
    for c in range(2):
        k0 = pl.multiple_of(k_lo + c * n_half, tk)
        back = lax.bitcast_convert_type((t0 - k0) - row_minus_col, jnp.uint32)
        bias_w = jnp.where(back < jnp.uint32(WIN), 0.0, NEG_BIG)
        s_w = _dot(kw_ref[0, 0, pl.ds(k0, n_half), :], qt) + tile_heads(bias_w)
        parts_w.append((s_w, vwt_ref[0, 0, :, pl.ds(k0, n_half)]))
    o_w = softmax_pv(parts_w)

    n_cmp = kc_ref.shape[2]
    row_c = lax.broadcasted_iota(jnp.int32, (n_cmp, tq), 0)
    tok_c = t0 + lax.broadcasted_iota(jnp.int32, (n_cmp, tq), 1)
    n_idx = ((row_c & (n_sel - 1)) << ratio_shift) + (row_c >> (n_sel.bit_length() - 1))
    bias_c = jnp.where(n_idx * CMP_STRIDE + (CMP_LEN - 1) <= tok_c, 0.0, NEG_BIG)
    s_c = _dot(kc_ref[0, 0], qt) + tile_heads(bias_c)
    e_c = jnp.exp2(s_c - jnp.max(s_c, axis=0, keepdims=True))
    tok_row = t0 + (lax.broadcasted_iota(jnp.int32, (1, hpg * tq), 1) & (tq - 1))
    inv_l = jnp.where(tok_row >= CMP_LEN - 1,
                      1.0 / jnp.maximum(jnp.sum(e_c, axis=0, keepdims=True), 1e-30), 0.0)
    p_c = e_c * inv_l
    o_c = _dot(vct_ref[0, 0], p_c.astype(BF16))

    p_sum = p_c[:, 0:tq]
    for h in range(1, hpg):
        p_sum = p_sum + p_c[:, h * tq:(h + 1) * tq]
    imp = p_sum[0:n_sel]
    for r in range(1, ratio):
        imp = imp + p_sum[r * n_sel:(r + 1) * n_sel]

    blk = lax.broadcasted_iota(jnp.int32, (n_sel, tq), 0)
    cur = (t0 + lax.broadcasted_iota(jnp.int32, (n_sel, tq), 1)) >> sel_shift
    forced = (blk == 0) | (blk == cur) | (blk == cur - 1)
    score = jnp.where(blk <= cur, jnp.where(forced, jnp.inf, imp), -jnp.inf)
    score_scr[...] = score
    grp_rows = 8
    row_in_grp = lax.broadcasted_iota(jnp.int32, (grp_rows, tq), 0)
    groups = [score[a:a + grp_rows] for a in range(0, n_sel, grp_rows)]
    ranks = [jnp.zeros((grp_rows, tq), F32) for _ in groups]
    for rival in range(n_sel):
        other = score_scr[rival:rival + 1, :]
        for a, mine in enumerate(groups):
            lo = a * grp_rows
            if rival < lo:
                beats = other >= mine
            elif rival >= lo + grp_rows:
                beats = other > mine
            else:
                beats = (other > mine) | ((other == mine) & (row_in_grp + lo > rival))
            ranks[a] = ranks[a] + jnp.where(beats, 1.0, 0.0)
    rank = jnp.concatenate(ranks, axis=0)
    sel_bias = jnp.where(rank < float(N_SELECT), 0.0, NEG_BIG).astype(BF16)

    pad_rows = ks_ref.shape[3] - dk - n_sel
    qt_aug = jnp.concatenate([qt, tile_heads(sel_bias), jnp.zeros((pad_rows, hpg * tq), BF16)], axis=0)
    tok_s = t0 + lax.broadcasted_iota(jnp.int32, (SEL_CHUNK, tq), 1)
    key_s = lax.broadcasted_iota(jnp.int32, (SEL_CHUNK, tq), 0)
    n_variants = ks_ref.shape[2] // SEL_CHUNK
    for v in range(n_variants):
        @pl.when((t0 >> (SEL_CHUNK.bit_length() - 1)) == v)
        def _():
            parts = []
            for c in range(v + 1):
                rows = slice(c * SEL_CHUNK, (c + 1) * SEL_CHUNK)
                s = _dot(ks_ref[0, 0, rows, :], qt_aug)
                if c == v:
                    s = s + tile_heads(jnp.where(c * SEL_CHUNK + key_s <= tok_s, 0.0, NEG_BIG))
                parts.append((s, vst_ref[0, 0, :, rows]))
            os_scr[...] = softmax_pv(parts)
    o_s = os_scr[...]

    row0 = pl.program_id(1) * (3 * hpg)
    gate = lambda c: jnp.concatenate([gate_ref[0, pl.ds(row0 + 3 * h + c, 1), :] for h in range(hpg)], axis=1)
    out_t = gate(0) * o_c + gate(1) * o_s + gate(2) * o_w
    stacked = jnp.concatenate([out_t[:, h * tq:(h + 1) * tq] for h in range(hpg)], axis=0)
    o_ref[...] = stacked.T.astype(o_ref.dtype)


def _nsa_attention(q2d, ks_aug, vst, kw, vwt, kc, vct, gates):
    b, g, s, dk = kw.shape
    hpg = NSA_HPG
    n_cmp = kc.shape[2]
    tq = ATT_TQ
    nt = s // tq
    assert s % SEL_CHUNK == 0 and SEL_CHUNK % tq == 0 and s >= WIN + tq
    k_spec = pl.BlockSpec((1, 1, s, dk), lambda bi, gi, i: (bi, gi, 0, 0))
    vt_spec = pl.BlockSpec((1, 1, dk, s), lambda bi, gi, i: (bi, gi, 0, 0))
    return pl.pallas_call(
        _nsa_attn_kernel,
        grid=(b, g, nt),
        in_specs=[pl.BlockSpec((tq, hpg * dk), lambda bi, gi, i: (bi * nt + i, gi)),
                  pl.BlockSpec((1, 1, s, ks_aug.shape[3]), lambda bi, gi, i: (bi, gi, 0, 0)),
                  vt_spec, k_spec, vt_spec,
                  pl.BlockSpec((1, 1, n_cmp, dk), lambda bi, gi, i: (bi, gi, 0, 0)),
                  pl.BlockSpec((1, 1, dk, n_cmp), lambda bi, gi, i: (bi, gi, 0, 0)),
                  pl.BlockSpec((1, gates.shape[1], tq), lambda bi, gi, i: (bi, 0, i))],
        out_specs=pl.BlockSpec((tq, hpg * dk), lambda bi, gi, i: (bi * nt + i, gi)),
        out_shape=jax.ShapeDtypeStruct((b * s, g * hpg * dk), BF16),
        scratch_shapes=[pltpu.VMEM((dk, hpg * tq), F32), pltpu.VMEM((n_cmp * CMP_STRIDE // SEL_LEN, tq), F32)],
        compiler_params=_cparams("parallel", "parallel", "parallel"),
        name="nsa_attention",
    )(q2d, ks_aug, vst, kw, vwt, kc, vct, gates)


def _nsa_layer(h2d, cos, sin, g_in, w_in, cmp_pe, cmp_w1, cmp_w2, w_out, g_out):
    b, s, _ = cos.shape
    n, d = h2d.shape
    heads, grp, hpg, dk = NSA_HEADS, NSA_KV_GROUPS, NSA_HPG, NSA_HEAD_DIM
    half = dk // 2
    qw = heads * dk
    col = lambda k: w_in[:, qw + k * NSA_KV: qw + (k + 1) * NSA_KV]
    w_gate = jnp.pad(w_in[:, qw + 6 * NSA_KV:], ((0, 0), (0, LANES - 3 * heads)))
    w_all = jnp.concatenate([w_in[:, :qw], col(2), col(4), col(0), col(1), col(3), col(5), w_gate],
                            axis=1).astype(BF16)

    zeros = jnp.zeros_like(sin)
    lane_tile = lambda t: jnp.tile(t, (1, 1, LANES // t.shape[-1])).reshape(n, LANES)
    cos_t = lane_tile(cos)
    sin_lo = lane_tile(jnp.concatenate([-sin, zeros], axis=-1))
    sin_hi = lane_tile(jnp.concatenate([zeros, sin], axis=-1))
    q2d, ks_aug, kw, kvc, vst, vwt, gates = _nsa_proj(h2d, g_in, w_all, cos_t, sin_lo, sin_hi, b, s)

    n_grp = s // CMP_STRIDE
    f = kvc.reshape(2, b, grp, n_grp, CMP_STRIDE * dk)
    pe_flat = cmp_pe.reshape(2, 1, CMP_LEN * dk)
    rot = lambda w: jnp.concatenate([-w[..., half:], w[..., :half]], axis=-1)
    w2cat = jnp.concatenate([cmp_w2, rot(cmp_w2)], axis=-1).astype(BF16)
    n_cmp = (s - CMP_LEN) // CMP_STRIDE + 1
    pad = ((0, 0), (0, n_grp - n_cmp), (0, 0))
    cos_e = jnp.pad(cos[:, CMP_LEN - 1::CMP_STRIDE], pad, constant_values=1.0)
    sin_e = jnp.pad(sin[:, CMP_LEN - 1::CMP_STRIDE], pad)
    cos_c = jnp.stack([jnp.concatenate([cos_e, cos_e], -1), jnp.ones((b, n_grp, dk), F32)])
    sin_c = jnp.stack([jnp.concatenate([sin_e, sin_e], -1), jnp.zeros((b, n_grp, dk), F32)])
    cmp = _compress(f, pe_flat, cmp_w1.astype(BF16), w2cat, cos_c, sin_c)
    ratio = SEL_LEN // CMP_STRIDE
    cmp = (cmp.reshape(2, b, grp, n_grp // ratio, ratio, dk).transpose(0, 1, 2, 4, 3, 5)
           .reshape(2, b, grp, n_grp, dk))

    o2d = _nsa_attention(q2d, ks_aug, vst, kw, vwt, cmp[0], cmp[1].transpose(0, 1, 3, 2), gates)
    return _proj_norm_residual(o2d, w_out.astype(BF16), g_out, h2d)


def _ret_proj_kernel(x_ref, g_ref, w_ref, cos_ref, sin_ref, qk_ref, vg_ref):
    heads, dk = RET_HEADS, RET_QK_DIM
    half = dk // 2
    a = _rms(x_ref[...], g_ref[...]).astype(BF16)
    cos, sin = cos_ref[...], sin_ref[...]
    for c in range(2 * heads):
        y = _dot(a, w_ref[:, c * dk:(c + 1) * dk])
        y1, y2 = y[:, :half], y[:, half:]
        scale = 1.0 if c < heads else dk ** -0.5
        qk_ref[:, c * dk:c * dk + half] = ((y1 * cos - y2 * sin) * scale).astype(qk_ref.dtype)
        qk_ref[:, c * dk + half:(c + 1) * dk] = ((y2 * cos + y1 * sin) * scale).astype(qk_ref.dtype)
    n_qk = 2 * heads * dk
    for c in range(vg_ref.shape[1] // RET_V_DIM):
        cols = slice(c * RET_V_DIM, (c + 1) * RET_V_DIM)
        vg_ref[:, cols] = _dot(a, w_ref[:, n_qk + c * RET_V_DIM:n_qk + (c + 1) * RET_V_DIM]).astype(vg_ref.dtype)


def _retention_proj(h2d, g_in, w, cos, sin):
    n, d = h2d.shape
    n_qk = 2 * RET_HEADS * RET_QK_DIM
    n_vg = w.shape[1] - n_qk
    tm = ROW_TILE
    return pl.pallas_call(
        _ret_proj_kernel,
        grid=(n // tm,),
        in_specs=[pl.BlockSpec((tm, d), lambda i: (i, 0)),
                  _resident((1, d)),
                  _resident(w.shape),
                  pl.BlockSpec((tm, cos.shape[1]), lambda i: (i, 0)),
                  pl.BlockSpec((tm, sin.shape[1]), lambda i: (i, 0))],
        out_specs=[pl.BlockSpec((tm, n_qk), lambda i: (i, 0)),
                   pl.BlockSpec((tm, n_vg), lambda i: (i, 0))],
        out_shape=[jax.ShapeDtypeStruct((n, n_qk), BF16), jax.ShapeDtypeStruct((n, n_vg), BF16)],
        compiler_params=_cparams("parallel"),
        name="retention_proj",
    )(h2d, g_in.reshape(1, d), w, cos, sin)


def _ret_kernel(qk_ref, vg_ref, gn_ref, decay_ref, xi_ref, zeta_ref, gc_ref, o_ref, state_scr):
    heads, dk, dv = RET_HEADS, RET_QK_DIM, RET_V_DIM

    @pl.when(pl.program_id(1) == 0)
    def _():
        state_scr[...] = jnp.zeros(state_scr.shape, F32)

    for h in range(heads):
        q = qk_ref[:, h * dk:(h + 1) * dk]
        k = qk_ref[:, (heads + h) * dk:(heads + h + 1) * dk]
        v = vg_ref[:, h * dv:(h + 1) * dv]
        s = lax.dot_general(q, k, (((1,), (1,)), ((), ())), preferred_element_type=F32)
        inner = _dot((s * decay_ref[h]).astype(BF16), v)
        state = state_scr[h]
        cross = _dot(q, state.astype(BF16)) * xi_ref[h]
        kz = (k.astype(F32) * zeta_ref[h]).T.astype(BF16)
        state_scr[h] = gc_ref[h] * state + _dot(kz, v)

        o = inner + cross
        mu = jnp.mean(o, axis=-1, keepdims=True)
        oc = o - mu
        var = jnp.mean(oc * oc, axis=-1, keepdims=True)
        on = oc * lax.rsqrt(var + LN_EPS) * gn_ref[:, h * dv:(h + 1) * dv]
        gate = vg_ref[:, (heads + h) * dv:(heads + h + 1) * dv].astype(F32)
        o_ref[:, h * dv:(h + 1) * dv] = (gate * _sigmoid(gate) * on).astype(o_ref.dtype)


def _retention_core(qk, vg, gn_g, b, s):
    heads, dk, dv, c = RET_HEADS, RET_QK_DIM, RET_V_DIM, RET_CHUNK
    n_ch = s // c
    log_g = jnp.log1p(-(2.0 ** (-5.0 - jnp.arange(heads, dtype=F32))))
    ix = jnp.arange(c, dtype=F32)
    rel = ix[:, None] - ix[None, :]
    decay = jnp.where(rel >= 0, jnp.exp(log_g[:, None, None] * jnp.maximum(rel, 0.0)), 0.0)
    xi = jnp.exp(log_g[:, None] * (ix + 1.0))[:, :, None]
    zeta = jnp.exp(log_g[:, None] * (c - 1.0 - ix))[:, :, None]
    g_chunk = jnp.exp(log_g * c)[:, None, None]
    return pl.pallas_call(
        _ret_kernel,
        grid=(b, n_ch),
        in_specs=[pl.BlockSpec((c, qk.shape[1]), lambda bi, ci: (bi * n_ch + ci, 0)),
                  pl.BlockSpec((c, vg.shape[1]), lambda bi, ci: (bi * n_ch + ci, 0)),
                  _resident((1, heads * dv)),
                  _resident(decay.shape), _resident(xi.shape), _resident(zeta.shape),
                  _resident(g_chunk.shape)],
        out_specs=pl.BlockSpec((c, heads * dv), lambda bi, ci: (bi * n_ch + ci, 0)),
        out_shape=jax.ShapeDtypeStruct((b * s, heads * dv), BF16),
        scratch_shapes=[pltpu.VMEM((heads, dk, dv), F32)],
        compiler_params=_cparams("parallel", "arbitrary"),
        name="retention_core",
    )(qk, vg, gn_g.reshape(1, -1), decay, xi, zeta, g_chunk)


def _retention_layer(h2d, cos, sin, g_in, w_in, gn_g, w_out, g_out, b, s):
    qk, vg = _retention_proj(h2d, g_in, w_in.astype(BF16), cos, sin)
    y = _retention_core(qk, vg, gn_g, b, s)
    return _proj_norm_residual(y, w_out.astype(BF16), g_out, h2d)


def kernel(x, positions, norm_g, ffn_w_gate, ffn_w_up, ffn_w_down, gm_w_in, gm_ln_g, gm_ln_b, gm_w_s, gm_b_s, gm_w_out, nsa_w_in, nsa_cmp_pe, nsa_cmp_w1, nsa_cmp_w2, nsa_w_out, ret_w_in, ret_gn_g, ret_w_out):
    b, s, d = x.shape
    h = x.reshape(b * s, d)
    cos_ret, sin_ret = _rope_tables(positions, RET_QK_DIM // 2)
    cos_nsa, sin_nsa = _rope_tables(positions, NSA_HEAD_DIM // 2)
    for i in range(DEPTH):
        kind, j = i % N_MIXERS, i // N_MIXERS
        if kind == 0:
            h = _gmlp_layer(h, norm_g[i, 0], gm_w_in[j], gm_ln_g[j], gm_ln_b[j], gm_w_s[j],
                            gm_b_s[j], gm_w_out[j], norm_g[i, 1])
        elif kind == 1:
            h = _nsa_layer(h, cos_nsa, sin_nsa, norm_g[i, 0], nsa_w_in[j],
                           nsa_cmp_pe[j], nsa_cmp_w1[j], nsa_cmp_w2[j], nsa_w_out[j], norm_g[i, 1])
        else:
            h = _retention_layer(h, cos_ret.reshape(b * s, -1), sin_ret.reshape(b * s, -1),
                                 norm_g[i, 0], ret_w_in[j], ret_gn_g[j], ret_w_out[j], norm_g[i, 1], b, s)
        h = _ffn(h, norm_g[i, 2], ffn_w_gate[i].astype(BF16), ffn_w_up[i].astype(BF16),
                 ffn_w_down[i].astype(BF16), norm_g[i, 3])
    return h.reshape(b, s, d)
```

```python
import functools
import math

import jax
import jax.numpy as jnp
from jax import lax
from jax.experimental import pallas as pl
from jax.experimental.pallas import tpu as pltpu

F32 = jnp.float32
BF16 = jnp.bfloat16

D_MODEL = 1024
DEPTH = 4
N_MIXERS = 3
RMS_EPS = 1e-6
LN_EPS = 1e-5
ROPE_THETA = 10000.0
D_FF = 2816

GM_CHUNK = 128
GM_WIDTH = 2 * D_MODEL
GM_GROUPS = 8
GM_GROUP_W = GM_WIDTH // GM_GROUPS

NSA_HEADS = 16
NSA_KV_GROUPS = 4
NSA_HPG = NSA_HEADS // NSA_KV_GROUPS
NSA_HEAD_DIM = 64
NSA_KV = NSA_KV_GROUPS * NSA_HEAD_DIM
CMP_LEN = 32
CMP_STRIDE = 16
CMP_HIDDEN = 4 * NSA_HEAD_DIM
SEL_LEN = 64
N_SELECT = 8
WIN = 512

RET_HEADS = 4
RET_QK_DIM = 256
RET_V_DIM = 512
RET_CHUNK = 128

LANES = 128
BF16_SUBLANES = 16
LOG2_E = math.log2(math.e)
VMEM_LIMIT = 48 * 1024 * 1024
NEG_BIG = -1e30

ROW_TILE = 512
GM_ROW_TILE = 256
FFN_COL_CHUNK = 256
ATT_TQ = 256
ATT_TK = 128
SEL_CHUNK = 512


def _cparams(*sem):
    return pltpu.CompilerParams(dimension_semantics=sem, vmem_limit_bytes=VMEM_LIMIT)


def _resident(shape):
    nd = len(shape)
    return pl.BlockSpec(shape, lambda *_: (0,) * nd, pipeline_mode=pl.Buffered(1))


def _resident_layer(stacked_shape, layer):
    nd = len(stacked_shape)
    return pl.BlockSpec((None,) + tuple(stacked_shape[1:]), lambda *_: (layer,) + (0,) * (nd - 1),
                        pipeline_mode=pl.Buffered(1))


def _rms(x32, g):
    ms = jnp.mean(x32 * x32, axis=-1, keepdims=True)
    return x32 * lax.rsqrt(ms + RMS_EPS) * g


def _gelu_tanh(x):
    c = math.sqrt(2.0 / math.pi)
    return x * (0.5 * (1.0 + jnp.tanh(c * (x + 0.044715 * (x * x * x)))))


def _sigmoid(x):
    return 1.0 / (1.0 + jnp.exp(-x))


def _dot(a, b):
    return jnp.dot(a, b, preferred_element_type=F32)


def _out_kernel(m_ref, w_ref, g_ref, h_ref, o_ref):
    y = _dot(m_ref[...], w_ref[...])
    o_ref[...] = h_ref[...] + _rms(y, g_ref[...])


def _proj_norm_residual(m2d, w, g, h2d):
    n, k = m2d.shape
    d = w.shape[1]
    tm = ROW_TILE
    return pl.pallas_call(
        _out_kernel,
        grid=(n // tm,),
        in_specs=[pl.BlockSpec((tm, k), lambda i: (i, 0)),
                  _resident((k, d)),
                  _resident((1, d)),
                  pl.BlockSpec((tm, d), lambda i: (i, 0))],
        out_specs=pl.BlockSpec((tm, d), lambda i: (i, 0)),
        out_shape=jax.ShapeDtypeStruct((n, d), F32),
        compiler_params=_cparams("parallel"),
        name="proj_norm_residual",
    )(m2d, w, g.reshape(1, d), h2d)


def _ffn_kernel(h_ref, g_in_ref, wg_ref, wu_ref, wd_ref, g_out_ref, o_ref, acc_ref):
    x = h_ref[...]
    a = _rms(x, g_in_ref[...]).astype(BF16)
    for c in range(D_FF // FFN_COL_CHUNK):
        cols = slice(c * FFN_COL_CHUNK, (c + 1) * FFN_COL_CHUNK)
        gate = _dot(a, wg_ref[:, cols])
        up = _dot(a, wu_ref[:, cols])
        act = (gate * _sigmoid(gate) * up).astype(BF16)
        part = _dot(act, wd_ref[cols, :])
        if c == 0:
            acc_ref[...] = part
        else:
            acc_ref[...] += part
    o_ref[...] = x + _rms(acc_ref[...], g_out_ref[...])


def _ffn(h2d, g_in, wg, wu, wd, g_out, layer):
    n, d = h2d.shape
    tm = ROW_TILE
    return pl.pallas_call(
        _ffn_kernel,
        grid=(n // tm,),
        in_specs=[pl.BlockSpec((tm, d), lambda i: (i, 0)),
                  _resident((1, d)),
                  _resident_layer(wg.shape, layer),
                  _resident_layer(wu.shape, layer),
                  _resident_layer(wd.shape, layer),
                  _resident((1, d))],
        out_specs=pl.BlockSpec((tm, d), lambda i: (i, 0)),
        out_shape=jax.ShapeDtypeStruct((n, d), F32),
        scratch_shapes=[pltpu.VMEM((tm, d), F32)],
        compiler_params=_cparams("parallel"),
        name="swiglu_ffn",
    )(h2d, g_in.reshape(1, d), wg, wu, wd, g_out.reshape(1, d))


def _gmlp_kernel(h_ref, g_in_ref, win_ref, lng_ref, lnb_ref, ws_ref, bst_ref, wout_ref,
                 g_out_ref, o_ref, vn_scr, y_scr):
    x = h_ref[...]
    tm = x.shape[0]
    a = _rms(x, g_in_ref[...]).astype(BF16)

    v = _gelu_tanh(_dot(a, win_ref[:, GM_WIDTH:]))
    for grp in range(GM_GROUPS):
        cols = slice(grp * GM_GROUP_W, (grp + 1) * GM_GROUP_W)
        y_scr[:, cols] = _gelu_tanh(_dot(a, win_ref[:, cols])).astype(BF16)
    mu = jnp.mean(v, axis=-1, keepdims=True)
    vc = v - mu
    var = jnp.mean(vc * vc, axis=-1, keepdims=True)
    vn_scr[...] = (vc * lax.rsqrt(var + LN_EPS) * lng_ref[...] + lnb_ref[...]).astype(BF16)

    t_out = lax.broadcasted_iota(jnp.int32, (GM_CHUNK, GM_CHUNK), 0)
    t_in = lax.broadcasted_iota(jnp.int32, (GM_CHUNK, GM_CHUNK), 1)
    causal = t_in <= t_out
    bst = bst_ref[...]
    for grp in range(GM_GROUPS):
        cols = slice(grp * GM_GROUP_W, (grp + 1) * GM_GROUP_W)
        w_mix = jnp.where(causal, ws_ref[grp], 0.0).astype(BF16)
        bias = bst[:, grp:grp + 1]
        for c in range(tm // GM_CHUNK):
            rows = slice(c * GM_CHUNK, (c + 1) * GM_CHUNK)
            mixed = _dot(w_mix, vn_scr[rows, cols]) + bias
            y_scr[rows, cols] = (y_scr[rows, cols].astype(F32) * mixed).astype(BF16)

    m = _dot(y_scr[...], wout_ref[...])
    o_ref[...] = x + _rms(m, g_out_ref[...])


def _gmlp_layer(h2d, g_in, w_in, ln_g, ln_b, w_s, b_s, w_out, g_out, layer):
    n, d = h2d.shape
    tm = GM_ROW_TILE
    return pl.pallas_call(
        _gmlp_kernel,
        grid=(n // tm,),
        in_specs=[pl.BlockSpec((tm, d), lambda i: (i, 0)),
                  _resident((1, d)),
                  _resident_layer(w_in.shape, layer),
                  _resident((1, GM_WIDTH)),
                  _resident((1, GM_WIDTH)),
                  _resident_layer(w_s.shape, layer),
                  _resident((GM_CHUNK, GM_GROUPS)),
                  _resident_layer(w_out.shape, layer),
                  _resident((1, d))],
        out_specs=pl.BlockSpec((tm, d), lambda i: (i, 0)),
        out_shape=jax.ShapeDtypeStruct((n, d), F32),
        scratch_shapes=[pltpu.VMEM((tm, GM_WIDTH), BF16), pltpu.VMEM((tm, GM_WIDTH), BF16)],
        compiler_params=_cparams("parallel"),
        name="gmlp_layer",
    )(h2d, g_in.reshape(1, d), w_in, ln_g.reshape(1, -1), ln_b.reshape(1, -1),
      w_s, b_s.T, w_out, g_out.reshape(1, d))


def _rope_tables(positions, half):
    inv = ROPE_THETA ** (-jnp.arange(half, dtype=F32) / half)
    ang = positions.astype(F32)[..., None] * inv
    return jnp.cos(ang), jnp.sin(ang)


def _nsa_proj_kernel(x_ref, g_ref, w_ref, cos_ref, sin_ref,
                     q_ref, ks_ref, kw_ref, kvc_ref, vst_ref, vwt_ref, gate_ref, y_scr, *, tiles_per_seq):
    grp, dk, kv = NSA_KV_GROUPS, NSA_HEAD_DIM, NSA_KV
    qw = NSA_HEADS * dk
    tm = x_ref.shape[0]
    a = _rms(x_ref[...], g_ref[...]).astype(BF16)
    lane = lax.broadcasted_iota(jnp.int32, (tm, LANES), 1)
    first_half = (lane & (dk - 1)) < dk // 2
    cos = cos_ref[...]
    sin_lo = jnp.where(first_half, -sin_ref[...], 0.0)
    sin_hi = jnp.where(first_half, 0.0, sin_ref[...])

    def rope_chunks(y):
        out = []
        for c in range(y.shape[1] // LANES):
            yc = y[:, c * LANES:(c + 1) * LANES]
            out.append(yc * cos + pltpu.roll(yc, LANES - dk // 2, 1) * sin_lo
                       + pltpu.roll(yc, dk // 2, 1) * sin_hi)
        return out

    def proj(k):
        return _dot(a, w_ref[:, qw + k * kv:qw + (k + 1) * kv])

    for c in range(qw // 512):
        for cc, chunk in enumerate(rope_chunks(_dot(a, w_ref[:, c * 512:(c + 1) * 512]))):
            q_ref[:, c * 512 + cc * LANES:c * 512 + (cc + 1) * LANES] = chunk.astype(q_ref.dtype)

    tok = (pl.program_id(0) % tiles_per_seq) * tm + lax.broadcasted_iota(jnp.int32, (tm, LANES), 0)
    onehot = jnp.where((tok >> (SEL_LEN.bit_length() - 1)) == lane - dk, 1.0, 0.0)
    ks_chunks = rope_chunks(proj(0))
    for gi in range(grp):
        chunk = ks_chunks[gi * dk // LANES]
        if (gi * dk) % LANES:
            chunk = pltpu.roll(chunk, LANES - (gi * dk) % LANES, 1)
        ks_ref[0, gi] = jnp.where(lane < dk, chunk, onehot).astype(ks_ref.dtype)

    def store_groups(ref_at, y):
        for gi in range(grp):
            ref_at(gi)[...] = y[:, gi * dk:(gi + 1) * dk].astype(kw_ref.dtype)

    store_groups(lambda gi: kw_ref.at[0, gi], jnp.concatenate(rope_chunks(proj(1)), axis=1))
    n_rows = tm // CMP_STRIDE
    for which in range(2):
        y = proj(2 + which)
        for c in range(kv // LANES):
            y_scr[c] = y[:, c * LANES:(c + 1) * LANES]
        for l in range(CMP_STRIDE):
            for c in range(kv // LANES):
                rows = y_scr[c, pl.ds(l, n_rows, stride=CMP_STRIDE), :]
                for sub in range(LANES // dk):
                    kvc_ref[which, 0, c * (LANES // dk) + sub, :, l * dk:(l + 1) * dk] = (
                        rows[:, sub * dk:(sub + 1) * dk].astype(kvc_ref.dtype))
    for ref, k in ((vst_ref, 4), (vwt_ref, 5)):
        y_t = proj(k).T
        for gi in range(grp):
            ref[0, gi] = y_t[gi * dk:(gi + 1) * dk].astype(ref.dtype)
    gates_t = _sigmoid(_dot(a, w_ref[:, qw + 6 * kv:])).T
    gate_ref[0] = gates_t[:gate_ref.shape[1]]


def _nsa_proj(h2d, g_in, w, cos_t, sin_t, b, s):
    n, d = h2d.shape
    grp, dk = NSA_KV_GROUPS, NSA_HEAD_DIM
    tm = ROW_TILE
    per_b = s // tm
    row = lambda i: (i, 0)
    tok_blk = lambda i: (i // per_b, 0, i % per_b, 0)
    tr_blk = lambda i: (i // per_b, 0, 0, i % per_b)
    return pl.pallas_call(
        functools.partial(_nsa_proj_kernel, tiles_per_seq=per_b),
        grid=(n // tm,),
        in_specs=[pl.BlockSpec((tm, d), row), _resident((1, d)), _resident(w.shape),
                  pl.BlockSpec((tm, LANES), row), pl.BlockSpec((tm, LANES), row)],
        out_specs=[pl.BlockSpec((tm, NSA_HEADS * dk), row),
                   pl.BlockSpec((1, grp, tm, LANES), tok_blk),
                   pl.BlockSpec((1, grp, tm, dk), tok_blk),
                   pl.BlockSpec((2, 1, grp, tm // CMP_STRIDE, CMP_STRIDE * dk),
                                lambda i: (0, i // per_b, 0, i % per_b, 0)),
                   pl.BlockSpec((1, grp, dk, tm), tr_blk),
                   pl.BlockSpec((1, grp, dk, tm), tr_blk),
                   pl.BlockSpec((1, 3 * NSA_HEADS, tm), lambda i: (i // per_b, 0, i % per_b))],
        out_shape=[jax.ShapeDtypeStruct((n, NSA_HEADS * dk), BF16),
                   jax.ShapeDtypeStruct((b, grp, s, LANES), BF16),
                   jax.ShapeDtypeStruct((b, grp, s, dk), BF16),
                   jax.ShapeDtypeStruct((2, b, grp, s // CMP_STRIDE, CMP_STRIDE * dk), BF16),
                   jax.ShapeDtypeStruct((b, grp, dk, s), BF16),
                   jax.ShapeDtypeStruct((b, grp, dk, s), BF16),
                   jax.ShapeDtypeStruct((b, 3 * NSA_HEADS, s), F32)],
        scratch_shapes=[pltpu.VMEM((NSA_KV // LANES, tm, LANES), F32)],
        compiler_params=_cparams("parallel"),
        name="nsa_proj",
    )(h2d, g_in.reshape(1, d), w, cos_t, sin_t)


def _cmp_kernel(f_ref, pe_ref, w1_ref, w2_ref, cos_ref, sin_ref, o_ref):
    half = CMP_LEN * NSA_HEAD_DIM // 2
    x = f_ref[0, 0, 0].astype(F32)
    pe = pe_ref[0]
    first = _dot((x + pe[:, :half]).astype(BF16), w1_ref[0, :half, :])
    second = _dot((x + pe[:, half:]).astype(BF16), w1_ref[0, half:, :])
    hid = _gelu_tanh(first + pltpu.roll(second, second.shape[0] - 1, 0))
    y = _dot(hid.astype(BF16), w2_ref[0])
    out = y[:, :NSA_HEAD_DIM] * cos_ref[0, 0] + y[:, NSA_HEAD_DIM:] * sin_ref[0, 0]
    o_ref[0, 0, 0] = out.astype(o_ref.dtype)


def _compress(f, pe_flat, w1, w2cat, cos_t, sin_t):
    _, b, g, n_grp, width = f.shape
    dk = NSA_HEAD_DIM
    return pl.pallas_call(
        _cmp_kernel,
        grid=(2, b, g),
        in_specs=[pl.BlockSpec((1, 1, 1, n_grp, width), lambda s, i, j: (s, i, j, 0, 0)),
                  pl.BlockSpec((1, 1, pe_flat.shape[-1]), lambda s, i, j: (s, 0, 0)),
                  pl.BlockSpec((1,) + w1.shape[1:], lambda s, i, j: (s, 0, 0)),
                  pl.BlockSpec((1,) + w2cat.shape[1:], lambda s, i, j: (s, 0, 0)),
                  pl.BlockSpec((1, 1, n_grp, dk), lambda s, i, j: (s, i, 0, 0)),
                  pl.BlockSpec((1, 1, n_grp, dk), lambda s, i, j: (s, i, 0, 0))],
        out_specs=pl.BlockSpec((1, 1, 1, n_grp, dk), lambda s, i, j: (s, i, j, 0, 0)),
        out_shape=jax.ShapeDtypeStruct((2, b, g, n_grp, dk), BF16),
        compiler_params=_cparams("parallel", "parallel", "parallel"),
        name="nsa_compress",
    )(f, pe_flat, w1, w2cat, cos_t, sin_t)


def _nsa_attn_kernel(q_ref, ks_ref, vst_ref, kw_ref, vwt_ref, kc_ref, vct_ref, gate_ref,
                     o_ref, os_scr, score_scr):
    tq, tk, hpg, dk = ATT_TQ, ATT_TK, NSA_HPG, NSA_HEAD_DIM
    ratio = SEL_LEN // CMP_STRIDE
    ratio_shift = ratio.bit_length() - 1
    sel_shift = SEL_LEN.bit_length() - 1
    n_sel = LANES // ratio
    i = pl.program_id(2)
    t0 = i * tq

    def tile_heads(x):
        return jnp.concatenate([x] * hpg, axis=1)

    q_t = q_ref[...].astype(F32).T
    qt = jnp.concatenate([q_t[h * dk:(h + 1) * dk] for h in range(hpg)], axis=1)
    qt = (qt * (dk ** -0.5 * LOG2_E)).astype(BF16)

    def softmax_pv(parts):
        maxes, accs = [], []
        for s, vt in parts:
            m_part = jnp.max(s, axis=0, keepdims=True)
            p = jnp.exp2(s - m_part).astype(BF16)
            vt_ones = jnp.concatenate([vt, jnp.ones((BF16_SUBLANES, vt.shape[1]), BF16)], axis=0)
            maxes.append(m_part)
            accs.append(_dot(vt_ones, p))
        m = functools.reduce(jnp.maximum, maxes)
        acc = sum(jnp.exp2(m_part - m) * a for m_part, a in zip(maxes, accs))
        return acc[:dk] * (1.0 / jnp.maximum(acc[dk:dk + 1], 1e-30))

    n_win = WIN + tq
    k_lo = pl.multiple_of(jnp.maximum(t0 - WIN, 0), tk)
    n_half = n_win // 2
    row_minus_col = (lax.broadcasted_iota(jnp.int32, (n_half, tq), 0)
                     - lax.broadcasted_iota(jnp.int32, (n_half, tq), 1))
    parts_w = []
    for c in range(2):
        k0 = pl.multiple_of(k_lo + c * n_half, tk)
        back = lax.bitcast_convert_type((t0 - k0) - row_minus_col, jnp.uint32)
        bias_w = jnp.where(back < jnp.uint32(WIN), 0.0, NEG_BIG)
        s_w = _dot(kw_ref[0, 0, pl.ds(k0, n_half), :], qt) + tile_heads(bias_w)
        parts_w.append((s_w, vwt_ref[0, 0, :, pl.ds(k0, n_half)]))
    o_w = softmax_pv(parts_w)

    n_cmp = kc_ref.shape[2]
    row_c = lax.broadcasted_iota(jnp.int32, (n_cmp, tq), 0)
    tok_c = t0 + lax.broadcasted_iota(jnp.int32, (n_cmp, tq), 1)
    n_idx = ((row_c & (n_sel - 1)) << ratio_shift) + (row_c >> (n_sel.bit_length() - 1))
    bias_c = jnp.where(n_idx * CMP_STRIDE + (CMP_LEN - 1) <= tok_c, 0.0, NEG_BIG)
    s_c = _dot(kc_ref[0, 0], qt) + tile_heads(bias_c)
    e_c = jnp.exp2(s_c - jnp.max(s_c, axis=0, keepdims=True))
    tok_row = t0 + (lax.broadcasted_iota(jnp.int32, (1, hpg * tq), 1) & (tq - 1))
    inv_l = jnp.where(tok_row >= CMP_LEN - 1,
                      1.0 / jnp.maximum(jnp.sum(e_c, axis=0, keepdims=True), 1e-30), 0.0)
    p_c = e_c * inv_l
    o_c = _dot(vct_ref[0, 0], p_c.astype(BF16))

    p_sum = p_c[:, 0:tq]
    for h in range(1, hpg):
        p_sum = p_sum + p_c[:, h * tq:(h + 1) * tq]
    imp = p_sum[0:n_sel]
    for r in range(1, ratio):
        imp = imp + p_sum[r * n_sel:(r + 1) * n_sel]

    blk = lax.broadcasted_iota(jnp.int32, (n_sel, tq), 0)
    cur = (t0 + lax.broadcasted_iota(jnp.int32, (n_sel, tq), 1)) >> sel_shift
    forced = (blk == 0) | (blk == cur) | (blk == cur - 1)
    score = jnp.where(blk <= cur, jnp.where(forced, jnp.inf, imp), -jnp.inf)
    score_scr[...] = score
    grp_rows = 8
    row_in_grp = lax.broadcasted_iota(jnp.int32, (grp_rows, tq), 0)
    groups = [score[a:a + grp_rows] for a in range(0, n_sel, grp_rows)]
    ranks = [jnp.zeros((grp_rows, tq), F32) for _ in groups]
    for rival in range(n_sel):
        other = score_scr[rival:rival + 1, :]
        for a, mine in enumerate(groups):
            lo = a * grp_rows
            if rival < lo:
                beats = other >= mine
            elif rival >= lo + grp_rows:
                beats = other > mine
            else:
                beats = (other > mine) | ((other == mine) & (row_in_grp + lo > rival))
            ranks[a] = ranks[a] + jnp.where(beats, 1.0, 0.0)
    rank = jnp.concatenate(ranks, axis=0)
    sel_bias = jnp.where(rank < float(N_SELECT), 0.0, NEG_BIG).astype(BF16)

    pad_rows = ks_ref.shape[3] - dk - n_sel
    qt_aug = jnp.concatenate([qt, tile_heads(sel_bias), jnp.zeros((pad_rows, hpg * tq), BF16)], axis=0)
    tok_s = t0 + lax.broadcasted_iota(jnp.int32, (SEL_CHUNK, tq), 1)
    key_s = lax.broadcasted_iota(jnp.int32, (SEL_CHUNK, tq), 0)
    n_variants = ks_ref.shape[2] // SEL_CHUNK
    for v in range(n_variants):
        @pl.when((t0 >> (SEL_CHUNK.bit_length() - 1)) == v)
        def _():
            parts = []
            for c in range(v + 1):
                rows = slice(c * SEL_CHUNK, (c + 1) * SEL_CHUNK)
                s = _dot(ks_ref[0, 0, rows, :], qt_aug)
                if c == v:
                    s = s + tile_heads(jnp.where(c * SEL_CHUNK + key_s <= tok_s, 0.0, NEG_BIG))
                parts.append((s, vst_ref[0, 0, :, rows]))
            os_scr[...] = softmax_pv(parts)
    o_s = os_scr[...]

    row0 = pl.program_id(1) * (3 * hpg)
    gate = lambda c: jnp.concatenate([gate_ref[0, pl.ds(row0 + 3 * h + c, 1), :] for h in range(hpg)], axis=1)
    out_t = gate(0) * o_c + gate(1) * o_s + gate(2) * o_w
    stacked = jnp.concatenate([out_t[:, h * tq:(h + 1) * tq] for h in range(hpg)], axis=0)
    o_ref[...] = stacked.T.astype(o_ref.dtype)


def _nsa_attention(q2d, ks_aug, vst, kw, vwt, kc, vct, gates):
    b, g, s, dk = kw.shape
    hpg = NSA_HPG
    n_cmp = kc.shape[2]
    tq = ATT_TQ
    nt = s // tq
    assert s % SEL_CHUNK == 0 and SEL_CHUNK % tq == 0 and s >= WIN + tq
    k_spec = pl.BlockSpec((1, 1, s, dk), lambda bi, gi, i: (bi, gi, 0, 0))
    vt_spec = pl.BlockSpec((1, 1, dk, s), lambda bi, gi, i: (bi, gi, 0, 0))
    return pl.pallas_call(
        _nsa_attn_kernel,
        grid=(b, g, nt),
        in_specs=[pl.BlockSpec((tq, hpg * dk), lambda bi, gi, i: (bi * nt + i, gi)),
                  pl.BlockSpec((1, 1, s, ks_aug.shape[3]), lambda bi, gi, i: (bi, gi, 0, 0)),
                  vt_spec, k_spec, vt_spec,
                  pl.BlockSpec((1, 1, n_cmp, dk), lambda bi, gi, i: (bi, gi, 0, 0)),
                  pl.BlockSpec((1, 1, dk, n_cmp), lambda bi, gi, i: (bi, gi, 0, 0)),
                  pl.BlockSpec((1, gates.shape[1], tq), lambda bi, gi, i: (bi, 0, i))],
        out_specs=pl.BlockSpec((tq, hpg * dk), lambda bi, gi, i: (bi * nt + i, gi)),
        out_shape=jax.ShapeDtypeStruct((b * s, g * hpg * dk), BF16),
        scratch_shapes=[pltpu.VMEM((dk, hpg * tq), F32), pltpu.VMEM((n_cmp * CMP_STRIDE // SEL_LEN, tq), F32)],
        compiler_params=_cparams("parallel", "parallel", "parallel"),
        name="nsa_attention",
    )(q2d, ks_aug, vst, kw, vwt, kc, vct, gates)


def _nsa_layer(h2d, cos, sin, g_in, w_in, cmp_pe, cmp_w1, cmp_w2, w_out, g_out):
    b, s, _ = cos.shape
    n, d = h2d.shape
    heads, grp, hpg, dk = NSA_HEADS, NSA_KV_GROUPS, NSA_HPG, NSA_HEAD_DIM
    half = dk // 2
    qw = heads * dk
    col = lambda k: w_in[:, qw + k * NSA_KV: qw + (k + 1) * NSA_KV]
    w_gate = jnp.pad(w_in[:, qw + 6 * NSA_KV:], ((0, 0), (0, LANES - 3 * heads)))
    w_all = jnp.concatenate([w_in[:, :qw], col(2), col(4), col(0), col(1), col(3), col(5), w_gate],
                            axis=1).astype(BF16)

    lane_tile = lambda t: jnp.tile(t, (1, 1, LANES // half)).reshape(n, LANES)
    q2d, ks_aug, kw, f, vst, vwt, gates = _nsa_proj(h2d, g_in, w_all, lane_tile(cos), lane_tile(sin), b, s)

    n_grp = s // CMP_STRIDE
    pe_flat = cmp_pe.reshape(2, 1, CMP_LEN * dk)
    rot = lambda w: jnp.concatenate([-w[..., half:], w[..., :half]], axis=-1)
    w2cat = jnp.concatenate([cmp_w2, rot(cmp_w2)], axis=-1).astype(BF16)
    n_cmp = (s - CMP_LEN) // CMP_STRIDE + 1
    pad = ((0, 0), (0, n_grp - n_cmp), (0, 0))
    cos_e = jnp.pad(cos[:, CMP_LEN - 1::CMP_STRIDE], pad, constant_values=1.0)
    sin_e = jnp.pad(sin[:, CMP_LEN - 1::CMP_STRIDE], pad)
    cos_c = jnp.stack([jnp.concatenate([cos_e, cos_e], -1), jnp.ones((b, n_grp, dk), F32)])
    sin_c = jnp.stack([jnp.concatenate([sin_e, sin_e], -1), jnp.zeros((b, n_grp, dk), F32)])
    cmp = _compress(f, pe_flat, cmp_w1.astype(BF16), w2cat, cos_c, sin_c)
    ratio = SEL_LEN // CMP_STRIDE
    cmp = (cmp.reshape(2, b, grp, n_grp // ratio, ratio, dk).transpose(0, 1, 2, 4, 3, 5)
           .reshape(2, b, grp, n_grp, dk))

    o2d = _nsa_attention(q2d, ks_aug, vst, kw, vwt, cmp[0], cmp[1].transpose(0, 1, 3, 2), gates)
    return _proj_norm_residual(o2d, w_out.astype(BF16), g_out, h2d)


def _ret_proj_kernel(x_ref, g_ref, w_ref, cos_ref, sin_ref, qk_ref, vg_ref):
    heads, dk = RET_HEADS, RET_QK_DIM
    half = dk // 2
    a = _rms(x_ref[...], g_ref[...]).astype(BF16)
    cos, sin = cos_ref[...], sin_ref[...]
    for c in range(2 * heads):
        y = _dot(a, w_ref[:, c * dk:(c + 1) * dk])
        y1, y2 = y[:, :half], y[:, half:]
        scale = 1.0 if c < heads else dk ** -0.5
        qk_ref[:, c * dk:c * dk + half] = ((y1 * cos - y2 * sin) * scale).astype(qk_ref.dtype)
        qk_ref[:, c * dk + half:(c + 1) * dk] = ((y2 * cos + y1 * sin) * scale).astype(qk_ref.dtype)
    n_qk = 2 * heads * dk
    for c in range(vg_ref.shape[1] // RET_V_DIM):
        cols = slice(c * RET_V_DIM, (c + 1) * RET_V_DIM)
        vg_ref[:, cols] = _dot(a, w_ref[:, n_qk + c * RET_V_DIM:n_qk + (c + 1) * RET_V_DIM]).astype(vg_ref.dtype)


def _retention_proj(h2d, g_in, w, cos, sin):
    n, d = h2d.shape
    n_qk = 2 * RET_HEADS * RET_QK_DIM
    n_vg = w.shape[1] - n_qk
    tm = ROW_TILE
    return pl.pallas_call(
        _ret_proj_kernel,
        grid=(n // tm,),
        in_specs=[pl.BlockSpec((tm, d), lambda i: (i, 0)),
                  _resident((1, d)),
                  _resident(w.shape),
                  pl.BlockSpec((tm, cos.shape[1]), lambda i: (i, 0)),
                  pl.BlockSpec((tm, sin.shape[1]), lambda i: (i, 0))],
        out_specs=[pl.BlockSpec((tm, n_qk), lambda i: (i, 0)),
                   pl.BlockSpec((tm, n_vg), lambda i: (i, 0))],
        out_shape=[jax.ShapeDtypeStruct((n, n_qk), BF16), jax.ShapeDtypeStruct((n, n_vg), BF16)],
        compiler_params=_cparams("parallel"),
        name="retention_proj",
    )(h2d, g_in.reshape(1, d), w, cos, sin)


def _ret_kernel(qk_ref, vg_ref, gn_ref, decay_ref, xi_ref, zeta_ref, gc_ref, o_ref, state_scr):
    heads, dk, dv = RET_HEADS, RET_QK_DIM, RET_V_DIM

    @pl.when(pl.program_id(1) == 0)
    def _():
        state_scr[...] = jnp.zeros(state_scr.shape, F32)

    for h in range(heads):
        q = qk_ref[:, h * dk:(h + 1) * dk]
        k = qk_ref[:, (heads + h) * dk:(heads + h + 1) * dk]
        v = vg_ref[:, h * dv:(h + 1) * dv]
        s = lax.dot_general(q, k, (((1,), (1,)), ((), ())), preferred_element_type=F32)
        inner = _dot((s * decay_ref[h]).astype(BF16), v)
        state = state_scr[h]
        cross = _dot(q, state.astype(BF16)) * xi_ref[h]
        kz = (k.astype(F32) * zeta_ref[h]).T.astype(BF16)
        state_scr[h] = gc_ref[h] * state + _dot(kz, v)

        o = inner + cross
        mu = jnp.mean(o, axis=-1, keepdims=True)
        oc = o - mu
        var = jnp.mean(oc * oc, axis=-1, keepdims=True)
        on = oc * lax.rsqrt(var + LN_EPS) * gn_ref[:, h * dv:(h + 1) * dv]
        gate = vg_ref[:, (heads + h) * dv:(heads + h + 1) * dv].astype(F32)
        o_ref[:, h * dv:(h + 1) * dv] = (gate * _sigmoid(gate) * on).astype(o_ref.dtype)


def _retention_core(qk, vg, gn_g, b, s):
    heads, dk, dv, c = RET_HEADS, RET_QK_DIM, RET_V_DIM, RET_CHUNK
    n_ch = s // c
    log_g = jnp.log1p(-(2.0 ** (-5.0 - jnp.arange(heads, dtype=F32))))
    ix = jnp.arange(c, dtype=F32)
    rel = ix[:, None] - ix[None, :]
    decay = jnp.where(rel >= 0, jnp.exp(log_g[:, None, None] * jnp.maximum(rel, 0.0)), 0.0)
    xi = jnp.exp(log_g[:, None] * (ix + 1.0))[:, :, None]
    zeta = jnp.exp(log_g[:, None] * (c - 1.0 - ix))[:, :, None]
    g_chunk = jnp.exp(log_g * c)[:, None, None]
    return pl.pallas_call(
        _ret_kernel,
        grid=(b, n_ch),
        in_specs=[pl.BlockSpec((c, qk.shape[1]), lambda bi, ci: (bi * n_ch + ci, 0)),
                  pl.BlockSpec((c, vg.shape[1]), lambda bi, ci: (bi * n_ch + ci, 0)),
                  _resident((1, heads * dv)),
                  _resident(decay.shape), _resident(xi.shape), _resident(zeta.shape),
                  _resident(g_chunk.shape)],
        out_specs=pl.BlockSpec((c, heads * dv), lambda bi, ci: (bi * n_ch + ci, 0)),
        out_shape=jax.ShapeDtypeStruct((b * s, heads * dv), BF16),
        scratch_shapes=[pltpu.VMEM((heads, dk, dv), F32)],
        compiler_params=_cparams("parallel", "arbitrary"),
        name="retention_core",
    )(qk, vg, gn_g.reshape(1, -1), decay, xi, zeta, g_chunk)


def _retention_layer(h2d, cos, sin, g_in, w_in, gn_g, w_out, g_out, b, s):
    qk, vg = _retention_proj(h2d, g_in, w_in.astype(BF16), cos, sin)
    y = _retention_core(qk, vg, gn_g, b, s)
    return _proj_norm_residual(y, w_out.astype(BF16), g_out, h2d)


def kernel(x, positions, norm_g, ffn_w_gate, ffn_w_up, ffn_w_down, gm_w_in, gm_ln_g, gm_ln_b, gm_w_s, gm_b_s, gm_w_out, nsa_w_in, nsa_cmp_pe, nsa_cmp_w1, nsa_cmp_w2, nsa_w_out, ret_w_in, ret_gn_g, ret_w_out):
    b, s, d = x.shape
    h = x.reshape(b * s, d)
    cos_ret, sin_ret = _rope_tables(positions, RET_QK_DIM // 2)
    cos_nsa, sin_nsa = _rope_tables(positions, NSA_HEAD_DIM // 2)
    ffn_wg, ffn_wu, ffn_wd = (w.astype(BF16) for w in (ffn_w_gate, ffn_w_up, ffn_w_down))
    gm_w_in_bf, gm_w_out_bf = gm_w_in.astype(BF16), gm_w_out.astype(BF16)
    for i in range(DEPTH):
        kind, j = i % N_MIXERS, i // N_MIXERS
        if kind == 0:
            h = _gmlp_layer(h, norm_g[i, 0], gm_w_in_bf, gm_ln_g[j], gm_ln_b[j], gm_w_s,
                            gm_b_s[j], gm_w_out_bf, norm_g[i, 1], j)
        elif kind == 1:
            h = _nsa_layer(h, cos_nsa, sin_nsa, norm_g[i, 0], nsa_w_in[j],
                           nsa_cmp_pe[j], nsa_cmp_w1[j], nsa_cmp_w2[j], nsa_w_out[j], norm_g[i, 1])
        else:
            h = _retention_layer(h, cos_ret.reshape(b * s, -1), sin_ret.reshape(b * s, -1),
                                 norm_g[i, 0], ret_w_in[j], ret_gn_g[j], ret_w_out[j], norm_g[i, 1], b, s)
        h = _ffn(h, norm_g[i, 2], ffn_wg, ffn_wu, ffn_wd, norm_g[i, 3], i)
    return h.reshape(b, s, d)
```

```python
import functools
import math

import jax
import jax.numpy as jnp
from jax import lax
from jax.experimental import pallas as pl
from jax.experimental.pallas import tpu as pltpu

F32 = jnp.float32
BF16 = jnp.bfloat16

D_MODEL = 1024
DEPTH = 4
N_MIXERS = 3
RMS_EPS = 1e-6
LN_EPS = 1e-5
ROPE_THETA = 10000.0
D_FF = 2816

GM_CHUNK = 128
GM_WIDTH = 2 * D_MODEL
GM_GROUPS = 8
GM_GROUP_W = GM_WIDTH // GM_GROUPS

NSA_HEADS = 16
NSA_KV_GROUPS = 4
NSA_HPG = NSA_HEADS // NSA_KV_GROUPS
NSA_HEAD_DIM = 64
NSA_KV = NSA_KV_GROUPS * NSA_HEAD_DIM
CMP_LEN = 32
CMP_STRIDE = 16
CMP_HIDDEN = 4 * NSA_HEAD_DIM
SEL_LEN = 64
N_SELECT = 8
WIN = 512

RET_HEADS = 4
RET_QK_DIM = 256
RET_V_DIM = 512
RET_CHUNK = 128
RET_BATCH = 2

LANES = 128
BF16_SUBLANES = 16
LOG2_E = math.log2(math.e)
VMEM_LIMIT = 48 * 1024 * 1024
NEG_BIG = -1e30

ROW_TILE = 512
GM_ROW_TILE = 256
FFN_COL_CHUNK = 256
ATT_TQ = 256
ATT_TK = 128
SEL_CHUNK = 512


def _cparams(*sem):
    return pltpu.CompilerParams(dimension_semantics=sem, vmem_limit_bytes=VMEM_LIMIT)


def _resident(shape):
    nd = len(shape)
    return pl.BlockSpec(shape, lambda *_: (0,) * nd, pipeline_mode=pl.Buffered(1))


def _resident_layer(stacked_shape, layer):
    nd = len(stacked_shape)
    return pl.BlockSpec((None,) + tuple(stacked_shape[1:]), lambda *_: (layer,) + (0,) * (nd - 1),
                        pipeline_mode=pl.Buffered(1))


def _rms(x32, g):
    ms = jnp.mean(x32 * x32, axis=-1, keepdims=True)
    return x32 * lax.rsqrt(ms + RMS_EPS) * g


def _gelu_tanh(x):
    c = math.sqrt(2.0 / math.pi)
    return x * (0.5 * (1.0 + jnp.tanh(c * (x + 0.044715 * (x * x * x)))))


def _sigmoid(x):
    return 1.0 / (1.0 + jnp.exp(-x))


def _dot(a, b):
    return jnp.dot(a, b, preferred_element_type=F32)


def _out_kernel(m_ref, w_ref, g_ref, h_ref, o_ref):
    y = _dot(m_ref[...], w_ref[...])
    o_ref[...] = h_ref[...] + _rms(y, g_ref[...])


def _proj_norm_residual(m2d, w, g, h2d):
    n, k = m2d.shape
    d = w.shape[1]
    tm = ROW_TILE
    return pl.pallas_call(
        _out_kernel,
        grid=(n // tm,),
        in_specs=[pl.BlockSpec((tm, k), lambda i: (i, 0)),
                  _resident((k, d)),
                  _resident((1, d)),
                  pl.BlockSpec((tm, d), lambda i: (i, 0))],
        out_specs=pl.BlockSpec((tm, d), lambda i: (i, 0)),
        out_shape=jax.ShapeDtypeStruct((n, d), F32),
        compiler_params=_cparams("parallel"),
        name="proj_norm_residual",
    )(m2d, w, g.reshape(1, d), h2d)


def _ffn_kernel(h_ref, g_in_ref, wg_ref, wu_ref, wd_ref, g_out_ref, o_ref, acc_ref):
    x = h_ref[...]
    a = _rms(x, g_in_ref[...]).astype(BF16)
    for c in range(D_FF // FFN_COL_CHUNK):
        cols = slice(c * FFN_COL_CHUNK, (c + 1) * FFN_COL_CHUNK)
        gate = _dot(a, wg_ref[:, cols])
        up = _dot(a, wu_ref[:, cols])
        act = (gate * _sigmoid(gate) * up).astype(BF16)
        part = _dot(act, wd_ref[cols, :])
        if c == 0:
            acc_ref[...] = part
        else:
            acc_ref[...] += part
    o_ref[...] = x + _rms(acc_ref[...], g_out_ref[...])


def _ffn(h2d, g_in, wg, wu, wd, g_out, layer):
    n, d = h2d.shape
    tm = ROW_TILE
    return pl.pallas_call(
        _ffn_kernel,
        grid=(n // tm,),
        in_specs=[pl.BlockSpec((tm, d), lambda i: (i, 0)),
                  _resident((1, d)),
                  _resident_layer(wg.shape, layer),
                  _resident_layer(wu.shape, layer),
                  _resident_layer(wd.shape, layer),
                  _resident((1, d))],
        out_specs=pl.BlockSpec((tm, d), lambda i: (i, 0)),
        out_shape=jax.ShapeDtypeStruct((n, d), F32),
        scratch_shapes=[pltpu.VMEM((tm, d), F32)],
        compiler_params=_cparams("parallel"),
        name="swiglu_ffn",
    )(h2d, g_in.reshape(1, d), wg, wu, wd, g_out.reshape(1, d))


def _gmlp_kernel(h_ref, g_in_ref, win_ref, lng_ref, lnb_ref, ws_ref, bst_ref, wout_ref,
                 g_out_ref, o_ref, vn_scr, y_scr):
    x = h_ref[...]
    tm = x.shape[0]
    a = _rms(x, g_in_ref[...]).astype(BF16)

    v = _gelu_tanh(_dot(a, win_ref[:, GM_WIDTH:]))
    mu = jnp.mean(v, axis=-1, keepdims=True)
    vc = v - mu
    var = jnp.mean(vc * vc, axis=-1, keepdims=True)
    vn_scr[...] = (vc * lax.rsqrt(var + LN_EPS) * lng_ref[...] + lnb_ref[...]).astype(BF16)

    t_out = lax.broadcasted_iota(jnp.int32, (GM_CHUNK, GM_CHUNK), 0)
    t_in = lax.broadcasted_iota(jnp.int32, (GM_CHUNK, GM_CHUNK), 1)
    causal = t_in <= t_out
    bst = bst_ref[...]
    for grp in range(GM_GROUPS):
        cols = slice(grp * GM_GROUP_W, (grp + 1) * GM_GROUP_W)
        w_mix = jnp.where(causal, ws_ref[grp], 0.0).astype(BF16)
        bias = bst[:, grp:grp + 1]
        for c in range(tm // GM_CHUNK):
            rows = slice(c * GM_CHUNK, (c + 1) * GM_CHUNK)
            mixed = _dot(w_mix, vn_scr[rows, cols]) + bias
            u = _gelu_tanh(_dot(a[rows], win_ref[:, cols]))
            y_scr[rows, cols] = (u * mixed).astype(BF16)

    m = _dot(y_scr[...], wout_ref[...])
    o_ref[...] = x + _rms(m, g_out_ref[...])


def _gmlp_layer(h2d, g_in, w_in, ln_g, ln_b, w_s, b_s, w_out, g_out, layer):
    n, d = h2d.shape
    tm = GM_ROW_TILE
    return pl.pallas_call(
        _gmlp_kernel,
        grid=(n // tm,),
        in_specs=[pl.BlockSpec((tm, d), lambda i: (i, 0)),
                  _resident((1, d)),
                  _resident_layer(w_in.shape, layer),
                  _resident((1, GM_WIDTH)),
                  _resident((1, GM_WIDTH)),
                  _resident_layer(w_s.shape, layer),
                  _resident((GM_CHUNK, GM_GROUPS)),
                  _resident_layer(w_out.shape, layer),
                  _resident((1, d))],
        out_specs=pl.BlockSpec((tm, d), lambda i: (i, 0)),
        out_shape=jax.ShapeDtypeStruct((n, d), F32),
        scratch_shapes=[pltpu.VMEM((tm, GM_WIDTH), BF16), pltpu.VMEM((tm, GM_WIDTH), BF16)],
        compiler_params=_cparams("parallel"),
        name="gmlp_layer",
    )(h2d, g_in.reshape(1, d), w_in, ln_g.reshape(1, -1), ln_b.reshape(1, -1),
      w_s, b_s.T, w_out, g_out.reshape(1, d))


def _rope_tables(positions, half):
    inv = ROPE_THETA ** (-jnp.arange(half, dtype=F32) / half)
    ang = positions.astype(F32)[..., None] * inv
    return jnp.cos(ang), jnp.sin(ang)


def _nsa_proj_kernel(x_ref, g_ref, w_ref, cos_ref, sin_ref,
                     q_ref, ks_ref, kw_ref, kvc_ref, vst_ref, vwt_ref, gate_ref, y_scr, *, tiles_per_seq):
    grp, dk, kv = NSA_KV_GROUPS, NSA_HEAD_DIM, NSA_KV
    qw = NSA_HEADS * dk
    tm = x_ref.shape[0]
    a = _rms(x_ref[...], g_ref[...]).astype(BF16)
    lane = lax.broadcasted_iota(jnp.int32, (tm, LANES), 1)
    first_half = (lane & (dk - 1)) < dk // 2
    cos = cos_ref[...]
    sin_lo = jnp.where(first_half, -sin_ref[...], 0.0)
    sin_hi = jnp.where(first_half, 0.0, sin_ref[...])

    def rope_chunks(y):
        out = []
        for c in range(y.shape[1] // LANES):
            yc = y[:, c * LANES:(c + 1) * LANES]
            out.append(yc * cos + pltpu.roll(yc, LANES - dk // 2, 1) * sin_lo
                       + pltpu.roll(yc, dk // 2, 1) * sin_hi)
        return out

    def proj(k):
        return _dot(a, w_ref[:, qw + k * kv:qw + (k + 1) * kv])

    for c in range(qw // 512):
        for cc, chunk in enumerate(rope_chunks(_dot(a, w_ref[:, c * 512:(c + 1) * 512]))):
            q_ref[:, c * 512 + cc * LANES:c * 512 + (cc + 1) * LANES] = chunk.astype(q_ref.dtype)

    tok = (pl.program_id(0) % tiles_per_seq) * tm + lax.broadcasted_iota(jnp.int32, (tm, LANES), 0)
    onehot = jnp.where((tok >> (SEL_LEN.bit_length() - 1)) == lane - dk, 1.0, 0.0)
    ks_chunks = rope_chunks(proj(0))
    for gi in range(grp):
        chunk = ks_chunks[gi * dk // LANES]
        if (gi * dk) % LANES:
            chunk = pltpu.roll(chunk, LANES - (gi * dk) % LANES, 1)
        ks_ref[0, gi] = jnp.where(lane < dk, chunk, onehot).astype(ks_ref.dtype)

    def store_groups(ref_at, y):
        for gi in range(grp):
            ref_at(gi)[...] = y[:, gi * dk:(gi + 1) * dk].astype(kw_ref.dtype)

    store_groups(lambda gi: kw_ref.at[0, gi], jnp.concatenate(rope_chunks(proj(1)), axis=1))
    n_rows = tm // CMP_STRIDE
    for which in range(2):
        y = proj(2 + which)
        for c in range(kv // LANES):
            y_scr[c] = y[:, c * LANES:(c + 1) * LANES]
        for l in range(CMP_STRIDE):
            for c in range(kv // LANES):
                rows = y_scr[c, pl.ds(l, n_rows, stride=CMP_STRIDE), :]
                for sub in range(LANES // dk):
                    kvc_ref[which, 0, c * (LANES // dk) + sub, :, l * dk:(l + 1) * dk] = (
                        rows[:, sub * dk:(sub + 1) * dk].astype(kvc_ref.dtype))
    for ref, k in ((vst_ref, 4), (vwt_ref, 5)):
        y_t = proj(k).T
        for gi in range(grp):
            ref[0, gi] = y_t[gi * dk:(gi + 1) * dk].astype(ref.dtype)
    gates_t = _sigmoid(_dot(a, w_ref[:, qw + 6 * kv:])).T
    gate_ref[0] = gates_t[:gate_ref.shape[1]]


def _nsa_proj(h2d, g_in, w, cos_t, sin_t, b, s):
    n, d = h2d.shape
    grp, dk = NSA_KV_GROUPS, NSA_HEAD_DIM
    tm = ROW_TILE
    per_b = s // tm
    row = lambda i: (i, 0)
    tok_blk = lambda i: (i // per_b, 0, i % per_b, 0)
    tr_blk = lambda i: (i // per_b, 0, 0, i % per_b)
    return pl.pallas_call(
        functools.partial(_nsa_proj_kernel, tiles_per_seq=per_b),
        grid=(n // tm,),
        in_specs=[pl.BlockSpec((tm, d), row), _resident((1, d)), _resident(w.shape),
                  pl.BlockSpec((tm, LANES), row), pl.BlockSpec((tm, LANES), row)],
        out_specs=[pl.BlockSpec((tm, NSA_HEADS * dk), row),
                   pl.BlockSpec((1, grp, tm, LANES), tok_blk),
                   pl.BlockSpec((1, grp, tm, dk), tok_blk),
                   pl.BlockSpec((2, 1, grp, tm // CMP_STRIDE, CMP_STRIDE * dk),
                                lambda i: (0, i // per_b, 0, i % per_b, 0)),
                   pl.BlockSpec((1, grp, dk, tm), tr_blk),
                   pl.BlockSpec((1, grp, dk, tm), tr_blk),
                   pl.BlockSpec((1, 3 * NSA_HEADS, tm), lambda i: (i // per_b, 0, i % per_b))],
        out_shape=[jax.ShapeDtypeStruct((n, NSA_HEADS * dk), BF16),
                   jax.ShapeDtypeStruct((b, grp, s, LANES), BF16),
                   jax.ShapeDtypeStruct((b, grp, s, dk), BF16),
                   jax.ShapeDtypeStruct((2, b, grp, s // CMP_STRIDE, CMP_STRIDE * dk), BF16),
                   jax.ShapeDtypeStruct((b, grp, dk, s), BF16),
                   jax.ShapeDtypeStruct((b, grp, dk, s), BF16),
                   jax.ShapeDtypeStruct((b, 3 * NSA_HEADS, s), F32)],
        scratch_shapes=[pltpu.VMEM((NSA_KV // LANES, tm, LANES), F32)],
        compiler_params=_cparams("parallel"),
        name="nsa_proj",
    )(h2d, g_in.reshape(1, d), w, cos_t, sin_t)


def _cmp_kernel(f_ref, pe_ref, w1_ref, w2_ref, cos_ref, sin_ref, o_ref):
    half = CMP_LEN * NSA_HEAD_DIM // 2
    x = f_ref[0, 0, 0].astype(F32)
    pe = pe_ref[0]
    first = _dot((x + pe[:, :half]).astype(BF16), w1_ref[0, :half, :])
    second = _dot((x + pe[:, half:]).astype(BF16), w1_ref[0, half:, :])
    hid = _gelu_tanh(first + pltpu.roll(second, second.shape[0] - 1, 0))
    y = _dot(hid.astype(BF16), w2_ref[0])
    out = y[:, :NSA_HEAD_DIM] * cos_ref[0, 0] + y[:, NSA_HEAD_DIM:] * sin_ref[0, 0]
    o_ref[0, 0, 0] = out.astype(o_ref.dtype)


def _compress(f, pe_flat, w1, w2cat, cos_t, sin_t):
    _, b, g, n_grp, width = f.shape
    dk = NSA_HEAD_DIM
    return pl.pallas_call(
        _cmp_kernel,
        grid=(2, b, g),
        in_specs=[pl.BlockSpec((1, 1, 1, n_grp, width), lambda s, i, j: (s, i, j, 0, 0)),
                  pl.BlockSpec((1, 1, pe_flat.shape[-1]), lambda s, i, j: (s, 0, 0)),
                  pl.BlockSpec((1,) + w1.shape[1:], lambda s, i, j: (s, 0, 0)),
                  pl.BlockSpec((1,) + w2cat.shape[1:], lambda s, i, j: (s, 0, 0)),
                  pl.BlockSpec((1, 1, n_grp, dk), lambda s, i, j: (s, i, 0, 0)),
                  pl.BlockSpec((1, 1, n_grp, dk), lambda s, i, j: (s, i, 0, 0))],
        out_specs=pl.BlockSpec((1, 1, 1, n_grp, dk), lambda s, i, j: (s, i, j, 0, 0)),
        out_shape=jax.ShapeDtypeStruct((2, b, g, n_grp, dk), BF16),
        compiler_params=_cparams("parallel", "parallel", "parallel"),
        name="nsa_compress",
    )(f, pe_flat, w1, w2cat, cos_t, sin_t)


def _nsa_attn_kernel(q_ref, ks_ref, vst_ref, kw_ref, vwt_ref, kc_ref, vct_ref, gate_ref,
                     o_ref, os_scr, score_scr):
    tq, tk, hpg, dk = ATT_TQ, ATT_TK, NSA_HPG, NSA_HEAD_DIM
    ratio = SEL_LEN // CMP_STRIDE
    ratio_shift = ratio.bit_length() - 1
    sel_shift = SEL_LEN.bit_length() - 1
    n_sel = LANES // ratio
    i = pl.program_id(2)
    t0 = i * tq

    def tile_heads(x):
        return jnp.concatenate([x] * hpg, axis=1)

    q_t = q_ref[...].astype(F32).T
    qt = jnp.concatenate([q_t[h * dk:(h + 1) * dk] for h in range(hpg)], axis=1)
    qt = (qt * (dk ** -0.5 * LOG2_E)).astype(BF16)

    def softmax_pv(parts):
        maxes, accs = [], []
        for s, vt in parts:
            m_part = jnp.max(s, axis=0, keepdims=True)
            p = jnp.exp2(s - m_part).astype(BF16)
            vt_ones = jnp.concatenate([vt, jnp.ones((BF16_SUBLANES, vt.shape[1]), BF16)], axis=0)
            maxes.append(m_part)
            accs.append(_dot(vt_ones, p))
        m = functools.reduce(jnp.maximum, maxes)
        acc = sum(jnp.exp2(m_part - m) * a for m_part, a in zip(maxes, accs))
        return acc[:dk] * (1.0 / jnp.maximum(acc[dk:dk + 1], 1e-30))

    n_win = WIN + tq
    k_lo = pl.multiple_of(jnp.maximum(t0 - WIN, 0), tk)
    n_half = n_win // 2
    row_minus_col = (lax.broadcasted_iota(jnp.int32, (n_half, tq), 0)
                     - lax.broadcasted_iota(jnp.int32, (n_half, tq), 1))
    parts_w = []
    for c in range(2):
        k0 = pl.multiple_of(k_lo + c * n_half, tk)
        back = lax.bitcast_convert_type((t0 - k0) - row_minus_col, jnp.uint32)
        bias_w = jnp.where(back < jnp.uint32(WIN), 0.0, NEG_BIG)
        s_w = _dot(kw_ref[0, 0, pl.ds(k0, n_half), :], qt) + tile_heads(bias_w)
        parts_w.append((s_w, vwt_ref[0, 0, :, pl.ds(k0, n_half)]))
    o_w = softmax_pv(parts_w)

    n_cmp = kc_ref.shape[2]
    row_c = lax.broadcasted_iota(jnp.int32, (n_cmp, tq), 0)
    tok_c = t0 + lax.broadcasted_iota(jnp.int32, (n_cmp, tq), 1)
    n_idx = ((row_c & (n_sel - 1)) << ratio_shift) + (row_c >> (n_sel.bit_length() - 1))
    bias_c = jnp.where(n_idx * CMP_STRIDE + (CMP_LEN - 1) <= tok_c, 0.0, NEG_BIG)
    s_c = _dot(kc_ref[0, 0], qt) + tile_heads(bias_c)
    e_c = jnp.exp2(s_c - jnp.max(s_c, axis=0, keepdims=True))
    tok_row = t0 + (lax.broadcasted_iota(jnp.int32, (1, hpg * tq), 1) & (tq - 1))
    inv_l = jnp.where(tok_row >= CMP_LEN - 1,
                      1.0 / jnp.maximum(jnp.sum(e_c, axis=0, keepdims=True), 1e-30), 0.0)
    p_c = e_c * inv_l
    o_c = _dot(vct_ref[0, 0], p_c.astype(BF16))

    p_sum = p_c[:, 0:tq]
    for h in range(1, hpg):
        p_sum = p_sum + p_c[:, h * tq:(h + 1) * tq]
    imp = p_sum[0:n_sel]
    for r in range(1, ratio):
        imp = imp + p_sum[r * n_sel:(r + 1) * n_sel]

    blk = lax.broadcasted_iota(jnp.int32, (n_sel, tq), 0)
    cur = (t0 + lax.broadcasted_iota(jnp.int32, (n_sel, tq), 1)) >> sel_shift
    forced = (blk == 0) | (blk == cur) | (blk == cur - 1)
    score = jnp.where(blk <= cur, jnp.where(forced, jnp.inf, imp), -jnp.inf)
    score_scr[...] = score
    grp_rows = 8
    row_in_grp = lax.broadcasted_iota(jnp.int32, (grp_rows, tq), 0)
    groups = [score[a:a + grp_rows] for a in range(0, n_sel, grp_rows)]
    ranks = [jnp.zeros((grp_rows, tq), F32) for _ in groups]
    for rival in range(n_sel):
        other = score_scr[rival:rival + 1, :]
        for a, mine in enumerate(groups):
            lo = a * grp_rows
            if rival < lo:
                beats = other >= mine
            elif rival >= lo + grp_rows:
                beats = other > mine
            else:
                beats = (other > mine) | ((other == mine) & (row_in_grp + lo > rival))
            ranks[a] = ranks[a] + jnp.where(beats, 1.0, 0.0)
    rank = jnp.concatenate(ranks, axis=0)
    sel_bias = jnp.where(rank < float(N_SELECT), 0.0, NEG_BIG).astype(BF16)

    pad_rows = ks_ref.shape[3] - dk - n_sel
    qt_aug = jnp.concatenate([qt, tile_heads(sel_bias), jnp.zeros((pad_rows, hpg * tq), BF16)], axis=0)
    tok_s = t0 + lax.broadcasted_iota(jnp.int32, (SEL_CHUNK, tq), 1)
    key_s = lax.broadcasted_iota(jnp.int32, (SEL_CHUNK, tq), 0)
    n_variants = ks_ref.shape[2] // SEL_CHUNK
    for v in range(n_variants):
        @pl.when((t0 >> (SEL_CHUNK.bit_length() - 1)) == v)
        def _():
            parts = []
            for c in range(v + 1):
                rows = slice(c * SEL_CHUNK, (c + 1) * SEL_CHUNK)
                s = _dot(ks_ref[0, 0, rows, :], qt_aug)
                if c == v:
                    s = s + tile_heads(jnp.where(c * SEL_CHUNK + key_s <= tok_s, 0.0, NEG_BIG))
                parts.append((s, vst_ref[0, 0, :, rows]))
            os_scr[...] = softmax_pv(parts)
    o_s = os_scr[...]

    row0 = pl.program_id(1) * (3 * hpg)
    gate = lambda c: jnp.concatenate([gate_ref[0, pl.ds(row0 + 3 * h + c, 1), :] for h in range(hpg)], axis=1)
    out_t = gate(0) * o_c + gate(1) * o_s + gate(2) * o_w
    stacked = jnp.concatenate([out_t[:, h * tq:(h + 1) * tq] for h in range(hpg)], axis=0)
    o_ref[...] = stacked.T.astype(o_ref.dtype)


def _nsa_attention(q2d, ks_aug, vst, kw, vwt, kc, vct, gates):
    b, g, s, dk = kw.shape
    hpg = NSA_HPG
    n_cmp = kc.shape[2]
    tq = ATT_TQ
    nt = s // tq
    assert s % SEL_CHUNK == 0 and SEL_CHUNK % tq == 0 and s >= WIN + tq
    k_spec = pl.BlockSpec((1, 1, s, dk), lambda bi, gi, i: (bi, gi, 0, 0))
    vt_spec = pl.BlockSpec((1, 1, dk, s), lambda bi, gi, i: (bi, gi, 0, 0))
    return pl.pallas_call(
        _nsa_attn_kernel,
        grid=(b, g, nt),
        in_specs=[pl.BlockSpec((tq, hpg * dk), lambda bi, gi, i: (bi * nt + i, gi)),
                  pl.BlockSpec((1, 1, s, ks_aug.shape[3]), lambda bi, gi, i: (bi, gi, 0, 0)),
                  vt_spec, k_spec, vt_spec,
                  pl.BlockSpec((1, 1, n_cmp, dk), lambda bi, gi, i: (bi, gi, 0, 0)),
                  pl.BlockSpec((1, 1, dk, n_cmp), lambda bi, gi, i: (bi, gi, 0, 0)),
                  pl.BlockSpec((1, gates.shape[1], tq), lambda bi, gi, i: (bi, 0, i))],
        out_specs=pl.BlockSpec((tq, hpg * dk), lambda bi, gi, i: (bi * nt + i, gi)),
        out_shape=jax.ShapeDtypeStruct((b * s, g * hpg * dk), BF16),
        scratch_shapes=[pltpu.VMEM((dk, hpg * tq), F32), pltpu.VMEM((n_cmp * CMP_STRIDE // SEL_LEN, tq), F32)],
        compiler_params=_cparams("parallel", "parallel", "parallel"),
        name="nsa_attention",
    )(q2d, ks_aug, vst, kw, vwt, kc, vct, gates)


def _nsa_layer(h2d, cos, sin, g_in, w_in, cmp_pe, cmp_w1, cmp_w2, w_out, g_out):
    b, s, _ = cos.shape
    n, d = h2d.shape
    heads, grp, hpg, dk = NSA_HEADS, NSA_KV_GROUPS, NSA_HPG, NSA_HEAD_DIM
    half = dk // 2
    qw = heads * dk
    col = lambda k: w_in[:, qw + k * NSA_KV: qw + (k + 1) * NSA_KV]
    w_gate = jnp.pad(w_in[:, qw + 6 * NSA_KV:], ((0, 0), (0, LANES - 3 * heads)))
    w_all = jnp.concatenate([w_in[:, :qw], col(2), col(4), col(0), col(1), col(3), col(5), w_gate],
                            axis=1).astype(BF16)

    lane_tile = lambda t: jnp.tile(t, (1, 1, LANES // half)).reshape(n, LANES)
    q2d, ks_aug, kw, f, vst, vwt, gates = _nsa_proj(h2d, g_in, w_all, lane_tile(cos), lane_tile(sin), b, s)

    n_grp = s // CMP_STRIDE
    pe_flat = cmp_pe.reshape(2, 1, CMP_LEN * dk)
    rot = lambda w: jnp.concatenate([-w[..., half:], w[..., :half]], axis=-1)
    w2cat = jnp.concatenate([cmp_w2, rot(cmp_w2)], axis=-1).astype(BF16)
    n_cmp = (s - CMP_LEN) // CMP_STRIDE + 1
    pad = ((0, 0), (0, n_grp - n_cmp), (0, 0))
    cos_e = jnp.pad(cos[:, CMP_LEN - 1::CMP_STRIDE], pad, constant_values=1.0)
    sin_e = jnp.pad(sin[:, CMP_LEN - 1::CMP_STRIDE], pad)
    cos_c = jnp.stack([jnp.concatenate([cos_e, cos_e], -1), jnp.ones((b, n_grp, dk), F32)])
    sin_c = jnp.stack([jnp.concatenate([sin_e, sin_e], -1), jnp.zeros((b, n_grp, dk), F32)])
    cmp = _compress(f, pe_flat, cmp_w1.astype(BF16), w2cat, cos_c, sin_c)
    ratio = SEL_LEN // CMP_STRIDE
    cmp = (cmp.reshape(2, b, grp, n_grp // ratio, ratio, dk).transpose(0, 1, 2, 4, 3, 5)
           .reshape(2, b, grp, n_grp, dk))

    o2d = _nsa_attention(q2d, ks_aug, vst, kw, vwt, cmp[0], cmp[1].transpose(0, 1, 3, 2), gates)
    return _proj_norm_residual(o2d, w_out.astype(BF16), g_out, h2d)


def _ret_proj_kernel(x_ref, g_ref, w_ref, cos_ref, sin_ref, qk_ref, vg_ref):
    heads, dk = RET_HEADS, RET_QK_DIM
    half = dk // 2
    a = _rms(x_ref[...], g_ref[...]).astype(BF16)
    cos, sin = cos_ref[...], sin_ref[...]
    for c in range(2 * heads):
        y = _dot(a, w_ref[:, c * dk:(c + 1) * dk])
        y1, y2 = y[:, :half], y[:, half:]
        scale = 1.0 if c < heads else dk ** -0.5
        qk_ref[:, c * dk:c * dk + half] = ((y1 * cos - y2 * sin) * scale).astype(qk_ref.dtype)
        qk_ref[:, c * dk + half:(c + 1) * dk] = ((y2 * cos + y1 * sin) * scale).astype(qk_ref.dtype)
    n_qk = 2 * heads * dk
    for c in range(vg_ref.shape[1] // RET_V_DIM):
        cols = slice(c * RET_V_DIM, (c + 1) * RET_V_DIM)
        vg_ref[:, cols] = _dot(a, w_ref[:, n_qk + c * RET_V_DIM:n_qk + (c + 1) * RET_V_DIM]).astype(vg_ref.dtype)


def _retention_proj(h2d, g_in, w, cos, sin):
    n, d = h2d.shape
    n_qk = 2 * RET_HEADS * RET_QK_DIM
    n_vg = w.shape[1] - n_qk
    tm = ROW_TILE
    return pl.pallas_call(
        _ret_proj_kernel,
        grid=(n // tm,),
        in_specs=[pl.BlockSpec((tm, d), lambda i: (i, 0)),
                  _resident((1, d)),
                  _resident(w.shape),
                  pl.BlockSpec((tm, cos.shape[1]), lambda i: (i, 0)),
                  pl.BlockSpec((tm, sin.shape[1]), lambda i: (i, 0))],
        out_specs=[pl.BlockSpec((tm, n_qk), lambda i: (i, 0)),
                   pl.BlockSpec((tm, n_vg), lambda i: (i, 0))],
        out_shape=[jax.ShapeDtypeStruct((n, n_qk), BF16), jax.ShapeDtypeStruct((n, n_vg), BF16)],
        compiler_params=_cparams("parallel"),
        name="retention_proj",
    )(h2d, g_in.reshape(1, d), w, cos, sin)


def _ret_kernel(qk_ref, vg_ref, gn_ref, decay_ref, xi_ref, zeta_ref, gc_ref, wout_ref, g_out_ref, h_ref,
                o_ref, state_scr, y_scr):
    heads, dk, dv, c = RET_HEADS, RET_QK_DIM, RET_V_DIM, RET_CHUNK

    @pl.when(pl.program_id(1) == 0)
    def _():
        state_scr[...] = jnp.zeros(state_scr.shape, F32)

    for bb in range(RET_BATCH):
        for h in range(heads):
            q = qk_ref[bb, :, h * dk:(h + 1) * dk]
            k = qk_ref[bb, :, (heads + h) * dk:(heads + h + 1) * dk]
            v = vg_ref[bb, :, h * dv:(h + 1) * dv]
            s = lax.dot_general(q, k, (((1,), (1,)), ((), ())), preferred_element_type=F32)
            inner = _dot((s * decay_ref[h]).astype(BF16), v)
            state = state_scr[bb * heads + h]
            cross = _dot(q, state.astype(BF16)) * xi_ref[h]
            kz = (k.astype(F32) * zeta_ref[h]).T.astype(BF16)
            state_scr[bb * heads + h] = gc_ref[h] * state + _dot(kz, v)

            o = inner + cross
            mu = jnp.mean(o, axis=-1, keepdims=True)
            oc = o - mu
            var = jnp.mean(oc * oc, axis=-1, keepdims=True)
            on = oc * lax.rsqrt(var + LN_EPS) * gn_ref[:, h * dv:(h + 1) * dv]
            gate = vg_ref[bb, :, (heads + h) * dv:(heads + h + 1) * dv].astype(F32)
            y_scr[bb * c:(bb + 1) * c, h * dv:(h + 1) * dv] = (gate * _sigmoid(gate) * on).astype(y_scr.dtype)

    m = _dot(y_scr[...], wout_ref[...])
    for bb in range(RET_BATCH):
        o_ref[bb] = h_ref[bb] + _rms(m[bb * c:(bb + 1) * c], g_out_ref[...])


def _retention_core(qk, vg, gn_g, w_out, g_out, h2d, b, s):
    heads, dk, dv, c = RET_HEADS, RET_QK_DIM, RET_V_DIM, RET_CHUNK
    n_ch = s // c
    d = h2d.shape[1]
    bb = RET_BATCH
    assert b % bb == 0
    log_g = jnp.log1p(-(2.0 ** (-5.0 - jnp.arange(heads, dtype=F32))))
    ix = jnp.arange(c, dtype=F32)
    rel = ix[:, None] - ix[None, :]
    decay = jnp.where(rel >= 0, jnp.exp(log_g[:, None, None] * jnp.maximum(rel, 0.0)), 0.0)
    xi = jnp.exp(log_g[:, None] * (ix + 1.0))[:, :, None]
    zeta = jnp.exp(log_g[:, None] * (c - 1.0 - ix))[:, :, None]
    g_chunk = jnp.exp(log_g * c)[:, None, None]
    seq_blk = lambda width: pl.BlockSpec((bb, c, width), lambda bi, ci: (bi, ci, 0))
    out = pl.pallas_call(
        _ret_kernel,
        grid=(b // bb, n_ch),
        in_specs=[seq_blk(qk.shape[1]), seq_blk(vg.shape[1]),
                  _resident((1, heads * dv)),
                  _resident(decay.shape), _resident(xi.shape), _resident(zeta.shape),
                  _resident(g_chunk.shape), _resident(w_out.shape), _resident((1, d)),
                  seq_blk(d)],
        out_specs=seq_blk(d),
        out_shape=jax.ShapeDtypeStruct((b, s, d), F32),
        scratch_shapes=[pltpu.VMEM((bb * heads, dk, dv), F32), pltpu.VMEM((bb * c, heads * dv), BF16)],
        compiler_params=_cparams("parallel", "arbitrary"),
        name="retention_core",
    )(qk.reshape(b, s, -1), vg.reshape(b, s, -1), gn_g.reshape(1, -1), decay, xi, zeta, g_chunk,
      w_out, g_out.reshape(1, d), h2d.reshape(b, s, d))
    return out.reshape(b * s, d)


def _retention_layer(h2d, cos, sin, g_in, w_in, gn_g, w_out, g_out, b, s):
    qk, vg = _retention_proj(h2d, g_in, w_in.astype(BF16), cos, sin)
    return _retention_core(qk, vg, gn_g, w_out.astype(BF16), g_out, h2d, b, s)


def kernel(x, positions, norm_g, ffn_w_gate, ffn_w_up, ffn_w_down, gm_w_in, gm_ln_g, gm_ln_b, gm_w_s, gm_b_s, gm_w_out, nsa_w_in, nsa_cmp_pe, nsa_cmp_w1, nsa_cmp_w2, nsa_w_out, ret_w_in, ret_gn_g, ret_w_out):
    b, s, d = x.shape
    h = x.reshape(b * s, d)
    cos_ret, sin_ret = _rope_tables(positions, RET_QK_DIM // 2)
    cos_nsa, sin_nsa = _rope_tables(positions, NSA_HEAD_DIM // 2)
    ffn_wg, ffn_wu, ffn_wd = (w.astype(BF16) for w in (ffn_w_gate, ffn_w_up, ffn_w_down))
    gm_w_in_bf, gm_w_out_bf = gm_w_in.astype(BF16), gm_w_out.astype(BF16)
    for i in range(DEPTH):
        kind, j = i % N_MIXERS, i // N_MIXERS
        if kind == 0:
            h = _gmlp_layer(h, norm_g[i, 0], gm_w_in_bf, gm_ln_g[j], gm_ln_b[j], gm_w_s,
                            gm_b_s[j], gm_w_out_bf, norm_g[i, 1], j)
        elif kind == 1:
            h = _nsa_layer(h, cos_nsa, sin_nsa, norm_g[i, 0], nsa_w_in[j],
                           nsa_cmp_pe[j], nsa_cmp_w1[j], nsa_cmp_w2[j], nsa_w_out[j], norm_g[i, 1])
        else:
            h = _retention_layer(h, cos_ret.reshape(b * s, -1), sin_ret.reshape(b * s, -1),
                                 norm_g[i, 0], ret_w_in[j], ret_gn_g[j], ret_w_out[j], norm_g[i, 1], b, s)
        h = _ffn(h, norm_g[i, 2], ffn_wg, ffn_wu, ffn_wd, norm_g[i, 3], i)
    return h.reshape(b, s, d)
```

```python
import functools
import math

import jax
import jax.numpy as jnp
from jax import lax
from jax.experimental import pallas as pl
from jax.experimental.pallas import tpu as pltpu

F32 = jnp.float32
BF16 = jnp.bfloat16

D_MODEL = 1024
DEPTH = 4
N_MIXERS = 3
RMS_EPS = 1e-6
LN_EPS = 1e-5
ROPE_THETA = 10000.0
D_FF = 2816

GM_CHUNK = 128
GM_WIDTH = 2 * D_MODEL
GM_GROUPS = 8
GM_GROUP_W = GM_WIDTH // GM_GROUPS

NSA_HEADS = 16
NSA_KV_GROUPS = 4
NSA_HPG = NSA_HEADS // NSA_KV_GROUPS
NSA_HEAD_DIM = 64
NSA_KV = NSA_KV_GROUPS * NSA_HEAD_DIM
CMP_LEN = 32
CMP_STRIDE = 16
CMP_HIDDEN = 4 * NSA_HEAD_DIM
SEL_LEN = 64
N_SELECT = 8
WIN = 512

RET_HEADS = 4
RET_QK_DIM = 256
RET_V_DIM = 512
RET_CHUNK = 128
RET_BATCH = 2

LANES = 128
BF16_SUBLANES = 16
LOG2_E = math.log2(math.e)
VMEM_LIMIT = 48 * 1024 * 1024
NEG_BIG = -1e30

ROW_TILE = 512
GM_ROW_TILE = 256
FFN_COL_CHUNK = 256
ATT_TQ = 256
ATT_TK = 128
SEL_CHUNK = 512


def _cparams(*sem):
    return pltpu.CompilerParams(dimension_semantics=sem, vmem_limit_bytes=VMEM_LIMIT)


def _resident(shape):
    nd = len(shape)
    return pl.BlockSpec(shape, lambda *_: (0,) * nd, pipeline_mode=pl.Buffered(1))


def _resident_layer(stacked_shape, layer):
    nd = len(stacked_shape)
    return pl.BlockSpec((None,) + tuple(stacked_shape[1:]), lambda *_: (layer,) + (0,) * (nd - 1),
                        pipeline_mode=pl.Buffered(1))


def _rms(x32, g):
    ms = jnp.mean(x32 * x32, axis=-1, keepdims=True)
    return x32 * lax.rsqrt(ms + RMS_EPS) * g


def _gelu_tanh(x):
    c = math.sqrt(2.0 / math.pi)
    return x * (0.5 * (1.0 + jnp.tanh(c * (x + 0.044715 * (x * x * x)))))


def _sigmoid(x):
    return 1.0 / (1.0 + jnp.exp(-x))


def _dot(a, b):
    return jnp.dot(a, b, preferred_element_type=F32)


def _out_kernel(m_ref, w_ref, g_ref, h_ref, o_ref):
    y = _dot(m_ref[...], w_ref[...])
    o_ref[...] = h_ref[...] + _rms(y, g_ref[...])


def _proj_norm_residual(m2d, w, g, h2d):
    n, k = m2d.shape
    d = w.shape[1]
    tm = ROW_TILE
    return pl.pallas_call(
        _out_kernel,
        grid=(n // tm,),
        in_specs=[pl.BlockSpec((tm, k), lambda i: (i, 0)),
                  _resident((k, d)),
                  _resident((1, d)),
                  pl.BlockSpec((tm, d), lambda i: (i, 0))],
        out_specs=pl.BlockSpec((tm, d), lambda i: (i, 0)),
        out_shape=jax.ShapeDtypeStruct((n, d), F32),
        compiler_params=_cparams("parallel"),
        name="proj_norm_residual",
    )(m2d, w, g.reshape(1, d), h2d)


def _ffn_kernel(h_ref, g_in_ref, wg_ref, wu_ref, wd_ref, g_out_ref, o_ref, acc_ref):
    x = h_ref[...]
    a = _rms(x, g_in_ref[...]).astype(BF16)
    for c in range(D_FF // FFN_COL_CHUNK):
        cols = slice(c * FFN_COL_CHUNK, (c + 1) * FFN_COL_CHUNK)
        gate = _dot(a, wg_ref[:, cols])
        up = _dot(a, wu_ref[:, cols])
        act = (gate * _sigmoid(gate) * up).astype(BF16)
        part = _dot(act, wd_ref[cols, :])
        if c == 0:
            acc_ref[...] = part
        else:
            acc_ref[...] += part
    o_ref[...] = x + _rms(acc_ref[...], g_out_ref[...])


def _ffn(h2d, g_in, wg, wu, wd, g_out, layer):
    n, d = h2d.shape
    tm = ROW_TILE
    return pl.pallas_call(
        _ffn_kernel,
        grid=(n // tm,),
        in_specs=[pl.BlockSpec((tm, d), lambda i: (i, 0)),
                  _resident((1, d)),
                  _resident_layer(wg.shape, layer),
                  _resident_layer(wu.shape, layer),
                  _resident_layer(wd.shape, layer),
                  _resident((1, d))],
        out_specs=pl.BlockSpec((tm, d), lambda i: (i, 0)),
        out_shape=jax.ShapeDtypeStruct((n, d), F32),
        scratch_shapes=[pltpu.VMEM((tm, d), F32)],
        compiler_params=_cparams("parallel"),
        name="swiglu_ffn",
    )(h2d, g_in.reshape(1, d), wg, wu, wd, g_out.reshape(1, d))


def _gmlp_kernel(h_ref, g_in_ref, win_ref, lng_ref, lnb_ref, ws_ref, bst_ref, wout_ref,
                 g_out_ref, o_ref, vn_scr, y_scr):
    x = h_ref[...]
    tm = x.shape[0]
    a = _rms(x, g_in_ref[...]).astype(BF16)

    v = _gelu_tanh(_dot(a, win_ref[:, GM_WIDTH:]))
    mu = jnp.mean(v, axis=-1, keepdims=True)
    vc = v - mu
    var = jnp.mean(vc * vc, axis=-1, keepdims=True)
    vn_scr[...] = (vc * lax.rsqrt(var + LN_EPS) * lng_ref[...] + lnb_ref[...]).astype(BF16)

    t_out = lax.broadcasted_iota(jnp.int32, (GM_CHUNK, GM_CHUNK), 0)
    t_in = lax.broadcasted_iota(jnp.int32, (GM_CHUNK, GM_CHUNK), 1)
    causal = t_in <= t_out
    bst = bst_ref[...]
    for grp in range(GM_GROUPS):
        cols = slice(grp * GM_GROUP_W, (grp + 1) * GM_GROUP_W)
        w_mix = jnp.where(causal, ws_ref[grp], 0.0).astype(BF16)
        bias = bst[:, grp:grp + 1]
        for c in range(tm // GM_CHUNK):
            rows = slice(c * GM_CHUNK, (c + 1) * GM_CHUNK)
            mixed = _dot(w_mix, vn_scr[rows, cols]) + bias
            u = _gelu_tanh(_dot(a[rows], win_ref[:, cols]))
            y_scr[rows, cols] = (u * mixed).astype(BF16)

    m = _dot(y_scr[...], wout_ref[...])
    o_ref[...] = x + _rms(m, g_out_ref[...])


def _gmlp_layer(h2d, g_in, w_in, ln_g, ln_b, w_s, b_s, w_out, g_out, layer):
    n, d = h2d.shape
    tm = GM_ROW_TILE
    return pl.pallas_call(
        _gmlp_kernel,
        grid=(n // tm,),
        in_specs=[pl.BlockSpec((tm, d), lambda i: (i, 0)),
                  _resident((1, d)),
                  _resident_layer(w_in.shape, layer),
                  _resident((1, GM_WIDTH)),
                  _resident((1, GM_WIDTH)),
                  _resident_layer(w_s.shape, layer),
                  _resident((GM_CHUNK, GM_GROUPS)),
                  _resident_layer(w_out.shape, layer),
                  _resident((1, d))],
        out_specs=pl.BlockSpec((tm, d), lambda i: (i, 0)),
        out_shape=jax.ShapeDtypeStruct((n, d), F32),
        scratch_shapes=[pltpu.VMEM((tm, GM_WIDTH), BF16), pltpu.VMEM((tm, GM_WIDTH), BF16)],
        compiler_params=_cparams("parallel"),
        name="gmlp_layer",
    )(h2d, g_in.reshape(1, d), w_in, ln_g.reshape(1, -1), ln_b.reshape(1, -1),
      w_s, b_s.T, w_out, g_out.reshape(1, d))


def _rope_inv_freq(half):
    return ROPE_THETA ** (-jnp.arange(half, dtype=F32) / half)


def _rope_tables(positions, half):
    ang = positions.astype(F32)[..., None] * _rope_inv_freq(half)
    return jnp.cos(ang), jnp.sin(ang)


def _rope_cos_sin(pos_ref, inv_ref):
    tm = pos_ref.shape[-1]
    pos_cols = jnp.broadcast_to(pos_ref[0].astype(F32), (LANES, tm)).T
    ang = pos_cols * inv_ref[...]
    return jnp.cos(ang), jnp.sin(ang)


def _nsa_proj_kernel(x_ref, g_ref, w_ref, pos_ref, inv_ref,
                     q_ref, ks_ref, kw_ref, kvc_ref, vst_ref, vwt_ref, gate_ref, y_scr, *, tiles_per_seq):
    grp, dk, kv = NSA_KV_GROUPS, NSA_HEAD_DIM, NSA_KV
    qw = NSA_HEADS * dk
    tm = x_ref.shape[0]
    a = _rms(x_ref[...], g_ref[...]).astype(BF16)
    lane = lax.broadcasted_iota(jnp.int32, (tm, LANES), 1)
    first_half = (lane & (dk - 1)) < dk // 2
    cos, sin = _rope_cos_sin(pos_ref, inv_ref)
    sin_lo = jnp.where(first_half, -sin, 0.0)
    sin_hi = jnp.where(first_half, 0.0, sin)

    def rope_chunks(y):
        out = []
        for c in range(y.shape[1] // LANES):
            yc = y[:, c * LANES:(c + 1) * LANES]
            out.append(yc * cos + pltpu.roll(yc, LANES - dk // 2, 1) * sin_lo
                       + pltpu.roll(yc, dk // 2, 1) * sin_hi)
        return out

    def proj(k):
        return _dot(a, w_ref[:, qw + k * kv:qw + (k + 1) * kv])

    for c in range(qw // 512):
        for cc, chunk in enumerate(rope_chunks(_dot(a, w_ref[:, c * 512:(c + 1) * 512]))):
            q_ref[:, c * 512 + cc * LANES:c * 512 + (cc + 1) * LANES] = chunk.astype(q_ref.dtype)

    tok = (pl.program_id(0) % tiles_per_seq) * tm + lax.broadcasted_iota(jnp.int32, (tm, LANES), 0)
    onehot = jnp.where((tok >> (SEL_LEN.bit_length() - 1)) == lane - dk, 1.0, 0.0)
    ks_chunks = rope_chunks(proj(0))
    for gi in range(grp):
        chunk = ks_chunks[gi * dk // LANES]
        if (gi * dk) % LANES:
            chunk = pltpu.roll(chunk, LANES - (gi * dk) % LANES, 1)
        ks_ref[0, gi] = jnp.where(lane < dk, chunk, onehot).astype(ks_ref.dtype)

    def store_groups(ref_at, y):
        for gi in range(grp):
            ref_at(gi)[...] = y[:, gi * dk:(gi + 1) * dk].astype(kw_ref.dtype)

    store_groups(lambda gi: kw_ref.at[0, gi], jnp.concatenate(rope_chunks(proj(1)), axis=1))
    n_rows = tm // CMP_STRIDE
    for which in range(2):
        y = proj(2 + which)
        for c in range(kv // LANES):
            y_scr[c] = y[:, c * LANES:(c + 1) * LANES]
        for l in range(CMP_STRIDE):
            for c in range(kv // LANES):
                rows = y_scr[c, pl.ds(l, n_rows, stride=CMP_STRIDE), :]
                for sub in range(LANES // dk):
                    kvc_ref[which, 0, c * (LANES // dk) + sub, :, l * dk:(l + 1) * dk] = (
                        rows[:, sub * dk:(sub + 1) * dk].astype(kvc_ref.dtype))
    for ref, k in ((vst_ref, 4), (vwt_ref, 5)):
        y_t = proj(k).T
        for gi in range(grp):
            ref[0, gi] = y_t[gi * dk:(gi + 1) * dk].astype(ref.dtype)
    gates_t = _sigmoid(_dot(a, w_ref[:, qw + 6 * kv:])).T
    gate_ref[0] = gates_t[:gate_ref.shape[1]]


def _nsa_proj(h2d, g_in, w, positions, b, s):
    n, d = h2d.shape
    grp, dk = NSA_KV_GROUPS, NSA_HEAD_DIM
    tm = ROW_TILE
    per_b = s // tm
    row = lambda i: (i, 0)
    tok_blk = lambda i: (i // per_b, 0, i % per_b, 0)
    tr_blk = lambda i: (i // per_b, 0, 0, i % per_b)
    return pl.pallas_call(
        functools.partial(_nsa_proj_kernel, tiles_per_seq=per_b),
        grid=(n // tm,),
        in_specs=[pl.BlockSpec((tm, d), row), _resident((1, d)), _resident(w.shape),
                  pl.BlockSpec((1, 1, tm), lambda i: (i, 0, 0)), _resident((1, LANES))],
        out_specs=[pl.BlockSpec((tm, NSA_HEADS * dk), row),
                   pl.BlockSpec((1, grp, tm, LANES), tok_blk),
                   pl.BlockSpec((1, grp, tm, dk), tok_blk),
                   pl.BlockSpec((2, 1, grp, tm // CMP_STRIDE, CMP_STRIDE * dk),
                                lambda i: (0, i // per_b, 0, i % per_b, 0)),
                   pl.BlockSpec((1, grp, dk, tm), tr_blk),
                   pl.BlockSpec((1, grp, dk, tm), tr_blk),
                   pl.BlockSpec((1, 3 * NSA_HEADS, tm), lambda i: (i // per_b, 0, i % per_b))],
        out_shape=[jax.ShapeDtypeStruct((n, NSA_HEADS * dk), BF16),
                   jax.ShapeDtypeStruct((b, grp, s, LANES), BF16),
                   jax.ShapeDtypeStruct((b, grp, s, dk), BF16),
                   jax.ShapeDtypeStruct((2, b, grp, s // CMP_STRIDE, CMP_STRIDE * dk), BF16),
                   jax.ShapeDtypeStruct((b, grp, dk, s), BF16),
                   jax.ShapeDtypeStruct((b, grp, dk, s), BF16),
                   jax.ShapeDtypeStruct((b, 3 * NSA_HEADS, s), F32)],
        scratch_shapes=[pltpu.VMEM((NSA_KV // LANES, tm, LANES), F32)],
        compiler_params=_cparams("parallel"),
        name="nsa_proj",
    )(h2d, g_in.reshape(1, d), w, positions.reshape(n // tm, 1, tm),
      jnp.tile(_rope_inv_freq(dk // 2), LANES // (dk // 2)).reshape(1, LANES))


def _cmp_kernel(f_ref, pe_ref, w1_ref, w2_ref, cos_ref, sin_ref, o_ref):
    half = CMP_LEN * NSA_HEAD_DIM // 2
    x = f_ref[0, 0, 0].astype(F32)
    pe = pe_ref[0]
    first = _dot((x + pe[:, :half]).astype(BF16), w1_ref[0, :half, :])
    second = _dot((x + pe[:, half:]).astype(BF16), w1_ref[0, half:, :])
    hid = _gelu_tanh(first + pltpu.roll(second, second.shape[0] - 1, 0))
    y = _dot(hid.astype(BF16), w2_ref[0])
    out = y[:, :NSA_HEAD_DIM] * cos_ref[0, 0] + y[:, NSA_HEAD_DIM:] * sin_ref[0, 0]
    o_ref[0, 0, 0] = out.astype(o_ref.dtype)


def _compress(f, pe_flat, w1, w2cat, cos_t, sin_t):
    _, b, g, n_grp, width = f.shape
    dk = NSA_HEAD_DIM
    return pl.pallas_call(
        _cmp_kernel,
        grid=(2, b, g),
        in_specs=[pl.BlockSpec((1, 1, 1, n_grp, width), lambda s, i, j: (s, i, j, 0, 0)),
                  pl.BlockSpec((1, 1, pe_flat.shape[-1]), lambda s, i, j: (s, 0, 0)),
                  pl.BlockSpec((1,) + w1.shape[1:], lambda s, i, j: (s, 0, 0)),
                  pl.BlockSpec((1,) + w2cat.shape[1:], lambda s, i, j: (s, 0, 0)),
                  pl.BlockSpec((1, 1, n_grp, dk), lambda s, i, j: (s, i, 0, 0)),
                  pl.BlockSpec((1, 1, n_grp, dk), lambda s, i, j: (s, i, 0, 0))],
        out_specs=pl.BlockSpec((1, 1, 1, n_grp, dk), lambda s, i, j: (s, i, j, 0, 0)),
        out_shape=jax.ShapeDtypeStruct((2, b, g, n_grp, dk), BF16),
        compiler_params=_cparams("parallel", "parallel", "parallel"),
        name="nsa_compress",
    )(f, pe_flat, w1, w2cat, cos_t, sin_t)


def _nsa_attn_kernel(q_ref, ks_ref, vst_ref, kw_ref, vwt_ref, kc_ref, vct_ref, gate_ref,
                     o_ref, os_scr, score_scr):
    tq, tk, hpg, dk = ATT_TQ, ATT_TK, NSA_HPG, NSA_HEAD_DIM
    ratio = SEL_LEN // CMP_STRIDE
    ratio_shift = ratio.bit_length() - 1
    sel_shift = SEL_LEN.bit_length() - 1
    n_sel = LANES // ratio
    i = pl.program_id(2)
    t0 = i * tq

    def tile_heads(x):
        return jnp.concatenate([x] * hpg, axis=1)

    q_t = q_ref[...].astype(F32).T
    qt = jnp.concatenate([q_t[h * dk:(h + 1) * dk] for h in range(hpg)], axis=1)
    qt = (qt * (dk ** -0.5 * LOG2_E)).astype(BF16)

    def softmax_pv(parts):
        maxes, accs = [], []
        for s, vt in parts:
            m_part = jnp.max(s, axis=0, keepdims=True)
            p = jnp.exp2(s - m_part).astype(BF16)
            vt_ones = jnp.concatenate([vt, jnp.ones((BF16_SUBLANES, vt.shape[1]), BF16)], axis=0)
            maxes.append(m_part)
            accs.append(_dot(vt_ones, p))
        m = functools.reduce(jnp.maximum, maxes)
        acc = sum(jnp.exp2(m_part - m) * a for m_part, a in zip(maxes, accs))
        return acc[:dk] * (1.0 / jnp.maximum(acc[dk:dk + 1], 1e-30))

    n_win = WIN + tq
    k_lo = pl.multiple_of(jnp.maximum(t0 - WIN, 0), tk)
    n_half = n_win // 2
    row_minus_col = (lax.broadcasted_iota(jnp.int32, (n_half, tq), 0)
                     - lax.broadcasted_iota(jnp.int32, (n_half, tq), 1))
    parts_w = []
    for c in range(2):
        k0 = pl.multiple_of(k_lo + c * n_half, tk)
        back = lax.bitcast_convert_type((t0 - k0) - row_minus_col, jnp.uint32)
        bias_w = jnp.where(back < jnp.uint32(WIN), 0.0, NEG_BIG)
        s_w = _dot(kw_ref[0, 0, pl.ds(k0, n_half), :], qt) + tile_heads(bias_w)
        parts_w.append((s_w, vwt_ref[0, 0, :, pl.ds(k0, n_half)]))
    o_w = softmax_pv(parts_w)

    n_cmp = kc_ref.shape[2]
    row_c = lax.broadcasted_iota(jnp.int32, (n_cmp, tq), 0)
    tok_c = t0 + lax.broadcasted_iota(jnp.int32, (n_cmp, tq), 1)
    n_idx = ((row_c & (n_sel - 1)) << ratio_shift) + (row_c >> (n_sel.bit_length() - 1))
    bias_c = jnp.where(n_idx * CMP_STRIDE + (CMP_LEN - 1) <= tok_c, 0.0, NEG_BIG)
    s_c = _dot(kc_ref[0, 0], qt) + tile_heads(bias_c)
    e_c = jnp.exp2(s_c - jnp.max(s_c, axis=0, keepdims=True))
    tok_row = t0 + (lax.broadcasted_iota(jnp.int32, (1, hpg * tq), 1) & (tq - 1))
    inv_l = jnp.where(tok_row >= CMP_LEN - 1,
                      1.0 / jnp.maximum(jnp.sum(e_c, axis=0, keepdims=True), 1e-30), 0.0)
    p_c = e_c * inv_l
    o_c = _dot(vct_ref[0, 0], p_c.astype(BF16))

    p_sum = p_c[:, 0:tq]
    for h in range(1, hpg):
        p_sum = p_sum + p_c[:, h * tq:(h + 1) * tq]
    imp = p_sum[0:n_sel]
    for r in range(1, ratio):
        imp = imp + p_sum[r * n_sel:(r + 1) * n_sel]

    blk = lax.broadcasted_iota(jnp.int32, (n_sel, tq), 0)
    cur = (t0 + lax.broadcasted_iota(jnp.int32, (n_sel, tq), 1)) >> sel_shift
    forced = (blk == 0) | (blk == cur) | (blk == cur - 1)
    score = jnp.where(blk <= cur, jnp.where(forced, jnp.inf, imp), -jnp.inf)
    score_scr[...] = score
    grp_rows = 8
    row_in_grp = lax.broadcasted_iota(jnp.int32, (grp_rows, tq), 0)
    groups = [score[a:a + grp_rows] for a in range(0, n_sel, grp_rows)]
    ranks = [jnp.zeros((grp_rows, tq), F32) for _ in groups]
    for rival in range(n_sel):
        other = score_scr[rival:rival + 1, :]
        for a, mine in enumerate(groups):
            lo = a * grp_rows
            if rival < lo:
                beats = other >= mine
            elif rival >= lo + grp_rows:
                beats = other > mine
            else:
                beats = (other > mine) | ((other == mine) & (row_in_grp + lo > rival))
            ranks[a] = ranks[a] + jnp.where(beats, 1.0, 0.0)
    rank = jnp.concatenate(ranks, axis=0)
    sel_bias = jnp.where(rank < float(N_SELECT), 0.0, NEG_BIG).astype(BF16)

    pad_rows = ks_ref.shape[3] - dk - n_sel
    qt_aug = jnp.concatenate([qt, tile_heads(sel_bias), jnp.zeros((pad_rows, hpg * tq), BF16)], axis=0)
    tok_s = t0 + lax.broadcasted_iota(jnp.int32, (SEL_CHUNK, tq), 1)
    key_s = lax.broadcasted_iota(jnp.int32, (SEL_CHUNK, tq), 0)
    n_variants = ks_ref.shape[2] // SEL_CHUNK
    for v in range(n_variants):
        @pl.when((t0 >> (SEL_CHUNK.bit_length() - 1)) == v)
        def _():
            parts = []
            for c in range(v + 1):
                rows = slice(c * SEL_CHUNK, (c + 1) * SEL_CHUNK)
                s = _dot(ks_ref[0, 0, rows, :], qt_aug)
                if c == v:
                    s = s + tile_heads(jnp.where(c * SEL_CHUNK + key_s <= tok_s, 0.0, NEG_BIG))
                parts.append((s, vst_ref[0, 0, :, rows]))
            os_scr[...] = softmax_pv(parts)
    o_s = os_scr[...]

    row0 = pl.program_id(1) * (3 * hpg)
    gate = lambda c: jnp.concatenate([gate_ref[0, pl.ds(row0 + 3 * h + c, 1), :] for h in range(hpg)], axis=1)
    out_t = gate(0) * o_c + gate(1) * o_s + gate(2) * o_w
    stacked = jnp.concatenate([out_t[:, h * tq:(h + 1) * tq] for h in range(hpg)], axis=0)
    o_ref[...] = stacked.T.astype(o_ref.dtype)


def _nsa_attention(q2d, ks_aug, vst, kw, vwt, kc, vct, gates):
    b, g, s, dk = kw.shape
    hpg = NSA_HPG
    n_cmp = kc.shape[2]
    tq = ATT_TQ
    nt = s // tq
    assert s % SEL_CHUNK == 0 and SEL_CHUNK % tq == 0 and s >= WIN + tq
    k_spec = pl.BlockSpec((1, 1, s, dk), lambda bi, gi, i: (bi, gi, 0, 0))
    vt_spec = pl.BlockSpec((1, 1, dk, s), lambda bi, gi, i: (bi, gi, 0, 0))
    return pl.pallas_call(
        _nsa_attn_kernel,
        grid=(b, g, nt),
        in_specs=[pl.BlockSpec((tq, hpg * dk), lambda bi, gi, i: (bi * nt + i, gi)),
                  pl.BlockSpec((1, 1, s, ks_aug.shape[3]), lambda bi, gi, i: (bi, gi, 0, 0)),
                  vt_spec, k_spec, vt_spec,
                  pl.BlockSpec((1, 1, n_cmp, dk), lambda bi, gi, i: (bi, gi, 0, 0)),
                  pl.BlockSpec((1, 1, dk, n_cmp), lambda bi, gi, i: (bi, gi, 0, 0)),
                  pl.BlockSpec((1, gates.shape[1], tq), lambda bi, gi, i: (bi, 0, i))],
        out_specs=pl.BlockSpec((tq, hpg * dk), lambda bi, gi, i: (bi * nt + i, gi)),
        out_shape=jax.ShapeDtypeStruct((b * s, g * hpg * dk), BF16),
        scratch_shapes=[pltpu.VMEM((dk, hpg * tq), F32), pltpu.VMEM((n_cmp * CMP_STRIDE // SEL_LEN, tq), F32)],
        compiler_params=_cparams("parallel", "parallel", "parallel"),
        name="nsa_attention",
    )(q2d, ks_aug, vst, kw, vwt, kc, vct, gates)


def _nsa_layer(h2d, positions, g_in, w_in, cmp_pe, cmp_w1, cmp_w2, w_out, g_out):
    b, s = positions.shape
    n, d = h2d.shape
    heads, grp, hpg, dk = NSA_HEADS, NSA_KV_GROUPS, NSA_HPG, NSA_HEAD_DIM
    half = dk // 2
    qw = heads * dk
    col = lambda k: w_in[:, qw + k * NSA_KV: qw + (k + 1) * NSA_KV]
    w_gate = jnp.pad(w_in[:, qw + 6 * NSA_KV:], ((0, 0), (0, LANES - 3 * heads)))
    w_all = jnp.concatenate([w_in[:, :qw], col(2), col(4), col(0), col(1), col(3), col(5), w_gate],
                            axis=1).astype(BF16)

    q2d, ks_aug, kw, f, vst, vwt, gates = _nsa_proj(h2d, g_in, w_all, positions, b, s)

    n_grp = s // CMP_STRIDE
    pe_flat = cmp_pe.reshape(2, 1, CMP_LEN * dk)
    rot = lambda w: jnp.concatenate([-w[..., half:], w[..., :half]], axis=-1)
    w2cat = jnp.concatenate([cmp_w2, rot(cmp_w2)], axis=-1).astype(BF16)
    n_cmp = (s - CMP_LEN) // CMP_STRIDE + 1
    pad = ((0, 0), (0, n_grp - n_cmp), (0, 0))
    cos, sin = _rope_tables(positions[:, CMP_LEN - 1::CMP_STRIDE], half)
    cos_e = jnp.pad(cos, pad, constant_values=1.0)
    sin_e = jnp.pad(sin, pad)
    cos_c = jnp.stack([jnp.concatenate([cos_e, cos_e], -1), jnp.ones((b, n_grp, dk), F32)])
    sin_c = jnp.stack([jnp.concatenate([sin_e, sin_e], -1), jnp.zeros((b, n_grp, dk), F32)])
    cmp = _compress(f, pe_flat, cmp_w1.astype(BF16), w2cat, cos_c, sin_c)
    ratio = SEL_LEN // CMP_STRIDE
    cmp = (cmp.reshape(2, b, grp, n_grp // ratio, ratio, dk).transpose(0, 1, 2, 4, 3, 5)
           .reshape(2, b, grp, n_grp, dk))

    o2d = _nsa_attention(q2d, ks_aug, vst, kw, vwt, cmp[0], cmp[1].transpose(0, 1, 3, 2), gates)
    return _proj_norm_residual(o2d, w_out.astype(BF16), g_out, h2d)


def _ret_proj_kernel(x_ref, g_ref, w_ref, pos_ref, inv_ref, qk_ref, vg_ref):
    heads, dk = RET_HEADS, RET_QK_DIM
    half = dk // 2
    a = _rms(x_ref[...], g_ref[...]).astype(BF16)
    cos, sin = _rope_cos_sin(pos_ref, inv_ref)
    for c in range(2 * heads):
        y = _dot(a, w_ref[:, c * dk:(c + 1) * dk])
        y1, y2 = y[:, :half], y[:, half:]
        scale = 1.0 if c < heads else dk ** -0.5
        qk_ref[:, c * dk:c * dk + half] = ((y1 * cos - y2 * sin) * scale).astype(qk_ref.dtype)
        qk_ref[:, c * dk + half:(c + 1) * dk] = ((y2 * cos + y1 * sin) * scale).astype(qk_ref.dtype)
    n_qk = 2 * heads * dk
    for c in range(vg_ref.shape[1] // RET_V_DIM):
        cols = slice(c * RET_V_DIM, (c + 1) * RET_V_DIM)
        vg_ref[:, cols] = _dot(a, w_ref[:, n_qk + c * RET_V_DIM:n_qk + (c + 1) * RET_V_DIM]).astype(vg_ref.dtype)


def _retention_proj(h2d, g_in, w, positions):
    n, d = h2d.shape
    n_qk = 2 * RET_HEADS * RET_QK_DIM
    n_vg = w.shape[1] - n_qk
    tm = ROW_TILE
    return pl.pallas_call(
        _ret_proj_kernel,
        grid=(n // tm,),
        in_specs=[pl.BlockSpec((tm, d), lambda i: (i, 0)),
                  _resident((1, d)),
                  _resident(w.shape),
                  pl.BlockSpec((1, 1, tm), lambda i: (i, 0, 0)), _resident((1, RET_QK_DIM // 2))],
        out_specs=[pl.BlockSpec((tm, n_qk), lambda i: (i, 0)),
                   pl.BlockSpec((tm, n_vg), lambda i: (i, 0))],
        out_shape=[jax.ShapeDtypeStruct((n, n_qk), BF16), jax.ShapeDtypeStruct((n, n_vg), BF16)],
        compiler_params=_cparams("parallel"),
        name="retention_proj",
    )(h2d, g_in.reshape(1, d), w, positions.reshape(n // tm, 1, tm),
      _rope_inv_freq(RET_QK_DIM // 2).reshape(1, -1))


def _ret_kernel(qk_ref, vg_ref, gn_ref, decay_ref, xi_ref, zeta_ref, gc_ref, wout_ref, g_out_ref, h_ref,
                o_ref, state_scr, y_scr):
    heads, dk, dv, c = RET_HEADS, RET_QK_DIM, RET_V_DIM, RET_CHUNK

    @pl.when(pl.program_id(1) == 0)
    def _():
        state_scr[...] = jnp.zeros(state_scr.shape, F32)

    for bb in range(RET_BATCH):
        for h in range(heads):
            q = qk_ref[bb, :, h * dk:(h + 1) * dk]
            k = qk_ref[bb, :, (heads + h) * dk:(heads + h + 1) * dk]
            v = vg_ref[bb, :, h * dv:(h + 1) * dv]
            s = lax.dot_general(q, k, (((1,), (1,)), ((), ())), preferred_element_type=F32)
            inner = _dot((s * decay_ref[h]).astype(BF16), v)
            state = state_scr[bb * heads + h]
            cross = _dot(q, state.astype(BF16)) * xi_ref[h]
            kz = (k.astype(F32) * zeta_ref[h]).T.astype(BF16)
            state_scr[bb * heads + h] = gc_ref[h] * state + _dot(kz, v)

            o = inner + cross
            mu = jnp.mean(o, axis=-1, keepdims=True)
            oc = o - mu
            var = jnp.mean(oc * oc, axis=-1, keepdims=True)
            on = oc * lax.rsqrt(var + LN_EPS) * gn_ref[:, h * dv:(h + 1) * dv]
            gate = vg_ref[bb, :, (heads + h) * dv:(heads + h + 1) * dv].astype(F32)
            y_scr[bb * c:(bb + 1) * c, h * dv:(h + 1) * dv] = (gate * _sigmoid(gate) * on).astype(y_scr.dtype)

    m = _dot(y_scr[...], wout_ref[...])
    for bb in range(RET_BATCH):
        o_ref[bb] = h_ref[bb] + _rms(m[bb * c:(bb + 1) * c], g_out_ref[...])


def _retention_core(qk, vg, gn_g, w_out, g_out, h2d, b, s):
    heads, dk, dv, c = RET_HEADS, RET_QK_DIM, RET_V_DIM, RET_CHUNK
    n_ch = s // c
    d = h2d.shape[1]
    bb = RET_BATCH
    assert b % bb == 0
    log_g = jnp.log1p(-(2.0 ** (-5.0 - jnp.arange(heads, dtype=F32))))
    ix = jnp.arange(c, dtype=F32)
    rel = ix[:, None] - ix[None, :]
    decay = jnp.where(rel >= 0, jnp.exp(log_g[:, None, None] * jnp.maximum(rel, 0.0)), 0.0)
    xi = jnp.exp(log_g[:, None] * (ix + 1.0))[:, :, None]
    zeta = jnp.exp(log_g[:, None] * (c - 1.0 - ix))[:, :, None]
    g_chunk = jnp.exp(log_g * c)[:, None, None]
    seq_blk = lambda width: pl.BlockSpec((bb, c, width), lambda bi, ci: (bi, ci, 0))
    out = pl.pallas_call(
        _ret_kernel,
        grid=(b // bb, n_ch),
        in_specs=[seq_blk(qk.shape[1]), seq_blk(vg.shape[1]),
                  _resident((1, heads * dv)),
                  _resident(decay.shape), _resident(xi.shape), _resident(zeta.shape),
                  _resident(g_chunk.shape), _resident(w_out.shape), _resident((1, d)),
                  seq_blk(d)],
        out_specs=seq_blk(d),
        out_shape=jax.ShapeDtypeStruct((b, s, d), F32),
        scratch_shapes=[pltpu.VMEM((bb * heads, dk, dv), F32), pltpu.VMEM((bb * c, heads * dv), BF16)],
        compiler_params=_cparams("parallel", "arbitrary"),
        name="retention_core",
    )(qk.reshape(b, s, -1), vg.reshape(b, s, -1), gn_g.reshape(1, -1), decay, xi, zeta, g_chunk,
      w_out, g_out.reshape(1, d), h2d.reshape(b, s, d))
    return out.reshape(b * s, d)


def _retention_layer(h2d, positions, g_in, w_in, gn_g, w_out, g_out):
    b, s = positions.shape
    qk, vg = _retention_proj(h2d, g_in, w_in.astype(BF16), positions)
    return _retention_core(qk, vg, gn_g, w_out.astype(BF16), g_out, h2d, b, s)


def kernel(x, positions, norm_g, ffn_w_gate, ffn_w_up, ffn_w_down, gm_w_in, gm_ln_g, gm_ln_b, gm_w_s, gm_b_s, gm_w_out, nsa_w_in, nsa_cmp_pe, nsa_cmp_w1, nsa_cmp_w2, nsa_w_out, ret_w_in, ret_gn_g, ret_w_out):
    b, s, d = x.shape
    h = x.reshape(b * s, d)
    ffn_wg, ffn_wu, ffn_wd = (w.astype(BF16) for w in (ffn_w_gate, ffn_w_up, ffn_w_down))
    gm_w_in_bf, gm_w_out_bf = gm_w_in.astype(BF16), gm_w_out.astype(BF16)
    for i in range(DEPTH):
        kind, j = i % N_MIXERS, i // N_MIXERS
        if kind == 0:
            h = _gmlp_layer(h, norm_g[i, 0], gm_w_in_bf, gm_ln_g[j], gm_ln_b[j], gm_w_s,
                            gm_b_s[j], gm_w_out_bf, norm_g[i, 1], j)
        elif kind == 1:
            h = _nsa_layer(h, positions, norm_g[i, 0], nsa_w_in[j],
                           nsa_cmp_pe[j], nsa_cmp_w1[j], nsa_cmp_w2[j], nsa_w_out[j], norm_g[i, 1])
        else:
            h = _retention_layer(h, positions, norm_g[i, 0], ret_w_in[j], ret_gn_g[j], ret_w_out[j],
                                 norm_g[i, 1])
        h = _ffn(h, norm_g[i, 2], ffn_wg, ffn_wu, ffn_wd, norm_g[i, 3], i)
    return h.reshape(b, s, d)
```

```python
import functools
import math

import jax
import jax.numpy as jnp
from jax import lax
from jax.experimental import pallas as pl
from jax.experimental.pallas import tpu as pltpu

F32 = jnp.float32
BF16 = jnp.bfloat16

D_MODEL = 1024
DEPTH = 4
N_MIXERS = 3
RMS_EPS = 1e-6
LN_EPS = 1e-5
ROPE_THETA = 10000.0
D_FF = 2816

GM_CHUNK = 128
GM_WIDTH = 2 * D_MODEL
GM_GROUPS = 8
GM_GROUP_W = GM_WIDTH // GM_GROUPS

NSA_HEADS = 16
NSA_KV_GROUPS = 4
NSA_HPG = NSA_HEADS // NSA_KV_GROUPS
NSA_HEAD_DIM = 64
NSA_KV = NSA_KV_GROUPS * NSA_HEAD_DIM
CMP_LEN = 32
CMP_STRIDE = 16
CMP_HIDDEN = 4 * NSA_HEAD_DIM
SEL_LEN = 64
N_SELECT = 8
WIN = 512

RET_HEADS = 4
RET_QK_DIM = 256
RET_V_DIM = 512
RET_CHUNK = 128
RET_BATCH = 2

LANES = 128
BF16_SUBLANES = 16
LOG2_E = math.log2(math.e)
VMEM_LIMIT = 48 * 1024 * 1024
NEG_BIG = -1e30

ROW_TILE = 512
GM_ROW_TILE = 512
GM_SUB_TILE = 256
FFN_COL_CHUNK = 256
ATT_TQ = 256
ATT_TK = 128
SEL_CHUNK = 512


def _cparams(*sem):
    return pltpu.CompilerParams(dimension_semantics=sem, vmem_limit_bytes=VMEM_LIMIT)


def _resident(shape):
    nd = len(shape)
    return pl.BlockSpec(shape, lambda *_: (0,) * nd, pipeline_mode=pl.Buffered(1))


def _resident_layer(stacked_shape, layer):
    nd = len(stacked_shape)
    return pl.BlockSpec((None,) + tuple(stacked_shape[1:]), lambda *_: (layer,) + (0,) * (nd - 1),
                        pipeline_mode=pl.Buffered(1))


def _rms(x32, g):
    ms = jnp.mean(x32 * x32, axis=-1, keepdims=True)
    return x32 * lax.rsqrt(ms + RMS_EPS) * g


def _gelu_tanh(x):
    c = math.sqrt(2.0 / math.pi)
    half_x = 0.5 * x
    return half_x + half_x * jnp.tanh(x * (c + (c * 0.044715) * (x * x)))


def _sigmoid(x):
    return 1.0 / (1.0 + jnp.exp(-x))


def _dot(a, b):
    return jnp.dot(a, b, preferred_element_type=F32)


def _ffn_kernel(*refs, mixer_proj):
    if mixer_proj:
        m_ref, wm_ref, gm_ref, h_ref, g_in_ref, wg_ref, wu_ref, wd_ref, g_out_ref, o_ref, acc_ref = refs
        x = h_ref[...] + _rms(_dot(m_ref[...], wm_ref[...]), gm_ref[...])
    else:
        h_ref, g_in_ref, wg_ref, wu_ref, wd_ref, g_out_ref, o_ref, acc_ref = refs
        x = h_ref[...]
    a = _rms(x, g_in_ref[...]).astype(BF16)
    for c in range(D_FF // FFN_COL_CHUNK):
        cols = slice(c * FFN_COL_CHUNK, (c + 1) * FFN_COL_CHUNK)
        gate = _dot(a, wg_ref[:, cols])
        up = _dot(a, wu_ref[:, cols])
        act = (gate * _sigmoid(gate) * up).astype(BF16)
        part = _dot(act, wd_ref[cols, :])
        if c == 0:
            acc_ref[...] = part
        else:
            acc_ref[...] += part
    o_ref[...] = x + _rms(acc_ref[...], g_out_ref[...])


def _ffn(h2d, g_in, wg, wu, wd, g_out, layer, mixer=None):
    n, d = h2d.shape
    tm = ROW_TILE
    mixer_specs, mixer_args = [], []
    if mixer is not None:
        m2d, w_m, g_m = mixer
        mixer_specs = [pl.BlockSpec((tm, m2d.shape[1]), lambda i: (i, 0)), _resident(w_m.shape), _resident((1, d))]
        mixer_args = [m2d, w_m, g_m.reshape(1, d)]
    return pl.pallas_call(
        functools.partial(_ffn_kernel, mixer_proj=mixer is not None),
        grid=(n // tm,),
        in_specs=mixer_specs + [pl.BlockSpec((tm, d), lambda i: (i, 0)),
                  _resident((1, d)),
                  _resident_layer(wg.shape, layer),
                  _resident_layer(wu.shape, layer),
                  _resident_layer(wd.shape, layer),
                  _resident((1, d))],
        out_specs=pl.BlockSpec((tm, d), lambda i: (i, 0)),
        out_shape=jax.ShapeDtypeStruct((n, d), F32),
        scratch_shapes=[pltpu.VMEM((tm, d), F32)],
        compiler_params=_cparams("parallel"),
        name="swiglu_ffn",
    )(*mixer_args, h2d, g_in.reshape(1, d), wg, wu, wd, g_out.reshape(1, d))


def _gmlp_kernel(h_ref, g_in_ref, win_ref, lng_ref, lnb_ref, ws_ref, bst_ref, wout_ref,
                 g_out_ref, o_ref, vn_scr, y_scr):
    t_out = lax.broadcasted_iota(jnp.int32, (GM_CHUNK, GM_CHUNK), 0)
    t_in = lax.broadcasted_iota(jnp.int32, (GM_CHUNK, GM_CHUNK), 1)
    causal = t_in <= t_out
    bst = bst_ref[...]
    for sub in range(h_ref.shape[0] // GM_SUB_TILE):
        tile = slice(sub * GM_SUB_TILE, (sub + 1) * GM_SUB_TILE)
        x = h_ref[tile, :]
        a = _rms(x, g_in_ref[...]).astype(BF16)

        v = _gelu_tanh(_dot(a, win_ref[:, GM_WIDTH:]))
        mu = jnp.mean(v, axis=-1, keepdims=True)
        vc = v - mu
        var = jnp.mean(vc * vc, axis=-1, keepdims=True)
        vn_scr[tile, :] = (vc * lax.rsqrt(var + LN_EPS) * lng_ref[...] + lnb_ref[...]).astype(BF16)

        for grp in range(GM_GROUPS):
            cols = slice(grp * GM_GROUP_W, (grp + 1) * GM_GROUP_W)
            w_mix = jnp.where(causal, ws_ref[grp], 0.0).astype(BF16)
            bias = bst[:, grp:grp + 1]
            for c in range(GM_SUB_TILE // GM_CHUNK):
                lrows = slice(c * GM_CHUNK, (c + 1) * GM_CHUNK)
                rows = slice(sub * GM_SUB_TILE + c * GM_CHUNK, sub * GM_SUB_TILE + (c + 1) * GM_CHUNK)
                mixed = _dot(w_mix, vn_scr[rows, cols]) + bias
                u = _gelu_tanh(_dot(a[lrows], win_ref[:, cols]))
                y_scr[rows, cols] = (u * mixed).astype(BF16)

        m = _dot(y_scr[tile, :], wout_ref[...])
        o_ref[tile, :] = x + _rms(m, g_out_ref[...])


def _gmlp_layer(h2d, g_in, w_in, ln_g, ln_b, w_s, b_s, w_out, g_out, layer):
    n, d = h2d.shape
    tm = GM_ROW_TILE
    return pl.pallas_call(
        _gmlp_kernel,
        grid=(n // tm,),
        in_specs=[pl.BlockSpec((tm, d), lambda i: (i, 0)),
                  _resident((1, d)),
                  _resident_layer(w_in.shape, layer),
                  _resident((1, GM_WIDTH)),
                  _resident((1, GM_WIDTH)),
                  _resident_layer(w_s.shape, layer),
                  _resident((GM_CHUNK, GM_GROUPS)),
                  _resident_layer(w_out.shape, layer),
                  _resident((1, d))],
        out_specs=pl.BlockSpec((tm, d), lambda i: (i, 0)),
        out_shape=jax.ShapeDtypeStruct((n, d), F32),
        scratch_shapes=[pltpu.VMEM((tm, GM_WIDTH), BF16), pltpu.VMEM((tm, GM_WIDTH), BF16)],
        compiler_params=_cparams("parallel"),
        name="gmlp_layer",
    )(h2d, g_in.reshape(1, d), w_in, ln_g.reshape(1, -1), ln_b.reshape(1, -1),
      w_s, b_s.T, w_out, g_out.reshape(1, d))


def _rope_inv_freq(half):
    return ROPE_THETA ** (-jnp.arange(half, dtype=F32) / half)


def _rope_tables(positions, half):
    ang = positions.astype(F32)[..., None] * _rope_inv_freq(half)
    return jnp.cos(ang), jnp.sin(ang)


def _rope_cos_sin(pos_ref, inv_ref):
    tm = pos_ref.shape[-1]
    pos_cols = jnp.broadcast_to(pos_ref[0].astype(F32), (LANES, tm)).T
    ang = pos_cols * inv_ref[...]
    return jnp.cos(ang), jnp.sin(ang)


def _nsa_proj_kernel(x_ref, g_ref, w_ref, pos_ref, inv_ref,
                     q_ref, ks_ref, kw_ref, kvc_ref, vst_ref, vwt_ref, gate_ref, y_scr, *, tiles_per_seq):
    grp, dk, kv = NSA_KV_GROUPS, NSA_HEAD_DIM, NSA_KV
    qw = NSA_HEADS * dk
    tm = x_ref.shape[0]
    a = _rms(x_ref[...], g_ref[...]).astype(BF16)
    lane = lax.broadcasted_iota(jnp.int32, (tm, LANES), 1)
    first_half = (lane & (dk - 1)) < dk // 2
    cos, sin = _rope_cos_sin(pos_ref, inv_ref)
    sin_lo = jnp.where(first_half, -sin, 0.0)
    sin_hi = jnp.where(first_half, 0.0, sin)

    def rope_chunks(y):
        out = []
        for c in range(y.shape[1] // LANES):
            yc = y[:, c * LANES:(c + 1) * LANES]
            out.append(yc * cos + pltpu.roll(yc, LANES - dk // 2, 1) * sin_lo
                       + pltpu.roll(yc, dk // 2, 1) * sin_hi)
        return out

    def proj(k):
        return _dot(a, w_ref[:, qw + k * kv:qw + (k + 1) * kv])

    for c in range(qw // 512):
        for cc, chunk in enumerate(rope_chunks(_dot(a, w_ref[:, c * 512:(c + 1) * 512]))):
            q_ref[:, c * 512 + cc * LANES:c * 512 + (cc + 1) * LANES] = chunk.astype(q_ref.dtype)

    tok = (pl.program_id(0) % tiles_per_seq) * tm + lax.broadcasted_iota(jnp.int32, (tm, LANES), 0)
    onehot = jnp.where((tok >> (SEL_LEN.bit_length() - 1)) == lane - dk, 1.0, 0.0)
    ks_chunks = rope_chunks(proj(0))
    for gi in range(grp):
        chunk = ks_chunks[gi * dk // LANES]
        if (gi * dk) % LANES:
            chunk = pltpu.roll(chunk, LANES - (gi * dk) % LANES, 1)
        ks_ref[0, gi] = jnp.where(lane < dk, chunk, onehot).astype(ks_ref.dtype)

    def store_groups(ref_at, y):
        for gi in range(grp):
            ref_at(gi)[...] = y[:, gi * dk:(gi + 1) * dk].astype(kw_ref.dtype)

    store_groups(lambda gi: kw_ref.at[0, gi], jnp.concatenate(rope_chunks(proj(1)), axis=1))
    n_rows = tm // CMP_STRIDE
    for which in range(2):
        y = proj(2 + which)
        for c in range(kv // LANES):
            y_scr[c] = y[:, c * LANES:(c + 1) * LANES]
        for l in range(CMP_STRIDE):
            for c in range(kv // LANES):
                rows = y_scr[c, pl.ds(l, n_rows, stride=CMP_STRIDE), :]
                for sub in range(LANES // dk):
                    kvc_ref[which, 0, c * (LANES // dk) + sub, :, l * dk:(l + 1) * dk] = (
                        rows[:, sub * dk:(sub + 1) * dk].astype(kvc_ref.dtype))
    for ref, k in ((vst_ref, 4), (vwt_ref, 5)):
        y_t = proj(k).T
        for gi in range(grp):
            ref[0, gi] = y_t[gi * dk:(gi + 1) * dk].astype(ref.dtype)
    gates_t = _sigmoid(_dot(a, w_ref[:, qw + 6 * kv:])).T
    gate_ref[0] = gates_t[:gate_ref.shape[1]]


def _nsa_proj(h2d, g_in, w, positions, b, s):
    n, d = h2d.shape
    grp, dk = NSA_KV_GROUPS, NSA_HEAD_DIM
    tm = ROW_TILE
    per_b = s // tm
    row = lambda i: (i, 0)
    tok_blk = lambda i: (i // per_b, 0, i % per_b, 0)
    tr_blk = lambda i: (i // per_b, 0, 0, i % per_b)
    return pl.pallas_call(
        functools.partial(_nsa_proj_kernel, tiles_per_seq=per_b),
        grid=(n // tm,),
        in_specs=[pl.BlockSpec((tm, d), row), _resident((1, d)), _resident(w.shape),
                  pl.BlockSpec((1, 1, tm), lambda i: (i, 0, 0)), _resident((1, LANES))],
        out_specs=[pl.BlockSpec((tm, NSA_HEADS * dk), row),
                   pl.BlockSpec((1, grp, tm, LANES), tok_blk),
                   pl.BlockSpec((1, grp, tm, dk), tok_blk),
                   pl.BlockSpec((2, 1, grp, tm // CMP_STRIDE, CMP_STRIDE * dk),
                                lambda i: (0, i // per_b, 0, i % per_b, 0)),
                   pl.BlockSpec((1, grp, dk, tm), tr_blk),
                   pl.BlockSpec((1, grp, dk, tm), tr_blk),
                   pl.BlockSpec((1, 3 * NSA_HEADS, tm), lambda i: (i // per_b, 0, i % per_b))],
        out_shape=[jax.ShapeDtypeStruct((n, NSA_HEADS * dk), BF16),
                   jax.ShapeDtypeStruct((b, grp, s, LANES), BF16),
                   jax.ShapeDtypeStruct((b, grp, s, dk), BF16),
                   jax.ShapeDtypeStruct((2, b, grp, s // CMP_STRIDE, CMP_STRIDE * dk), BF16),
                   jax.ShapeDtypeStruct((b, grp, dk, s), BF16),
                   jax.ShapeDtypeStruct((b, grp, dk, s), BF16),
                   jax.ShapeDtypeStruct((b, 3 * NSA_HEADS, s), F32)],
        scratch_shapes=[pltpu.VMEM((NSA_KV // LANES, tm, LANES), F32)],
        compiler_params=_cparams("parallel"),
        name="nsa_proj",
    )(h2d, g_in.reshape(1, d), w, positions.reshape(n // tm, 1, tm),
      jnp.tile(_rope_inv_freq(dk // 2), LANES // (dk // 2)).reshape(1, LANES))


def _cmp_kernel(f_ref, pe_ref, w1_ref, w2_ref, cos_ref, sin_ref, o_ref):
    half = CMP_LEN * NSA_HEAD_DIM // 2
    x = f_ref[0, 0, 0].astype(F32)
    pe = pe_ref[0]
    first = _dot((x + pe[:, :half]).astype(BF16), w1_ref[0, :half, :])
    second = _dot((x + pe[:, half:]).astype(BF16), w1_ref[0, half:, :])
    hid = _gelu_tanh(first + pltpu.roll(second, second.shape[0] - 1, 0))
    y = _dot(hid.astype(BF16), w2_ref[0])
    out = y[:, :NSA_HEAD_DIM] * cos_ref[0, 0] + y[:, NSA_HEAD_DIM:] * sin_ref[0, 0]
    o_ref[0, 0, 0] = out.astype(o_ref.dtype)


def _compress(f, pe_flat, w1, w2cat, cos_t, sin_t):
    _, b, g, n_grp, width = f.shape
    dk = NSA_HEAD_DIM
    return pl.pallas_call(
        _cmp_kernel,
        grid=(2, b, g),
        in_specs=[pl.BlockSpec((1, 1, 1, n_grp, width), lambda s, i, j: (s, i, j, 0, 0)),
                  pl.BlockSpec((1, 1, pe_flat.shape[-1]), lambda s, i, j: (s, 0, 0)),
                  pl.BlockSpec((1,) + w1.shape[1:], lambda s, i, j: (s, 0, 0)),
                  pl.BlockSpec((1,) + w2cat.shape[1:], lambda s, i, j: (s, 0, 0)),
                  pl.BlockSpec((1, 1, n_grp, dk), lambda s, i, j: (s, i, 0, 0)),
                  pl.BlockSpec((1, 1, n_grp, dk), lambda s, i, j: (s, i, 0, 0))],
        out_specs=pl.BlockSpec((1, 1, 1, n_grp, dk), lambda s, i, j: (s, i, j, 0, 0)),
        out_shape=jax.ShapeDtypeStruct((2, b, g, n_grp, dk), BF16),
        compiler_params=_cparams("parallel", "parallel", "parallel"),
        name="nsa_compress",
    )(f, pe_flat, w1, w2cat, cos_t, sin_t)


def _nsa_attn_kernel(q_ref, ks_ref, vst_ref, kw_ref, vwt_ref, kc_ref, vct_ref, gate_ref,
                     o_ref, os_scr, score_scr):
    tq, tk, hpg, dk = ATT_TQ, ATT_TK, NSA_HPG, NSA_HEAD_DIM
    ratio = SEL_LEN // CMP_STRIDE
    ratio_shift = ratio.bit_length() - 1
    sel_shift = SEL_LEN.bit_length() - 1
    n_sel = LANES // ratio
    i = pl.program_id(2)
    t0 = i * tq

    def tile_heads(x):
        return jnp.concatenate([x] * hpg, axis=1)

    q_t = q_ref[...].astype(F32).T
    qt = jnp.concatenate([q_t[h * dk:(h + 1) * dk] for h in range(hpg)], axis=1)
    qt = (qt * (dk ** -0.5 * LOG2_E)).astype(BF16)

    def softmax_pv(parts):
        maxes, accs = [], []
        for s, vt in parts:
            m_part = jnp.max(s, axis=0, keepdims=True)
            p = jnp.exp2(s - m_part).astype(BF16)
            vt_ones = jnp.concatenate([vt, jnp.ones((BF16_SUBLANES, vt.shape[1]), BF16)], axis=0)
            maxes.append(m_part)
            accs.append(_dot(vt_ones, p))
        m = functools.reduce(jnp.maximum, maxes)
        acc = sum(jnp.exp2(m_part - m) * a for m_part, a in zip(maxes, accs))
        return acc[:dk] * (1.0 / jnp.maximum(acc[dk:dk + 1], 1e-30))

    n_win = WIN + tq
    k_lo = pl.multiple_of(jnp.maximum(t0 - WIN, 0), tk)
    n_half = n_win // 2
    row_minus_col = (lax.broadcasted_iota(jnp.int32, (n_half, tq), 0)
                     - lax.broadcasted_iota(jnp.int32, (n_half, tq), 1))
    parts_w = []
    for c in range(2):
        k0 = pl.multiple_of(k_lo + c * n_half, tk)
        back = lax.bitcast_convert_type((t0 - k0) - row_minus_col, jnp.uint32)
        bias_w = jnp.where(back < jnp.uint32(WIN), 0.0, NEG_BIG)
        s_w = _dot(kw_ref[0, 0, pl.ds(k0, n_half), :], qt) + tile_heads(bias_w)
        parts_w.append((s_w, vwt_ref[0, 0, :, pl.ds(k0, n_half)]))
    o_w = softmax_pv(parts_w)

    n_cmp = kc_ref.shape[2]
    row_c = lax.broadcasted_iota(jnp.int32, (n_cmp, tq), 0)
    tok_c = t0 + lax.broadcasted_iota(jnp.int32, (n_cmp, tq), 1)
    n_idx = ((row_c & (n_sel - 1)) << ratio_shift) + (row_c >> (n_sel.bit_length() - 1))
    bias_c = jnp.where(n_idx * CMP_STRIDE + (CMP_LEN - 1) <= tok_c, 0.0, NEG_BIG)
    s_c = _dot(kc_ref[0, 0], qt) + tile_heads(bias_c)
    e_c = jnp.exp2(s_c - jnp.max(s_c, axis=0, keepdims=True))
    tok_row = t0 + (lax.broadcasted_iota(jnp.int32, (1, hpg * tq), 1) & (tq - 1))
    inv_l = jnp.where(tok_row >= CMP_LEN - 1,
                      1.0 / jnp.maximum(jnp.sum(e_c, axis=0, keepdims=True), 1e-30), 0.0)
    p_c = e_c * inv_l
    o_c = _dot(vct_ref[0, 0], p_c.astype(BF16))

    p_sum = p_c[:, 0:tq]
    for h in range(1, hpg):
        p_sum = p_sum + p_c[:, h * tq:(h + 1) * tq]
    imp = p_sum[0:n_sel]
    for r in range(1, ratio):
        imp = imp + p_sum[r * n_sel:(r + 1) * n_sel]

    blk = lax.broadcasted_iota(jnp.int32, (n_sel, tq), 0)
    cur = (t0 + lax.broadcasted_iota(jnp.int32, (n_sel, tq), 1)) >> sel_shift
    forced = (blk == 0) | (blk == cur) | (blk == cur - 1)
    score = jnp.where(blk <= cur, jnp.where(forced, jnp.inf, imp), -jnp.inf)
    score_scr[...] = score
    grp_rows = 8
    row_in_grp = lax.broadcasted_iota(jnp.int32, (grp_rows, tq), 0)
    groups = [score[a:a + grp_rows] for a in range(0, n_sel, grp_rows)]
    ranks = [jnp.zeros((grp_rows, tq), F32) for _ in groups]
    for rival in range(n_sel):
        other = score_scr[rival:rival + 1, :]
        for a, mine in enumerate(groups):
            lo = a * grp_rows
            if rival < lo:
                beats = other >= mine
            elif rival >= lo + grp_rows:
                beats = other > mine
            else:
                beats = (other > mine) | ((other == mine) & (row_in_grp + lo > rival))
            ranks[a] = ranks[a] + jnp.where(beats, 1.0, 0.0)
    rank = jnp.concatenate(ranks, axis=0)
    sel_bias = jnp.where(rank < float(N_SELECT), 0.0, NEG_BIG).astype(BF16)

    pad_rows = ks_ref.shape[3] - dk - n_sel
    qt_aug = jnp.concatenate([qt, tile_heads(sel_bias), jnp.zeros((pad_rows, hpg * tq), BF16)], axis=0)
    tok_s = t0 + lax.broadcasted_iota(jnp.int32, (SEL_CHUNK, tq), 1)
    key_s = lax.broadcasted_iota(jnp.int32, (SEL_CHUNK, tq), 0)
    n_variants = ks_ref.shape[2] // SEL_CHUNK
    for v in range(n_variants):
        @pl.when((t0 >> (SEL_CHUNK.bit_length() - 1)) == v)
        def _():
            parts = []
            for c in range(v + 1):
                rows = slice(c * SEL_CHUNK, (c + 1) * SEL_CHUNK)
                s = _dot(ks_ref[0, 0, rows, :], qt_aug)
                if c == v:
                    s = s + tile_heads(jnp.where(c * SEL_CHUNK + key_s <= tok_s, 0.0, NEG_BIG))
                parts.append((s, vst_ref[0, 0, :, rows]))
            os_scr[...] = softmax_pv(parts)
    o_s = os_scr[...]

    row0 = pl.program_id(1) * (3 * hpg)
    gate = lambda c: jnp.concatenate([gate_ref[0, pl.ds(row0 + 3 * h + c, 1), :] for h in range(hpg)], axis=1)
    out_t = gate(0) * o_c + gate(1) * o_s + gate(2) * o_w
    stacked = jnp.concatenate([out_t[:, h * tq:(h + 1) * tq] for h in range(hpg)], axis=0)
    o_ref[...] = stacked.T.astype(o_ref.dtype)


def _nsa_attention(q2d, ks_aug, vst, kw, vwt, kc, vct, gates):
    b, g, s, dk = kw.shape
    hpg = NSA_HPG
    n_cmp = kc.shape[2]
    tq = ATT_TQ
    nt = s // tq
    assert s % SEL_CHUNK == 0 and SEL_CHUNK % tq == 0 and s >= WIN + tq
    k_spec = pl.BlockSpec((1, 1, s, dk), lambda bi, gi, i: (bi, gi, 0, 0))
    vt_spec = pl.BlockSpec((1, 1, dk, s), lambda bi, gi, i: (bi, gi, 0, 0))
    return pl.pallas_call(
        _nsa_attn_kernel,
        grid=(b, g, nt),
        in_specs=[pl.BlockSpec((tq, hpg * dk), lambda bi, gi, i: (bi * nt + i, gi)),
                  pl.BlockSpec((1, 1, s, ks_aug.shape[3]), lambda bi, gi, i: (bi, gi, 0, 0)),
                  vt_spec, k_spec, vt_spec,
                  pl.BlockSpec((1, 1, n_cmp, dk), lambda bi, gi, i: (bi, gi, 0, 0)),
                  pl.BlockSpec((1, 1, dk, n_cmp), lambda bi, gi, i: (bi, gi, 0, 0)),
                  pl.BlockSpec((1, gates.shape[1], tq), lambda bi, gi, i: (bi, 0, i))],
        out_specs=pl.BlockSpec((tq, hpg * dk), lambda bi, gi, i: (bi * nt + i, gi)),
        out_shape=jax.ShapeDtypeStruct((b * s, g * hpg * dk), BF16),
        scratch_shapes=[pltpu.VMEM((dk, hpg * tq), F32), pltpu.VMEM((n_cmp * CMP_STRIDE // SEL_LEN, tq), F32)],
        compiler_params=_cparams("parallel", "parallel", "parallel"),
        name="nsa_attention",
    )(q2d, ks_aug, vst, kw, vwt, kc, vct, gates)


def _nsa_layer(h2d, positions, g_in, w_in, cmp_pe, cmp_w1, cmp_w2):
    b, s = positions.shape
    n, d = h2d.shape
    heads, grp, hpg, dk = NSA_HEADS, NSA_KV_GROUPS, NSA_HPG, NSA_HEAD_DIM
    half = dk // 2
    qw = heads * dk
    col = lambda k: w_in[:, qw + k * NSA_KV: qw + (k + 1) * NSA_KV]
    w_gate = jnp.pad(w_in[:, qw + 6 * NSA_KV:], ((0, 0), (0, LANES - 3 * heads)))
    w_all = jnp.concatenate([w_in[:, :qw], col(2), col(4), col(0), col(1), col(3), col(5), w_gate],
                            axis=1).astype(BF16)

    q2d, ks_aug, kw, f, vst, vwt, gates = _nsa_proj(h2d, g_in, w_all, positions, b, s)

    n_grp = s // CMP_STRIDE
    pe_flat = cmp_pe.reshape(2, 1, CMP_LEN * dk)
    rot = lambda w: jnp.concatenate([-w[..., half:], w[..., :half]], axis=-1)
    w2cat = jnp.concatenate([cmp_w2, rot(cmp_w2)], axis=-1).astype(BF16)
    n_cmp = (s - CMP_LEN) // CMP_STRIDE + 1
    pad = ((0, 0), (0, n_grp - n_cmp), (0, 0))
    cos, sin = _rope_tables(positions[:, CMP_LEN - 1::CMP_STRIDE], half)
    cos_e = jnp.pad(cos, pad, constant_values=1.0)
    sin_e = jnp.pad(sin, pad)
    cos_c = jnp.stack([jnp.concatenate([cos_e, cos_e], -1), jnp.ones((b, n_grp, dk), F32)])
    sin_c = jnp.stack([jnp.concatenate([sin_e, sin_e], -1), jnp.zeros((b, n_grp, dk), F32)])
    cmp = _compress(f, pe_flat, cmp_w1.astype(BF16), w2cat, cos_c, sin_c)
    ratio = SEL_LEN // CMP_STRIDE
    cmp = (cmp.reshape(2, b, grp, n_grp // ratio, ratio, dk).transpose(0, 1, 2, 4, 3, 5)
           .reshape(2, b, grp, n_grp, dk))

    return _nsa_attention(q2d, ks_aug, vst, kw, vwt, cmp[0], cmp[1].transpose(0, 1, 3, 2), gates)


def _ret_proj_kernel(x_ref, g_ref, w_ref, pos_ref, inv_ref, qk_ref, vg_ref):
    heads, dk = RET_HEADS, RET_QK_DIM
    half = dk // 2
    a = _rms(x_ref[...], g_ref[...]).astype(BF16)
    cos, sin = _rope_cos_sin(pos_ref, inv_ref)
    for c in range(2 * heads):
        y = _dot(a, w_ref[:, c * dk:(c + 1) * dk])
        y1, y2 = y[:, :half], y[:, half:]
        scale = 1.0 if c < heads else dk ** -0.5
        qk_ref[:, c * dk:c * dk + half] = ((y1 * cos - y2 * sin) * scale).astype(qk_ref.dtype)
        qk_ref[:, c * dk + half:(c + 1) * dk] = ((y2 * cos + y1 * sin) * scale).astype(qk_ref.dtype)
    n_qk = 2 * heads * dk
    for c in range(vg_ref.shape[1] // RET_V_DIM):
        cols = slice(c * RET_V_DIM, (c + 1) * RET_V_DIM)
        vg_ref[:, cols] = _dot(a, w_ref[:, n_qk + c * RET_V_DIM:n_qk + (c + 1) * RET_V_DIM]).astype(vg_ref.dtype)


def _retention_proj(h2d, g_in, w, positions):
    n, d = h2d.shape
    n_qk = 2 * RET_HEADS * RET_QK_DIM
    n_vg = w.shape[1] - n_qk
    tm = ROW_TILE
    return pl.pallas_call(
        _ret_proj_kernel,
        grid=(n // tm,),
        in_specs=[pl.BlockSpec((tm, d), lambda i: (i, 0)),
                  _resident((1, d)),
                  _resident(w.shape),
                  pl.BlockSpec((1, 1, tm), lambda i: (i, 0, 0)), _resident((1, RET_QK_DIM // 2))],
        out_specs=[pl.BlockSpec((tm, n_qk), lambda i: (i, 0)),
                   pl.BlockSpec((tm, n_vg), lambda i: (i, 0))],
        out_shape=[jax.ShapeDtypeStruct((n, n_qk), BF16), jax.ShapeDtypeStruct((n, n_vg), BF16)],
        compiler_params=_cparams("parallel"),
        name="retention_proj",
    )(h2d, g_in.reshape(1, d), w, positions.reshape(n // tm, 1, tm),
      _rope_inv_freq(RET_QK_DIM // 2).reshape(1, -1))


def _ret_kernel(qk_ref, vg_ref, gn_ref, decay_ref, xi_ref, zeta_ref, gc_ref, wout_ref, g_out_ref, h_ref,
                o_ref, state_scr, y_scr):
    heads, dk, dv, c = RET_HEADS, RET_QK_DIM, RET_V_DIM, RET_CHUNK

    @pl.when(pl.program_id(1) == 0)
    def _():
        state_scr[...] = jnp.zeros(state_scr.shape, F32)

    for bb in range(RET_BATCH):
        for h in range(heads):
            q = qk_ref[bb, :, h * dk:(h + 1) * dk]
            k = qk_ref[bb, :, (heads + h) * dk:(heads + h + 1) * dk]
            v = vg_ref[bb, :, h * dv:(h + 1) * dv]
            s = lax.dot_general(q, k, (((1,), (1,)), ((), ())), preferred_element_type=F32)
            inner = _dot((s * decay_ref[h]).astype(BF16), v)
            state = state_scr[bb * heads + h]
            cross = _dot(q, state.astype(BF16)) * xi_ref[h]
            kz = (k.astype(F32) * zeta_ref[h]).T.astype(BF16)
            state_scr[bb * heads + h] = gc_ref[h] * state + _dot(kz, v)

            o = inner + cross
            mu = jnp.mean(o, axis=-1, keepdims=True)
            oc = o - mu
            var = jnp.mean(oc * oc, axis=-1, keepdims=True)
            on = oc * lax.rsqrt(var + LN_EPS) * gn_ref[:, h * dv:(h + 1) * dv]
            gate = vg_ref[bb, :, (heads + h) * dv:(heads + h + 1) * dv].astype(F32)
            y_scr[bb * c:(bb + 1) * c, h * dv:(h + 1) * dv] = (gate * _sigmoid(gate) * on).astype(y_scr.dtype)

    m = _dot(y_scr[...], wout_ref[...])
    for bb in range(RET_BATCH):
        o_ref[bb] = h_ref[bb] + _rms(m[bb * c:(bb + 1) * c], g_out_ref[...])


def _retention_core(qk, vg, gn_g, w_out, g_out, h2d, b, s):
    heads, dk, dv, c = RET_HEADS, RET_QK_DIM, RET_V_DIM, RET_CHUNK
    n_ch = s // c
    d = h2d.shape[1]
    bb = RET_BATCH
    assert b % bb == 0
    log_g = jnp.log1p(-(2.0 ** (-5.0 - jnp.arange(heads, dtype=F32))))
    ix = jnp.arange(c, dtype=F32)
    rel = ix[:, None] - ix[None, :]
    decay = jnp.where(rel >= 0, jnp.exp(log_g[:, None, None] * jnp.maximum(rel, 0.0)), 0.0)
    xi = jnp.exp(log_g[:, None] * (ix + 1.0))[:, :, None]
    zeta = jnp.exp(log_g[:, None] * (c - 1.0 - ix))[:, :, None]
    g_chunk = jnp.exp(log_g * c)[:, None, None]
    seq_blk = lambda width: pl.BlockSpec((bb, c, width), lambda bi, ci: (bi, ci, 0))
    out = pl.pallas_call(
        _ret_kernel,
        grid=(b // bb, n_ch),
        in_specs=[seq_blk(qk.shape[1]), seq_blk(vg.shape[1]),
                  _resident((1, heads * dv)),
                  _resident(decay.shape), _resident(xi.shape), _resident(zeta.shape),
                  _resident(g_chunk.shape), _resident(w_out.shape), _resident((1, d)),
                  seq_blk(d)],
        out_specs=seq_blk(d),
        out_shape=jax.ShapeDtypeStruct((b, s, d), F32),
        scratch_shapes=[pltpu.VMEM((bb * heads, dk, dv), F32), pltpu.VMEM((bb * c, heads * dv), BF16)],
        compiler_params=_cparams("parallel", "arbitrary"),
        name="retention_core",
    )(qk.reshape(b, s, -1), vg.reshape(b, s, -1), gn_g.reshape(1, -1), decay, xi, zeta, g_chunk,
      w_out, g_out.reshape(1, d), h2d.reshape(b, s, d))
    return out.reshape(b * s, d)


def _retention_layer(h2d, positions, g_in, w_in, gn_g, w_out, g_out):
    b, s = positions.shape
    qk, vg = _retention_proj(h2d, g_in, w_in.astype(BF16), positions)
    return _retention_core(qk, vg, gn_g, w_out.astype(BF16), g_out, h2d, b, s)


def kernel(x, positions, norm_g, ffn_w_gate, ffn_w_up, ffn_w_down, gm_w_in, gm_ln_g, gm_ln_b, gm_w_s, gm_b_s, gm_w_out, nsa_w_in, nsa_cmp_pe, nsa_cmp_w1, nsa_cmp_w2, nsa_w_out, ret_w_in, ret_gn_g, ret_w_out):
    b, s, d = x.shape
    h = x.reshape(b * s, d)
    ffn_wg, ffn_wu, ffn_wd = (w.astype(BF16) for w in (ffn_w_gate, ffn_w_up, ffn_w_down))
    gm_w_in_bf, gm_w_out_bf = gm_w_in.astype(BF16), gm_w_out.astype(BF16)
    for i in range(DEPTH):
        kind, j = i % N_MIXERS, i // N_MIXERS
        mixer = None
        if kind == 0:
            h = _gmlp_layer(h, norm_g[i, 0], gm_w_in_bf, gm_ln_g[j], gm_ln_b[j], gm_w_s,
                            gm_b_s[j], gm_w_out_bf, norm_g[i, 1], j)
        elif kind == 1:
            attn = _nsa_layer(h, positions, norm_g[i, 0], nsa_w_in[j], nsa_cmp_pe[j], nsa_cmp_w1[j],
                              nsa_cmp_w2[j])
            mixer = (attn, nsa_w_out[j].astype(BF16), norm_g[i, 1])
        else:
            h = _retention_layer(h, positions, norm_g[i, 0], ret_w_in[j], ret_gn_g[j], ret_w_out[j],
                                 norm_g[i, 1])
        h = _ffn(h, norm_g[i, 2], ffn_wg, ffn_wu, ffn_wd, norm_g[i, 3], i, mixer)
    return h.reshape(b, s, d)
```

```python
import functools
import math

import jax
import jax.numpy as jnp
from jax import lax
from jax.experimental import pallas as pl
from jax.experimental.pallas import tpu as pltpu

F32 = jnp.float32
BF16 = jnp.bfloat16

D_MODEL = 1024
DEPTH = 4
N_MIXERS = 3
RMS_EPS = 1e-6
LN_EPS = 1e-5
ROPE_THETA = 10000.0
D_FF = 2816

GM_CHUNK = 128
GM_WIDTH = 2 * D_MODEL
GM_GROUPS = 8
GM_GROUP_W = GM_WIDTH // GM_GROUPS

NSA_HEADS = 16
NSA_KV_GROUPS = 4
NSA_HPG = NSA_HEADS // NSA_KV_GROUPS
NSA_HEAD_DIM = 64
NSA_KV = NSA_KV_GROUPS * NSA_HEAD_DIM
CMP_LEN = 32
CMP_STRIDE = 16
CMP_HIDDEN = 4 * NSA_HEAD_DIM
SEL_LEN = 64
N_SELECT = 8
WIN = 512

RET_HEADS = 4
RET_QK_DIM = 256
RET_V_DIM = 512
RET_CHUNK = 128
RET_BATCH = 4

LANES = 128
BF16_SUBLANES = 16
LOG2_E = math.log2(math.e)
VMEM_LIMIT = 48 * 1024 * 1024
NEG_BIG = -1e30

ROW_TILE = 512
GM_ROW_TILE = 512
GM_SUB_TILE = 256
FFN_COL_CHUNK = 256
ATT_TQ = 256
ATT_TK = 128
SEL_CHUNK = 512


def _cparams(*sem):
    return pltpu.CompilerParams(dimension_semantics=sem, vmem_limit_bytes=VMEM_LIMIT)


def _resident(shape):
    nd = len(shape)
    return pl.BlockSpec(shape, lambda *_: (0,) * nd, pipeline_mode=pl.Buffered(1))


def _resident_layer(stacked_shape, layer):
    nd = len(stacked_shape)
    return pl.BlockSpec((None,) + tuple(stacked_shape[1:]), lambda *_: (layer,) + (0,) * (nd - 1),
                        pipeline_mode=pl.Buffered(1))


def _rms(x32, g):
    ms = jnp.mean(x32 * x32, axis=-1, keepdims=True)
    return x32 * lax.rsqrt(ms + RMS_EPS) * g


def _gelu_tanh(x):
    c = math.sqrt(2.0 / math.pi)
    half_x = 0.5 * x
    return half_x + half_x * jnp.tanh(x * (c + (c * 0.044715) * (x * x)))


def _sigmoid(x):
    return 1.0 / (1.0 + jnp.exp(-x))


def _dot(a, b):
    return jnp.dot(a, b, preferred_element_type=F32)


def _ffn_kernel(*refs, mixer_proj):
    if mixer_proj:
        m_ref, wm_ref, gm_ref, h_ref, g_in_ref, wg_ref, wu_ref, wd_ref, g_out_ref, o_ref, acc_ref = refs
        x = h_ref[...] + _rms(_dot(m_ref[...], wm_ref[...]), gm_ref[...])
    else:
        h_ref, g_in_ref, wg_ref, wu_ref, wd_ref, g_out_ref, o_ref, acc_ref = refs
        x = h_ref[...]
    a = _rms(x, g_in_ref[...]).astype(BF16)
    for c in range(D_FF // FFN_COL_CHUNK):
        cols = slice(c * FFN_COL_CHUNK, (c + 1) * FFN_COL_CHUNK)
        gate = _dot(a, wg_ref[:, cols])
        up = _dot(a, wu_ref[:, cols])
        act = (gate * _sigmoid(gate) * up).astype(BF16)
        part = _dot(act, wd_ref[cols, :])
        if c == 0:
            acc_ref[...] = part
        else:
            acc_ref[...] += part
    o_ref[...] = x + _rms(acc_ref[...], g_out_ref[...])


def _ffn(h2d, g_in, wg, wu, wd, g_out, layer, mixer=None):
    n, d = h2d.shape
    tm = ROW_TILE
    mixer_specs, mixer_args = [], []
    if mixer is not None:
        m2d, w_m, g_m = mixer
        mixer_specs = [pl.BlockSpec((tm, m2d.shape[1]), lambda i: (i, 0)), _resident(w_m.shape), _resident((1, d))]
        mixer_args = [m2d, w_m, g_m.reshape(1, d)]
    return pl.pallas_call(
        functools.partial(_ffn_kernel, mixer_proj=mixer is not None),
        grid=(n // tm,),
        in_specs=mixer_specs + [pl.BlockSpec((tm, d), lambda i: (i, 0)),
                  _resident((1, d)),
                  _resident_layer(wg.shape, layer),
                  _resident_layer(wu.shape, layer),
                  _resident_layer(wd.shape, layer),
                  _resident((1, d))],
        out_specs=pl.BlockSpec((tm, d), lambda i: (i, 0)),
        out_shape=jax.ShapeDtypeStruct((n, d), F32),
        scratch_shapes=[pltpu.VMEM((tm, d), F32)],
        compiler_params=_cparams("parallel"),
        name="swiglu_ffn",
    )(*mixer_args, h2d, g_in.reshape(1, d), wg, wu, wd, g_out.reshape(1, d))


def _gmlp_kernel(h_ref, g_in_ref, win_ref, lng_ref, lnb_ref, ws_ref, bst_ref, wout_ref,
                 g_out_ref, o_ref, vn_scr, y_scr):
    t_out = lax.broadcasted_iota(jnp.int32, (GM_CHUNK, GM_CHUNK), 0)
    t_in = lax.broadcasted_iota(jnp.int32, (GM_CHUNK, GM_CHUNK), 1)
    causal = t_in <= t_out
    bst = bst_ref[...]
    for sub in range(h_ref.shape[0] // GM_SUB_TILE):
        tile = slice(sub * GM_SUB_TILE, (sub + 1) * GM_SUB_TILE)
        x = h_ref[tile, :]
        a = _rms(x, g_in_ref[...]).astype(BF16)

        v = _gelu_tanh(_dot(a, win_ref[:, GM_WIDTH:]))
        mu = jnp.mean(v, axis=-1, keepdims=True)
        vc = v - mu
        var = jnp.mean(vc * vc, axis=-1, keepdims=True)
        vn_scr[tile, :] = (vc * lax.rsqrt(var + LN_EPS) * lng_ref[...] + lnb_ref[...]).astype(BF16)

        for grp in range(GM_GROUPS):
            cols = slice(grp * GM_GROUP_W, (grp + 1) * GM_GROUP_W)
            w_mix = jnp.where(causal, ws_ref[grp], 0.0).astype(BF16)
            bias = bst[:, grp:grp + 1]
            for c in range(GM_SUB_TILE // GM_CHUNK):
                lrows = slice(c * GM_CHUNK, (c + 1) * GM_CHUNK)
                rows = slice(sub * GM_SUB_TILE + c * GM_CHUNK, sub * GM_SUB_TILE + (c + 1) * GM_CHUNK)
                mixed = _dot(w_mix, vn_scr[rows, cols]) + bias
                u = _gelu_tanh(_dot(a[lrows], win_ref[:, cols]))
                y_scr[rows, cols] = (u * mixed).astype(BF16)

        m = _dot(y_scr[tile, :], wout_ref[...])
        o_ref[tile, :] = x + _rms(m, g_out_ref[...])


def _gmlp_layer(h2d, g_in, w_in, ln_g, ln_b, w_s, b_s, w_out, g_out, layer):
    n, d = h2d.shape
    tm = GM_ROW_TILE
    return pl.pallas_call(
        _gmlp_kernel,
        grid=(n // tm,),
        in_specs=[pl.BlockSpec((tm, d), lambda i: (i, 0)),
                  _resident((1, d)),
                  _resident_layer(w_in.shape, layer),
                  _resident((1, GM_WIDTH)),
                  _resident((1, GM_WIDTH)),
                  _resident_layer(w_s.shape, layer),
                  _resident((GM_CHUNK, GM_GROUPS)),
                  _resident_layer(w_out.shape, layer),
                  _resident((1, d))],
        out_specs=pl.BlockSpec((tm, d), lambda i: (i, 0)),
        out_shape=jax.ShapeDtypeStruct((n, d), F32),
        scratch_shapes=[pltpu.VMEM((tm, GM_WIDTH), BF16), pltpu.VMEM((tm, GM_WIDTH), BF16)],
        compiler_params=_cparams("parallel"),
        name="gmlp_layer",
    )(h2d, g_in.reshape(1, d), w_in, ln_g.reshape(1, -1), ln_b.reshape(1, -1),
      w_s, b_s.T, w_out, g_out.reshape(1, d))


def _rope_inv_freq(half):
    return ROPE_THETA ** (-jnp.arange(half, dtype=F32) / half)


def _rope_tables(positions, half):
    ang = positions.astype(F32)[..., None] * _rope_inv_freq(half)
    return jnp.cos(ang), jnp.sin(ang)


def _rope_cos_sin(pos_ref, inv_ref):
    tm = pos_ref.shape[-1]
    pos_cols = jnp.broadcast_to(pos_ref[0].astype(F32), (LANES, tm)).T
    ang = pos_cols * inv_ref[...]
    return jnp.cos(ang), jnp.sin(ang)


def _nsa_proj_kernel(x_ref, g_ref, w_ref, pos_ref, inv_ref,
                     q_ref, ks_ref, kw_ref, kvc_ref, vst_ref, vwt_ref, gate_ref, y_scr, *, tiles_per_seq):
    grp, dk, kv = NSA_KV_GROUPS, NSA_HEAD_DIM, NSA_KV
    qw = NSA_HEADS * dk
    tm = x_ref.shape[0]
    a = _rms(x_ref[...], g_ref[...]).astype(BF16)
    lane = lax.broadcasted_iota(jnp.int32, (tm, LANES), 1)
    first_half = (lane & (dk - 1)) < dk // 2
    cos, sin = _rope_cos_sin(pos_ref, inv_ref)
    sin_lo = jnp.where(first_half, -sin, 0.0)
    sin_hi = jnp.where(first_half, 0.0, sin)

    def rope_chunks(y):
        out = []
        for c in range(y.shape[1] // LANES):
            yc = y[:, c * LANES:(c + 1) * LANES]
            out.append(yc * cos + pltpu.roll(yc, LANES - dk // 2, 1) * sin_lo
                       + pltpu.roll(yc, dk // 2, 1) * sin_hi)
        return out

    def proj(k):
        return _dot(a, w_ref[:, qw + k * kv:qw + (k + 1) * kv])

    for c in range(qw // 512):
        for cc, chunk in enumerate(rope_chunks(_dot(a, w_ref[:, c * 512:(c + 1) * 512]))):
            q_ref[:, c * 512 + cc * LANES:c * 512 + (cc + 1) * LANES] = chunk.astype(q_ref.dtype)

    tok = (pl.program_id(0) % tiles_per_seq) * tm + lax.broadcasted_iota(jnp.int32, (tm, LANES), 0)
    onehot = jnp.where((tok >> (SEL_LEN.bit_length() - 1)) == lane - dk, 1.0, 0.0)
    ks_chunks = rope_chunks(proj(0))
    for gi in range(grp):
        chunk = ks_chunks[gi * dk // LANES]
        if (gi * dk) % LANES:
            chunk = pltpu.roll(chunk, LANES - (gi * dk) % LANES, 1)
        ks_ref[0, gi] = jnp.where(lane < dk, chunk, onehot).astype(ks_ref.dtype)

    def store_groups(ref_at, y):
        for gi in range(grp):
            ref_at(gi)[...] = y[:, gi * dk:(gi + 1) * dk].astype(kw_ref.dtype)

    store_groups(lambda gi: kw_ref.at[0, gi], jnp.concatenate(rope_chunks(proj(1)), axis=1))
    n_rows = tm // CMP_STRIDE
    for which in range(2):
        y = proj(2 + which)
        for c in range(kv // LANES):
            y_scr[c] = y[:, c * LANES:(c + 1) * LANES]
        for l in range(CMP_STRIDE):
            for c in range(kv // LANES):
                rows = y_scr[c, pl.ds(l, n_rows, stride=CMP_STRIDE), :]
                for sub in range(LANES // dk):
                    kvc_ref[which, 0, c * (LANES // dk) + sub, :, l * dk:(l + 1) * dk] = (
                        rows[:, sub * dk:(sub + 1) * dk].astype(kvc_ref.dtype))
    for ref, k in ((vst_ref, 4), (vwt_ref, 5)):
        y_t = proj(k).T
        for gi in range(grp):
            ref[0, gi] = y_t[gi * dk:(gi + 1) * dk].astype(ref.dtype)
    gates_t = _sigmoid(_dot(a, w_ref[:, qw + 6 * kv:])).T
    gate_ref[0] = gates_t[:gate_ref.shape[1]]


def _nsa_proj(h2d, g_in, w, positions, b, s):
    n, d = h2d.shape
    grp, dk = NSA_KV_GROUPS, NSA_HEAD_DIM
    tm = ROW_TILE
    per_b = s // tm
    row = lambda i: (i, 0)
    tok_blk = lambda i: (i // per_b, 0, i % per_b, 0)
    tr_blk = lambda i: (i // per_b, 0, 0, i % per_b)
    return pl.pallas_call(
        functools.partial(_nsa_proj_kernel, tiles_per_seq=per_b),
        grid=(n // tm,),
        in_specs=[pl.BlockSpec((tm, d), row), _resident((1, d)), _resident(w.shape),
                  pl.BlockSpec((1, 1, tm), lambda i: (i, 0, 0)), _resident((1, LANES))],
        out_specs=[pl.BlockSpec((tm, NSA_HEADS * dk), row),
                   pl.BlockSpec((1, grp, tm, LANES), tok_blk),
                   pl.BlockSpec((1, grp, tm, dk), tok_blk),
                   pl.BlockSpec((2, 1, grp, tm // CMP_STRIDE, CMP_STRIDE * dk),
                                lambda i: (0, i // per_b, 0, i % per_b, 0)),
                   pl.BlockSpec((1, grp, dk, tm), tr_blk),
                   pl.BlockSpec((1, grp, dk, tm), tr_blk),
                   pl.BlockSpec((1, 3 * NSA_HEADS, tm), lambda i: (i // per_b, 0, i % per_b))],
        out_shape=[jax.ShapeDtypeStruct((n, NSA_HEADS * dk), BF16),
                   jax.ShapeDtypeStruct((b, grp, s, LANES), BF16),
                   jax.ShapeDtypeStruct((b, grp, s, dk), BF16),
                   jax.ShapeDtypeStruct((2, b, grp, s // CMP_STRIDE, CMP_STRIDE * dk), BF16),
                   jax.ShapeDtypeStruct((b, grp, dk, s), BF16),
                   jax.ShapeDtypeStruct((b, grp, dk, s), BF16),
                   jax.ShapeDtypeStruct((b, 3 * NSA_HEADS, s), F32)],
        scratch_shapes=[pltpu.VMEM((NSA_KV // LANES, tm, LANES), F32)],
        compiler_params=_cparams("parallel"),
        name="nsa_proj",
    )(h2d, g_in.reshape(1, d), w, positions.reshape(n // tm, 1, tm),
      jnp.tile(_rope_inv_freq(dk // 2), LANES // (dk // 2)).reshape(1, LANES))


def _cmp_kernel(f_ref, pe_ref, w1_ref, w2_ref, cos_ref, sin_ref, o_ref):
    half = CMP_LEN * NSA_HEAD_DIM // 2
    x = f_ref[0, 0, 0].astype(F32)
    pe = pe_ref[0]
    first = _dot((x + pe[:, :half]).astype(BF16), w1_ref[0, :half, :])
    second = _dot((x + pe[:, half:]).astype(BF16), w1_ref[0, half:, :])
    hid = _gelu_tanh(first + pltpu.roll(second, second.shape[0] - 1, 0))
    y = _dot(hid.astype(BF16), w2_ref[0])
    out = y[:, :NSA_HEAD_DIM] * cos_ref[0, 0] + y[:, NSA_HEAD_DIM:] * sin_ref[0, 0]
    o_ref[0, 0, 0] = out.astype(o_ref.dtype)


def _compress(f, pe_flat, w1, w2cat, cos_t, sin_t):
    _, b, g, n_grp, width = f.shape
    dk = NSA_HEAD_DIM
    return pl.pallas_call(
        _cmp_kernel,
        grid=(2, b, g),
        in_specs=[pl.BlockSpec((1, 1, 1, n_grp, width), lambda s, i, j: (s, i, j, 0, 0)),
                  pl.BlockSpec((1, 1, pe_flat.shape[-1]), lambda s, i, j: (s, 0, 0)),
                  pl.BlockSpec((1,) + w1.shape[1:], lambda s, i, j: (s, 0, 0)),
                  pl.BlockSpec((1,) + w2cat.shape[1:], lambda s, i, j: (s, 0, 0)),
                  pl.BlockSpec((1, 1, n_grp, dk), lambda s, i, j: (s, i, 0, 0)),
                  pl.BlockSpec((1, 1, n_grp, dk), lambda s, i, j: (s, i, 0, 0))],
        out_specs=pl.BlockSpec((1, 1, 1, n_grp, dk), lambda s, i, j: (s, i, j, 0, 0)),
        out_shape=jax.ShapeDtypeStruct((2, b, g, n_grp, dk), BF16),
        compiler_params=_cparams("parallel", "parallel", "parallel"),
        name="nsa_compress",
    )(f, pe_flat, w1, w2cat, cos_t, sin_t)


def _nsa_attn_kernel(q_ref, ks_ref, vst_ref, kw_ref, vwt_ref, kc_ref, vct_ref, gate_ref,
                     o_ref, os_scr, score_scr):
    tq, tk, hpg, dk = ATT_TQ, ATT_TK, NSA_HPG, NSA_HEAD_DIM
    ratio = SEL_LEN // CMP_STRIDE
    ratio_shift = ratio.bit_length() - 1
    sel_shift = SEL_LEN.bit_length() - 1
    n_sel = LANES // ratio
    i = pl.program_id(2)
    t0 = i * tq

    def tile_heads(x):
        return jnp.concatenate([x] * hpg, axis=1)

    q_t = q_ref[...].astype(F32).T
    qt = jnp.concatenate([q_t[h * dk:(h + 1) * dk] for h in range(hpg)], axis=1)
    qt = (qt * (dk ** -0.5 * LOG2_E)).astype(BF16)

    def softmax_pv(parts):
        maxes, accs = [], []
        for s, vt in parts:
            m_part = jnp.max(s, axis=0, keepdims=True)
            p = jnp.exp2(s - m_part).astype(BF16)
            vt_ones = jnp.concatenate([vt, jnp.ones((BF16_SUBLANES, vt.shape[1]), BF16)], axis=0)
            maxes.append(m_part)
            accs.append(_dot(vt_ones, p))
        m = functools.reduce(jnp.maximum, maxes)
        acc = sum(jnp.exp2(m_part - m) * a for m_part, a in zip(maxes, accs))
        return acc[:dk] * (1.0 / jnp.maximum(acc[dk:dk + 1], 1e-30))

    n_win = WIN + tq
    k_lo = pl.multiple_of(jnp.maximum(t0 - WIN, 0), tk)
    n_half = n_win // 2
    row_minus_col = (lax.broadcasted_iota(jnp.int32, (n_half, tq), 0)
                     - lax.broadcasted_iota(jnp.int32, (n_half, tq), 1))
    parts_w = []
    for c in range(2):
        k0 = pl.multiple_of(k_lo + c * n_half, tk)
        back = lax.bitcast_convert_type((t0 - k0) - row_minus_col, jnp.uint32)
        bias_w = jnp.where(back < jnp.uint32(WIN), 0.0, NEG_BIG)
        s_w = _dot(kw_ref[0, 0, pl.ds(k0, n_half), :], qt) + tile_heads(bias_w)
        parts_w.append((s_w, vwt_ref[0, 0, :, pl.ds(k0, n_half)]))
    o_w = softmax_pv(parts_w)

    n_cmp = kc_ref.shape[2]
    row_c = lax.broadcasted_iota(jnp.int32, (n_cmp, tq), 0)
    tok_c = t0 + lax.broadcasted_iota(jnp.int32, (n_cmp, tq), 1)
    n_idx = ((row_c & (n_sel - 1)) << ratio_shift) + (row_c >> (n_sel.bit_length() - 1))
    bias_c = jnp.where(n_idx * CMP_STRIDE + (CMP_LEN - 1) <= tok_c, 0.0, NEG_BIG)
    s_c = _dot(kc_ref[0, 0], qt) + tile_heads(bias_c)
    e_c = jnp.exp2(s_c - jnp.max(s_c, axis=0, keepdims=True))
    tok_row = t0 + (lax.broadcasted_iota(jnp.int32, (1, hpg * tq), 1) & (tq - 1))
    inv_l = jnp.where(tok_row >= CMP_LEN - 1,
                      1.0 / jnp.maximum(jnp.sum(e_c, axis=0, keepdims=True), 1e-30), 0.0)
    p_c = e_c * inv_l
    o_c = _dot(vct_ref[0, 0], p_c.astype(BF16))

    p_sum = p_c[:, 0:tq]
    for h in range(1, hpg):
        p_sum = p_sum + p_c[:, h * tq:(h + 1) * tq]
    imp = p_sum[0:n_sel]
    for r in range(1, ratio):
        imp = imp + p_sum[r * n_sel:(r + 1) * n_sel]

    blk = lax.broadcasted_iota(jnp.int32, (n_sel, tq), 0)
    cur = (t0 + lax.broadcasted_iota(jnp.int32, (n_sel, tq), 1)) >> sel_shift
    forced = (blk == 0) | (blk == cur) | (blk == cur - 1)
    score = jnp.where(blk <= cur, jnp.where(forced, jnp.inf, imp), -jnp.inf)
    score_scr[...] = score
    grp_rows = 8
    row_in_grp = lax.broadcasted_iota(jnp.int32, (grp_rows, tq), 0)
    groups = [score[a:a + grp_rows] for a in range(0, n_sel, grp_rows)]
    ranks = [jnp.zeros((grp_rows, tq), F32) for _ in groups]
    for rival in range(n_sel):
        other = score_scr[rival:rival + 1, :]
        for a, mine in enumerate(groups):
            lo = a * grp_rows
            if rival < lo:
                beats = other >= mine
            elif rival >= lo + grp_rows:
                beats = other > mine
            else:
                beats = (other > mine) | ((other == mine) & (row_in_grp + lo > rival))
            ranks[a] = ranks[a] + jnp.where(beats, 1.0, 0.0)
    rank = jnp.concatenate(ranks, axis=0)
    sel_bias = jnp.where(rank < float(N_SELECT), 0.0, NEG_BIG).astype(BF16)

    pad_rows = ks_ref.shape[3] - dk - n_sel
    qt_aug = jnp.concatenate([qt, tile_heads(sel_bias), jnp.zeros((pad_rows, hpg * tq), BF16)], axis=0)
    for v in range(ks_ref.shape[2] // tq):
        @pl.when(i == v)
        def _():
            n_keys = (v + 1) * tq
            parts = []
            for start in range(0, n_keys, SEL_CHUNK):
                size = min(SEL_CHUNK, n_keys - start)
                rows = slice(start, start + size)
                s = _dot(ks_ref[0, 0, rows, :], qt_aug)
                if start + size == n_keys:
                    key_p = start + lax.broadcasted_iota(jnp.int32, (size, tq), 0)
                    tok_p = t0 + lax.broadcasted_iota(jnp.int32, (size, tq), 1)
                    s = s + tile_heads(jnp.where(key_p <= tok_p, 0.0, NEG_BIG))
                parts.append((s, vst_ref[0, 0, :, rows]))
            os_scr[...] = softmax_pv(parts)
    o_s = os_scr[...]

    row0 = pl.program_id(1) * (3 * hpg)
    gate = lambda c: jnp.concatenate([gate_ref[0, pl.ds(row0 + 3 * h + c, 1), :] for h in range(hpg)], axis=1)
    out_t = gate(0) * o_c + gate(1) * o_s + gate(2) * o_w
    stacked = jnp.concatenate([out_t[:, h * tq:(h + 1) * tq] for h in range(hpg)], axis=0)
    o_ref[...] = stacked.T.astype(o_ref.dtype)


def _nsa_attention(q2d, ks_aug, vst, kw, vwt, kc, vct, gates):
    b, g, s, dk = kw.shape
    hpg = NSA_HPG
    n_cmp = kc.shape[2]
    tq = ATT_TQ
    nt = s // tq
    assert s % SEL_CHUNK == 0 and SEL_CHUNK % tq == 0 and s >= WIN + tq
    k_spec = pl.BlockSpec((1, 1, s, dk), lambda bi, gi, i: (bi, gi, 0, 0))
    vt_spec = pl.BlockSpec((1, 1, dk, s), lambda bi, gi, i: (bi, gi, 0, 0))
    return pl.pallas_call(
        _nsa_attn_kernel,
        grid=(b, g, nt),
        in_specs=[pl.BlockSpec((tq, hpg * dk), lambda bi, gi, i: (bi * nt + i, gi)),
                  pl.BlockSpec((1, 1, s, ks_aug.shape[3]), lambda bi, gi, i: (bi, gi, 0, 0)),
                  vt_spec, k_spec, vt_spec,
                  pl.BlockSpec((1, 1, n_cmp, dk), lambda bi, gi, i: (bi, gi, 0, 0)),
                  pl.BlockSpec((1, 1, dk, n_cmp), lambda bi, gi, i: (bi, gi, 0, 0)),
                  pl.BlockSpec((1, gates.shape[1], tq), lambda bi, gi, i: (bi, 0, i))],
        out_specs=pl.BlockSpec((tq, hpg * dk), lambda bi, gi, i: (bi * nt + i, gi)),
        out_shape=jax.ShapeDtypeStruct((b * s, g * hpg * dk), BF16),
        scratch_shapes=[pltpu.VMEM((dk, hpg * tq), F32), pltpu.VMEM((n_cmp * CMP_STRIDE // SEL_LEN, tq), F32)],
        compiler_params=_cparams("parallel", "parallel", "parallel"),
        name="nsa_attention",
    )(q2d, ks_aug, vst, kw, vwt, kc, vct, gates)


def _nsa_layer(h2d, positions, g_in, w_in, cmp_pe, cmp_w1, cmp_w2):
    b, s = positions.shape
    n, d = h2d.shape
    heads, grp, hpg, dk = NSA_HEADS, NSA_KV_GROUPS, NSA_HPG, NSA_HEAD_DIM
    half = dk // 2
    qw = heads * dk
    col = lambda k: w_in[:, qw + k * NSA_KV: qw + (k + 1) * NSA_KV]
    w_gate = jnp.pad(w_in[:, qw + 6 * NSA_KV:], ((0, 0), (0, LANES - 3 * heads)))
    w_all = jnp.concatenate([w_in[:, :qw], col(2), col(4), col(0), col(1), col(3), col(5), w_gate],
                            axis=1).astype(BF16)

    q2d, ks_aug, kw, f, vst, vwt, gates = _nsa_proj(h2d, g_in, w_all, positions, b, s)

    n_grp = s // CMP_STRIDE
    pe_flat = cmp_pe.reshape(2, 1, CMP_LEN * dk)
    rot = lambda w: jnp.concatenate([-w[..., half:], w[..., :half]], axis=-1)
    w2cat = jnp.concatenate([cmp_w2, rot(cmp_w2)], axis=-1).astype(BF16)
    n_cmp = (s - CMP_LEN) // CMP_STRIDE + 1
    pad = ((0, 0), (0, n_grp - n_cmp), (0, 0))
    cos, sin = _rope_tables(positions[:, CMP_LEN - 1::CMP_STRIDE], half)
    cos_e = jnp.pad(cos, pad, constant_values=1.0)
    sin_e = jnp.pad(sin, pad)
    cos_c = jnp.stack([jnp.concatenate([cos_e, cos_e], -1), jnp.ones((b, n_grp, dk), F32)])
    sin_c = jnp.stack([jnp.concatenate([sin_e, sin_e], -1), jnp.zeros((b, n_grp, dk), F32)])
    cmp = _compress(f, pe_flat, cmp_w1.astype(BF16), w2cat, cos_c, sin_c)
    ratio = SEL_LEN // CMP_STRIDE
    cmp = (cmp.reshape(2, b, grp, n_grp // ratio, ratio, dk).transpose(0, 1, 2, 4, 3, 5)
           .reshape(2, b, grp, n_grp, dk))

    return _nsa_attention(q2d, ks_aug, vst, kw, vwt, cmp[0], cmp[1].transpose(0, 1, 3, 2), gates)


def _ret_proj_kernel(x_ref, g_ref, w_ref, pos_ref, inv_ref, qk_ref, vg_ref):
    heads, dk = RET_HEADS, RET_QK_DIM
    half = dk // 2
    a = _rms(x_ref[...], g_ref[...]).astype(BF16)
    cos, sin = _rope_cos_sin(pos_ref, inv_ref)
    for c in range(2 * heads):
        y = _dot(a, w_ref[:, c * dk:(c + 1) * dk])
        y1, y2 = y[:, :half], y[:, half:]
        scale = 1.0 if c < heads else dk ** -0.5
        qk_ref[:, c * dk:c * dk + half] = ((y1 * cos - y2 * sin) * scale).astype(qk_ref.dtype)
        qk_ref[:, c * dk + half:(c + 1) * dk] = ((y2 * cos + y1 * sin) * scale).astype(qk_ref.dtype)
    n_qk = 2 * heads * dk
    for c in range(vg_ref.shape[1] // RET_V_DIM):
        cols = slice(c * RET_V_DIM, (c + 1) * RET_V_DIM)
        vg_ref[:, cols] = _dot(a, w_ref[:, n_qk + c * RET_V_DIM:n_qk + (c + 1) * RET_V_DIM]).astype(vg_ref.dtype)


def _retention_proj(h2d, g_in, w, positions):
    n, d = h2d.shape
    n_qk = 2 * RET_HEADS * RET_QK_DIM
    n_vg = w.shape[1] - n_qk
    tm = ROW_TILE
    return pl.pallas_call(
        _ret_proj_kernel,
        grid=(n // tm,),
        in_specs=[pl.BlockSpec((tm, d), lambda i: (i, 0)),
                  _resident((1, d)),
                  _resident(w.shape),
                  pl.BlockSpec((1, 1, tm), lambda i: (i, 0, 0)), _resident((1, RET_QK_DIM // 2))],
        out_specs=[pl.BlockSpec((tm, n_qk), lambda i: (i, 0)),
                   pl.BlockSpec((tm, n_vg), lambda i: (i, 0))],
        out_shape=[jax.ShapeDtypeStruct((n, n_qk), BF16), jax.ShapeDtypeStruct((n, n_vg), BF16)],
        compiler_params=_cparams("parallel"),
        name="retention_proj",
    )(h2d, g_in.reshape(1, d), w, positions.reshape(n // tm, 1, tm),
      _rope_inv_freq(RET_QK_DIM // 2).reshape(1, -1))


def _ret_kernel(qk_ref, vg_ref, gn_ref, decay_ref, xi_ref, zeta_ref, gc_ref, wout_ref, g_out_ref, h_ref,
                o_ref, state_scr, y_scr):
    heads, dk, dv, c = RET_HEADS, RET_QK_DIM, RET_V_DIM, RET_CHUNK

    @pl.when(pl.program_id(1) == 0)
    def _():
        state_scr[...] = jnp.zeros(state_scr.shape, F32)

    for bb in range(RET_BATCH):
        for h in range(heads):
            q = qk_ref[bb, :, h * dk:(h + 1) * dk]
            k = qk_ref[bb, :, (heads + h) * dk:(heads + h + 1) * dk]
            v = vg_ref[bb, :, h * dv:(h + 1) * dv]
            s = lax.dot_general(q, k, (((1,), (1,)), ((), ())), preferred_element_type=F32)
            inner = _dot((s * decay_ref[h]).astype(BF16), v)
            state = state_scr[bb * heads + h]
            cross = _dot(q, state.astype(BF16)) * xi_ref[h]
            kz = (k.astype(F32) * zeta_ref[h]).T.astype(BF16)
            state_scr[bb * heads + h] = gc_ref[h] * state + _dot(kz, v)

            o = inner + cross
            mu = jnp.mean(o, axis=-1, keepdims=True)
            oc = o - mu
            var = jnp.mean(oc * oc, axis=-1, keepdims=True)
            on = oc * lax.rsqrt(var + LN_EPS) * gn_ref[:, h * dv:(h + 1) * dv]
            gate = vg_ref[bb, :, (heads + h) * dv:(heads + h + 1) * dv].astype(F32)
            y_scr[bb * c:(bb + 1) * c, h * dv:(h + 1) * dv] = (gate * _sigmoid(gate) * on).astype(y_scr.dtype)

    m = _dot(y_scr[...], wout_ref[...])
    for bb in range(RET_BATCH):
        o_ref[bb] = h_ref[bb] + _rms(m[bb * c:(bb + 1) * c], g_out_ref[...])


def _retention_core(qk, vg, gn_g, w_out, g_out, h2d, b, s):
    heads, dk, dv, c = RET_HEADS, RET_QK_DIM, RET_V_DIM, RET_CHUNK
    n_ch = s // c
    d = h2d.shape[1]
    bb = RET_BATCH
    assert b % bb == 0
    log_g = jnp.log1p(-(2.0 ** (-5.0 - jnp.arange(heads, dtype=F32))))
    ix = jnp.arange(c, dtype=F32)
    rel = ix[:, None] - ix[None, :]
    decay = jnp.where(rel >= 0, jnp.exp(log_g[:, None, None] * jnp.maximum(rel, 0.0)), 0.0)
    xi = jnp.exp(log_g[:, None] * (ix + 1.0))[:, :, None]
    zeta = jnp.exp(log_g[:, None] * (c - 1.0 - ix))[:, :, None]
    g_chunk = jnp.exp(log_g * c)[:, None, None]
    seq_blk = lambda width: pl.BlockSpec((bb, c, width), lambda bi, ci: (bi, ci, 0))
    out = pl.pallas_call(
        _ret_kernel,
        grid=(b // bb, n_ch),
        in_specs=[seq_blk(qk.shape[1]), seq_blk(vg.shape[1]),
                  _resident((1, heads * dv)),
                  _resident(decay.shape), _resident(xi.shape), _resident(zeta.shape),
                  _resident(g_chunk.shape), _resident(w_out.shape), _resident((1, d)),
                  seq_blk(d)],
        out_specs=seq_blk(d),
        out_shape=jax.ShapeDtypeStruct((b, s, d), F32),
        scratch_shapes=[pltpu.VMEM((bb * heads, dk, dv), F32), pltpu.VMEM((bb * c, heads * dv), BF16)],
        compiler_params=_cparams("parallel", "arbitrary"),
        name="retention_core",
    )(qk.reshape(b, s, -1), vg.reshape(b, s, -1), gn_g.reshape(1, -1), decay, xi, zeta, g_chunk,
      w_out, g_out.reshape(1, d), h2d.reshape(b, s, d))
    return out.reshape(b * s, d)


def _retention_layer(h2d, positions, g_in, w_in, gn_g, w_out, g_out):
    b, s = positions.shape
    qk, vg = _retention_proj(h2d, g_in, w_in.astype(BF16), positions)
    return _retention_core(qk, vg, gn_g, w_out.astype(BF16), g_out, h2d, b, s)


def kernel(x, positions, norm_g, ffn_w_gate, ffn_w_up, ffn_w_down, gm_w_in, gm_ln_g, gm_ln_b, gm_w_s, gm_b_s, gm_w_out, nsa_w_in, nsa_cmp_pe, nsa_cmp_w1, nsa_cmp_w2, nsa_w_out, ret_w_in, ret_gn_g, ret_w_out):
    b, s, d = x.shape
    h = x.reshape(b * s, d)
    ffn_wg, ffn_wu, ffn_wd = (w.astype(BF16) for w in (ffn_w_gate, ffn_w_up, ffn_w_down))
    gm_w_in_bf, gm_w_out_bf = gm_w_in.astype(BF16), gm_w_out.astype(BF16)
    for i in range(DEPTH):
        kind, j = i % N_MIXERS, i // N_MIXERS
        mixer = None
        if kind == 0:
            h = _gmlp_layer(h, norm_g[i, 0], gm_w_in_bf, gm_ln_g[j], gm_ln_b[j], gm_w_s,
                            gm_b_s[j], gm_w_out_bf, norm_g[i, 1], j)
        elif kind == 1:
            attn = _nsa_layer(h, positions, norm_g[i, 0], nsa_w_in[j], nsa_cmp_pe[j], nsa_cmp_w1[j],
                              nsa_cmp_w2[j])
            mixer = (attn, nsa_w_out[j].astype(BF16), norm_g[i, 1])
        else:
            h = _retention_layer(h, positions, norm_g[i, 0], ret_w_in[j], ret_gn_g[j], ret_w_out[j],
                                 norm_g[i, 1])
        h = _ffn(h, norm_g[i, 2], ffn_wg, ffn_wu, ffn_wd, norm_g[i, 3], i, mixer)
    return h.reshape(b, s, d)
```

```python
import functools
import math

import jax
import jax.numpy as jnp
from jax import lax
from jax.experimental import pallas as pl
from jax.experimental.pallas import tpu as pltpu

F32 = jnp.float32
BF16 = jnp.bfloat16

D_MODEL = 1024
DEPTH = 4
N_MIXERS = 3
RMS_EPS = 1e-6
LN_EPS = 1e-5
ROPE_THETA = 10000.0
D_FF = 2816

GM_CHUNK = 128
GM_WIDTH = 2 * D_MODEL
GM_GROUPS = 8
GM_GROUP_W = GM_WIDTH // GM_GROUPS

NSA_HEADS = 16
NSA_KV_GROUPS = 4
NSA_HPG = NSA_HEADS // NSA_KV_GROUPS
NSA_HEAD_DIM = 64
NSA_KV = NSA_KV_GROUPS * NSA_HEAD_DIM
CMP_LEN = 32
CMP_STRIDE = 16
CMP_HIDDEN = 4 * NSA_HEAD_DIM
SEL_LEN = 64
N_SELECT = 8
WIN = 512

RET_HEADS = 4
RET_QK_DIM = 256
RET_V_DIM = 512
RET_CHUNK = 128
RET_BATCH = 4

LANES = 128
BF16_SUBLANES = 16
LOG2_E = math.log2(math.e)
VMEM_LIMIT = 48 * 1024 * 1024
NEG_BIG = -1e30

ROW_TILE = 512
GM_ROW_TILE = 512
GM_SUB_TILE = 256
FFN_COL_CHUNK = 256
ATT_TQ = 256
ATT_TK = 128
SEL_CHUNK = 128


def _cparams(*sem):
    return pltpu.CompilerParams(dimension_semantics=sem, vmem_limit_bytes=VMEM_LIMIT)


def _resident(shape):
    nd = len(shape)
    return pl.BlockSpec(shape, lambda *_: (0,) * nd, pipeline_mode=pl.Buffered(1))


def _resident_layer(stacked_shape, layer):
    nd = len(stacked_shape)
    return pl.BlockSpec((None,) + tuple(stacked_shape[1:]), lambda *_: (layer,) + (0,) * (nd - 1),
                        pipeline_mode=pl.Buffered(1))


def _rms(x32, g):
    ms = jnp.mean(x32 * x32, axis=-1, keepdims=True)
    return x32 * lax.rsqrt(ms + RMS_EPS) * g


def _gelu_tanh(x):
    c = math.sqrt(2.0 / math.pi)
    half_x = 0.5 * x
    return half_x + half_x * jnp.tanh(x * (c + (c * 0.044715) * (x * x)))


def _sigmoid(x):
    return 1.0 / (1.0 + jnp.exp(-x))


def _dot(a, b):
    return jnp.dot(a, b, preferred_element_type=F32)


def _ffn_kernel(*refs, mixer_proj):
    if mixer_proj:
        m_ref, wm_ref, gm_ref, h_ref, g_in_ref, wg_ref, wu_ref, wd_ref, g_out_ref, o_ref, acc_ref = refs
        x = h_ref[...] + _rms(_dot(m_ref[...], wm_ref[...]), gm_ref[...])
    else:
        h_ref, g_in_ref, wg_ref, wu_ref, wd_ref, g_out_ref, o_ref, acc_ref = refs
        x = h_ref[...]
    a = _rms(x, g_in_ref[...]).astype(BF16)
    for c in range(D_FF // FFN_COL_CHUNK):
        cols = slice(c * FFN_COL_CHUNK, (c + 1) * FFN_COL_CHUNK)
        gate = _dot(a, wg_ref[:, cols])
        up = _dot(a, wu_ref[:, cols])
        act = (gate * _sigmoid(gate) * up).astype(BF16)
        part = _dot(act, wd_ref[cols, :])
        if c == 0:
            acc_ref[...] = part
        else:
            acc_ref[...] += part
    o_ref[...] = x + _rms(acc_ref[...], g_out_ref[...])


def _ffn(h2d, g_in, wg, wu, wd, g_out, layer, mixer=None):
    n, d = h2d.shape
    tm = ROW_TILE
    mixer_specs, mixer_args = [], []
    if mixer is not None:
        m2d, w_m, g_m = mixer
        mixer_specs = [pl.BlockSpec((tm, m2d.shape[1]), lambda i: (i, 0)), _resident(w_m.shape), _resident((1, d))]
        mixer_args = [m2d, w_m, g_m.reshape(1, d)]
    return pl.pallas_call(
        functools.partial(_ffn_kernel, mixer_proj=mixer is not None),
        grid=(n // tm,),
        in_specs=mixer_specs + [pl.BlockSpec((tm, d), lambda i: (i, 0)),
                  _resident((1, d)),
                  _resident_layer(wg.shape, layer),
                  _resident_layer(wu.shape, layer),
                  _resident_layer(wd.shape, layer),
                  _resident((1, d))],
        out_specs=pl.BlockSpec((tm, d), lambda i: (i, 0)),
        out_shape=jax.ShapeDtypeStruct((n, d), F32),
        scratch_shapes=[pltpu.VMEM((tm, d), F32)],
        compiler_params=_cparams("parallel"),
        name="swiglu_ffn",
    )(*mixer_args, h2d, g_in.reshape(1, d), wg, wu, wd, g_out.reshape(1, d))


def _gmlp_kernel(h_ref, g_in_ref, win_ref, lng_ref, lnb_ref, ws_ref, bst_ref, wout_ref,
                 g_out_ref, o_ref, vn_scr, y_scr):
    t_out = lax.broadcasted_iota(jnp.int32, (GM_CHUNK, GM_CHUNK), 0)
    t_in = lax.broadcasted_iota(jnp.int32, (GM_CHUNK, GM_CHUNK), 1)
    causal = t_in <= t_out
    bst = bst_ref[...]
    for sub in range(h_ref.shape[0] // GM_SUB_TILE):
        tile = slice(sub * GM_SUB_TILE, (sub + 1) * GM_SUB_TILE)
        x = h_ref[tile, :]
        a = _rms(x, g_in_ref[...]).astype(BF16)

        v = _gelu_tanh(_dot(a, win_ref[:, GM_WIDTH:]))
        mu = jnp.mean(v, axis=-1, keepdims=True)
        vc = v - mu
        var = jnp.mean(vc * vc, axis=-1, keepdims=True)
        vn_scr[tile, :] = (vc * lax.rsqrt(var + LN_EPS) * lng_ref[...] + lnb_ref[...]).astype(BF16)

        for grp in range(GM_GROUPS):
            cols = slice(grp * GM_GROUP_W, (grp + 1) * GM_GROUP_W)
            w_mix = jnp.where(causal, ws_ref[grp], 0.0).astype(BF16)
            bias = bst[:, grp:grp + 1]
            for c in range(GM_SUB_TILE // GM_CHUNK):
                lrows = slice(c * GM_CHUNK, (c + 1) * GM_CHUNK)
                rows = slice(sub * GM_SUB_TILE + c * GM_CHUNK, sub * GM_SUB_TILE + (c + 1) * GM_CHUNK)
                mixed = _dot(w_mix, vn_scr[rows, cols]) + bias
                u = _gelu_tanh(_dot(a[lrows], win_ref[:, cols]))
                y_scr[rows, cols] = (u * mixed).astype(BF16)

        m = _dot(y_scr[tile, :], wout_ref[...])
        o_ref[tile, :] = x + _rms(m, g_out_ref[...])


def _gmlp_layer(h2d, g_in, w_in, ln_g, ln_b, w_s, b_s, w_out, g_out, layer):
    n, d = h2d.shape
    tm = GM_ROW_TILE
    return pl.pallas_call(
        _gmlp_kernel,
        grid=(n // tm,),
        in_specs=[pl.BlockSpec((tm, d), lambda i: (i, 0)),
                  _resident((1, d)),
                  _resident_layer(w_in.shape, layer),
                  _resident((1, GM_WIDTH)),
                  _resident((1, GM_WIDTH)),
                  _resident_layer(w_s.shape, layer),
                  _resident((GM_CHUNK, GM_GROUPS)),
                  _resident_layer(w_out.shape, layer),
                  _resident((1, d))],
        out_specs=pl.BlockSpec((tm, d), lambda i: (i, 0)),
        out_shape=jax.ShapeDtypeStruct((n, d), F32),
        scratch_shapes=[pltpu.VMEM((tm, GM_WIDTH), BF16), pltpu.VMEM((tm, GM_WIDTH), BF16)],
        compiler_params=_cparams("parallel"),
        name="gmlp_layer",
    )(h2d, g_in.reshape(1, d), w_in, ln_g.reshape(1, -1), ln_b.reshape(1, -1),
      w_s, b_s.T, w_out, g_out.reshape(1, d))


def _rope_inv_freq(half):
    return ROPE_THETA ** (-jnp.arange(half, dtype=F32) / half)


def _rope_tables(positions, half):
    ang = positions.astype(F32)[..., None] * _rope_inv_freq(half)
    return jnp.cos(ang), jnp.sin(ang)


def _rope_cos_sin(pos_ref, inv_ref):
    tm = pos_ref.shape[-1]
    pos_cols = jnp.broadcast_to(pos_ref[0].astype(F32), (LANES, tm)).T
    ang = pos_cols * inv_ref[...]
    return jnp.cos(ang), jnp.sin(ang)


def _nsa_proj_kernel(x_ref, g_ref, w_ref, pos_ref, inv_ref,
                     q_ref, ks_ref, kw_ref, kvc_ref, vst_ref, vwt_ref, gate_ref, y_scr, *, tiles_per_seq):
    grp, dk, kv = NSA_KV_GROUPS, NSA_HEAD_DIM, NSA_KV
    qw = NSA_HEADS * dk
    tm = x_ref.shape[0]
    a = _rms(x_ref[...], g_ref[...]).astype(BF16)
    lane = lax.broadcasted_iota(jnp.int32, (tm, LANES), 1)
    first_half = (lane & (dk - 1)) < dk // 2
    cos, sin = _rope_cos_sin(pos_ref, inv_ref)
    sin_lo = jnp.where(first_half, -sin, 0.0)
    sin_hi = jnp.where(first_half, 0.0, sin)

    def rope_chunks(y):
        out = []
        for c in range(y.shape[1] // LANES):
            yc = y[:, c * LANES:(c + 1) * LANES]
            out.append(yc * cos + pltpu.roll(yc, LANES - dk // 2, 1) * sin_lo
                       + pltpu.roll(yc, dk // 2, 1) * sin_hi)
        return out

    def proj(k):
        return _dot(a, w_ref[:, qw + k * kv:qw + (k + 1) * kv])

    for c in range(qw // 512):
        for cc, chunk in enumerate(rope_chunks(_dot(a, w_ref[:, c * 512:(c + 1) * 512]))):
            q_ref[:, c * 512 + cc * LANES:c * 512 + (cc + 1) * LANES] = chunk.astype(q_ref.dtype)

    tok = (pl.program_id(0) % tiles_per_seq) * tm + lax.broadcasted_iota(jnp.int32, (tm, LANES), 0)
    onehot = jnp.where((tok >> (SEL_LEN.bit_length() - 1)) == lane - dk, 1.0, 0.0)
    ks_chunks = rope_chunks(proj(0))
    for gi in range(grp):
        chunk = ks_chunks[gi * dk // LANES]
        if (gi * dk) % LANES:
            chunk = pltpu.roll(chunk, LANES - (gi * dk) % LANES, 1)
        ks_ref[0, gi] = jnp.where(lane < dk, chunk, onehot).astype(ks_ref.dtype)

    def store_groups(ref_at, y):
        for gi in range(grp):
            ref_at(gi)[...] = y[:, gi * dk:(gi + 1) * dk].astype(kw_ref.dtype)

    store_groups(lambda gi: kw_ref.at[0, gi], jnp.concatenate(rope_chunks(proj(1)), axis=1))
    n_rows = tm // CMP_STRIDE
    for which in range(2):
        y = proj(2 + which)
        for c in range(kv // LANES):
            y_scr[c] = y[:, c * LANES:(c + 1) * LANES]
        for l in range(CMP_STRIDE):
            for c in range(kv // LANES):
                rows = y_scr[c, pl.ds(l, n_rows, stride=CMP_STRIDE), :]
                for sub in range(LANES // dk):
                    kvc_ref[which, 0, c * (LANES // dk) + sub, :, l * dk:(l + 1) * dk] = (
                        rows[:, sub * dk:(sub + 1) * dk].astype(kvc_ref.dtype))
    for ref, k in ((vst_ref, 4), (vwt_ref, 5)):
        y_t = proj(k).T
        for gi in range(grp):
            ref[0, gi] = y_t[gi * dk:(gi + 1) * dk].astype(ref.dtype)
    gates_t = _sigmoid(_dot(a, w_ref[:, qw + 6 * kv:])).T
    gate_ref[0] = gates_t[:gate_ref.shape[1]]


def _nsa_proj(h2d, g_in, w, positions, b, s):
    n, d = h2d.shape
    grp, dk = NSA_KV_GROUPS, NSA_HEAD_DIM
    tm = ROW_TILE
    per_b = s // tm
    row = lambda i: (i, 0)
    tok_blk = lambda i: (i // per_b, 0, i % per_b, 0)
    tr_blk = lambda i: (i // per_b, 0, 0, i % per_b)
    return pl.pallas_call(
        functools.partial(_nsa_proj_kernel, tiles_per_seq=per_b),
        grid=(n // tm,),
        in_specs=[pl.BlockSpec((tm, d), row), _resident((1, d)), _resident(w.shape),
                  pl.BlockSpec((1, 1, tm), lambda i: (i, 0, 0)), _resident((1, LANES))],
        out_specs=[pl.BlockSpec((tm, NSA_HEADS * dk), row),
                   pl.BlockSpec((1, grp, tm, LANES), tok_blk),
                   pl.BlockSpec((1, grp, tm, dk), tok_blk),
                   pl.BlockSpec((2, 1, grp, tm // CMP_STRIDE, CMP_STRIDE * dk),
                                lambda i: (0, i // per_b, 0, i % per_b, 0)),
                   pl.BlockSpec((1, grp, dk, tm), tr_blk),
                   pl.BlockSpec((1, grp, dk, tm), tr_blk),
                   pl.BlockSpec((1, 3 * NSA_HEADS, tm), lambda i: (i // per_b, 0, i % per_b))],
        out_shape=[jax.ShapeDtypeStruct((n, NSA_HEADS * dk), BF16),
                   jax.ShapeDtypeStruct((b, grp, s, LANES), BF16),
                   jax.ShapeDtypeStruct((b, grp, s, dk), BF16),
                   jax.ShapeDtypeStruct((2, b, grp, s // CMP_STRIDE, CMP_STRIDE * dk), BF16),
                   jax.ShapeDtypeStruct((b, grp, dk, s), BF16),
                   jax.ShapeDtypeStruct((b, grp, dk, s), BF16),
                   jax.ShapeDtypeStruct((b, 3 * NSA_HEADS, s), F32)],
        scratch_shapes=[pltpu.VMEM((NSA_KV // LANES, tm, LANES), F32)],
        compiler_params=_cparams("parallel"),
        name="nsa_proj",
    )(h2d, g_in.reshape(1, d), w, positions.reshape(n // tm, 1, tm),
      jnp.tile(_rope_inv_freq(dk // 2), LANES // (dk // 2)).reshape(1, LANES))


def _cmp_kernel(f_ref, pe_ref, w1_ref, w2_ref, cos_ref, sin_ref, o_ref):
    half = CMP_LEN * NSA_HEAD_DIM // 2
    for kv in range(2):
        x = f_ref[kv, 0, 0].astype(F32)
        pe = pe_ref[kv]
        first = _dot((x + pe[:, :half]).astype(BF16), w1_ref[kv, :half, :])
        second = _dot((x + pe[:, half:]).astype(BF16), w1_ref[kv, half:, :])
        hid = _gelu_tanh(first + pltpu.roll(second, second.shape[0] - 1, 0))
        y = _dot(hid.astype(BF16), w2_ref[kv])
        out = y[:, :NSA_HEAD_DIM] * cos_ref[kv, 0] + y[:, NSA_HEAD_DIM:] * sin_ref[kv, 0]
        o_ref[kv, 0, 0] = out.astype(o_ref.dtype)


def _compress(f, pe_flat, w1, w2cat, cos_t, sin_t):
    _, b, g, n_grp, width = f.shape
    dk = NSA_HEAD_DIM
    return pl.pallas_call(
        _cmp_kernel,
        grid=(b, g),
        in_specs=[pl.BlockSpec((2, 1, 1, n_grp, width), lambda i, j: (0, i, j, 0, 0)),
                  _resident(pe_flat.shape), _resident(w1.shape), _resident(w2cat.shape),
                  pl.BlockSpec((2, 1, n_grp, dk), lambda i, j: (0, i, 0, 0)),
                  pl.BlockSpec((2, 1, n_grp, dk), lambda i, j: (0, i, 0, 0))],
        out_specs=pl.BlockSpec((2, 1, 1, n_grp, dk), lambda i, j: (0, i, j, 0, 0)),
        out_shape=jax.ShapeDtypeStruct((2, b, g, n_grp, dk), BF16),
        compiler_params=_cparams("parallel", "parallel"),
        name="nsa_compress",
    )(f, pe_flat, w1, w2cat, cos_t, sin_t)


def _nsa_attn_kernel(q_ref, ks_ref, vst_ref, kw_ref, vwt_ref, kc_ref, vct_ref, gate_ref,
                     o_ref, os_scr, score_scr):
    tq, tk, hpg, dk = ATT_TQ, ATT_TK, NSA_HPG, NSA_HEAD_DIM
    ratio = SEL_LEN // CMP_STRIDE
    ratio_shift = ratio.bit_length() - 1
    sel_shift = SEL_LEN.bit_length() - 1
    n_sel = LANES // ratio
    i = pl.program_id(2)
    t0 = i * tq

    def tile_heads(x):
        return jnp.concatenate([x] * hpg, axis=1)

    q_t = q_ref[...].astype(F32).T
    qt = jnp.concatenate([q_t[h * dk:(h + 1) * dk] for h in range(hpg)], axis=1)
    qt = (qt * (dk ** -0.5 * LOG2_E)).astype(BF16)

    def softmax_pv(parts):
        maxes, accs = [], []
        for s, vt in parts:
            m_part = jnp.max(s, axis=0, keepdims=True)
            p = jnp.exp2(s - m_part).astype(BF16)
            vt_ones = jnp.concatenate([vt, jnp.ones((BF16_SUBLANES, vt.shape[1]), BF16)], axis=0)
            maxes.append(m_part)
            accs.append(_dot(vt_ones, p))
        m = functools.reduce(jnp.maximum, maxes)
        acc = sum(jnp.exp2(m_part - m) * a for m_part, a in zip(maxes, accs))
        return acc[:dk] * (1.0 / jnp.maximum(acc[dk:dk + 1], 1e-30))

    n_win = WIN + tq
    k_lo = pl.multiple_of(jnp.maximum(t0 - WIN, 0), tk)
    n_half = n_win // 2
    row_minus_col = (lax.broadcasted_iota(jnp.int32, (n_half, tq), 0)
                     - lax.broadcasted_iota(jnp.int32, (n_half, tq), 1))
    parts_w = []
    for c in range(2):
        k0 = pl.multiple_of(k_lo + c * n_half, tk)
        back = lax.bitcast_convert_type((t0 - k0) - row_minus_col, jnp.uint32)
        bias_w = jnp.where(back < jnp.uint32(WIN), 0.0, NEG_BIG)
        s_w = _dot(kw_ref[0, 0, pl.ds(k0, n_half), :], qt) + tile_heads(bias_w)
        parts_w.append((s_w, vwt_ref[0, 0, :, pl.ds(k0, n_half)]))
    o_w = softmax_pv(parts_w)

    n_cmp = kc_ref.shape[2]
    row_c = lax.broadcasted_iota(jnp.int32, (n_cmp, tq), 0)
    tok_c = t0 + lax.broadcasted_iota(jnp.int32, (n_cmp, tq), 1)
    n_idx = ((row_c & (n_sel - 1)) << ratio_shift) + (row_c >> (n_sel.bit_length() - 1))
    bias_c = jnp.where(n_idx * CMP_STRIDE + (CMP_LEN - 1) <= tok_c, 0.0, NEG_BIG)
    s_c = _dot(kc_ref[0, 0], qt) + tile_heads(bias_c)
    e_c = jnp.exp2(s_c - jnp.max(s_c, axis=0, keepdims=True))
    tok_row = t0 + (lax.broadcasted_iota(jnp.int32, (1, hpg * tq), 1) & (tq - 1))
    inv_l = jnp.where(tok_row >= CMP_LEN - 1,
                      1.0 / jnp.maximum(jnp.sum(e_c, axis=0, keepdims=True), 1e-30), 0.0)
    p_c = e_c * inv_l
    o_c = _dot(vct_ref[0, 0], p_c.astype(BF16))

    p_sum = p_c[:, 0:tq]
    for h in range(1, hpg):
        p_sum = p_sum + p_c[:, h * tq:(h + 1) * tq]
    imp = p_sum[0:n_sel]
    for r in range(1, ratio):
        imp = imp + p_sum[r * n_sel:(r + 1) * n_sel]

    blk = lax.broadcasted_iota(jnp.int32, (n_sel, tq), 0)
    cur = (t0 + lax.broadcasted_iota(jnp.int32, (n_sel, tq), 1)) >> sel_shift
    forced = (blk == 0) | (blk == cur) | (blk == cur - 1)
    score = jnp.where(blk <= cur, jnp.where(forced, jnp.inf, imp), -jnp.inf)
    score_scr[...] = score
    grp_rows = 8
    row_in_grp = lax.broadcasted_iota(jnp.int32, (grp_rows, tq), 0)
    groups = [score[a:a + grp_rows] for a in range(0, n_sel, grp_rows)]
    ranks = [jnp.zeros((grp_rows, tq), F32) for _ in groups]
    for rival in range(n_sel):
        other = score_scr[rival:rival + 1, :]
        for a, mine in enumerate(groups):
            lo = a * grp_rows
            if rival < lo:
                beats = other >= mine
            elif rival >= lo + grp_rows:
                beats = other > mine
            else:
                beats = (other > mine) | ((other == mine) & (row_in_grp + lo > rival))
            ranks[a] = ranks[a] + jnp.where(beats, 1.0, 0.0)
    rank = jnp.concatenate(ranks, axis=0)
    sel_bias = jnp.where(rank < float(N_SELECT), 0.0, NEG_BIG).astype(BF16)

    pad_rows = ks_ref.shape[3] - dk - n_sel
    qt_aug = jnp.concatenate([qt, tile_heads(sel_bias), jnp.zeros((pad_rows, hpg * tq), BF16)], axis=0)
    for v in range(ks_ref.shape[2] // tq):
        @pl.when(i == v)
        def _():
            n_keys = (v + 1) * tq
            parts = []
            for start in range(0, n_keys, SEL_CHUNK):
                size = min(SEL_CHUNK, n_keys - start)
                rows = slice(start, start + size)
                s = _dot(ks_ref[0, 0, rows, :], qt_aug)
                if start + size > n_keys - tq:
                    key_p = start + lax.broadcasted_iota(jnp.int32, (size, tq), 0)
                    tok_p = t0 + lax.broadcasted_iota(jnp.int32, (size, tq), 1)
                    s = s + tile_heads(jnp.where(key_p <= tok_p, 0.0, NEG_BIG))
                parts.append((s, vst_ref[0, 0, :, rows]))
            os_scr[...] = softmax_pv(parts)
    o_s = os_scr[...]

    row0 = pl.program_id(1) * (3 * hpg)
    gate = lambda c: jnp.concatenate([gate_ref[0, pl.ds(row0 + 3 * h + c, 1), :] for h in range(hpg)], axis=1)
    out_t = gate(0) * o_c + gate(1) * o_s + gate(2) * o_w
    stacked = jnp.concatenate([out_t[:, h * tq:(h + 1) * tq] for h in range(hpg)], axis=0)
    o_ref[...] = stacked.T.astype(o_ref.dtype)


def _nsa_attention(q2d, ks_aug, vst, kw, vwt, kc, vct, gates):
    b, g, s, dk = kw.shape
    hpg = NSA_HPG
    n_cmp = kc.shape[2]
    tq = ATT_TQ
    nt = s // tq
    assert s % tq == 0 and tq % SEL_CHUNK == 0 and s >= WIN + tq
    k_spec = pl.BlockSpec((1, 1, s, dk), lambda bi, gi, i: (bi, gi, 0, 0))
    vt_spec = pl.BlockSpec((1, 1, dk, s), lambda bi, gi, i: (bi, gi, 0, 0))
    return pl.pallas_call(
        _nsa_attn_kernel,
        grid=(b, g, nt),
        in_specs=[pl.BlockSpec((tq, hpg * dk), lambda bi, gi, i: (bi * nt + i, gi)),
                  pl.BlockSpec((1, 1, s, ks_aug.shape[3]), lambda bi, gi, i: (bi, gi, 0, 0)),
                  vt_spec, k_spec, vt_spec,
                  pl.BlockSpec((1, 1, n_cmp, dk), lambda bi, gi, i: (bi, gi, 0, 0)),
                  pl.BlockSpec((1, 1, dk, n_cmp), lambda bi, gi, i: (bi, gi, 0, 0)),
                  pl.BlockSpec((1, gates.shape[1], tq), lambda bi, gi, i: (bi, 0, i))],
        out_specs=pl.BlockSpec((tq, hpg * dk), lambda bi, gi, i: (bi * nt + i, gi)),
        out_shape=jax.ShapeDtypeStruct((b * s, g * hpg * dk), BF16),
        scratch_shapes=[pltpu.VMEM((dk, hpg * tq), F32), pltpu.VMEM((n_cmp * CMP_STRIDE // SEL_LEN, tq), F32)],
        compiler_params=_cparams("parallel", "parallel", "parallel"),
        name="nsa_attention",
    )(q2d, ks_aug, vst, kw, vwt, kc, vct, gates)


def _nsa_layer(h2d, positions, g_in, w_in, cmp_pe, cmp_w1, cmp_w2):
    b, s = positions.shape
    n, d = h2d.shape
    heads, grp, hpg, dk = NSA_HEADS, NSA_KV_GROUPS, NSA_HPG, NSA_HEAD_DIM
    half = dk // 2
    qw = heads * dk
    col = lambda k: w_in[:, qw + k * NSA_KV: qw + (k + 1) * NSA_KV]
    w_gate = jnp.pad(w_in[:, qw + 6 * NSA_KV:], ((0, 0), (0, LANES - 3 * heads)))
    w_all = jnp.concatenate([w_in[:, :qw], col(2), col(4), col(0), col(1), col(3), col(5), w_gate],
                            axis=1).astype(BF16)

    q2d, ks_aug, kw, f, vst, vwt, gates = _nsa_proj(h2d, g_in, w_all, positions, b, s)

    n_grp = s // CMP_STRIDE
    pe_flat = cmp_pe.reshape(2, 1, CMP_LEN * dk)
    rot = lambda w: jnp.concatenate([-w[..., half:], w[..., :half]], axis=-1)
    w2cat = jnp.concatenate([cmp_w2, rot(cmp_w2)], axis=-1).astype(BF16)
    n_cmp = (s - CMP_LEN) // CMP_STRIDE + 1
    pad = ((0, 0), (0, n_grp - n_cmp), (0, 0))
    cos, sin = _rope_tables(positions[:, CMP_LEN - 1::CMP_STRIDE], half)
    cos_e = jnp.pad(cos, pad, constant_values=1.0)
    sin_e = jnp.pad(sin, pad)
    cos_c = jnp.stack([jnp.concatenate([cos_e, cos_e], -1), jnp.ones((b, n_grp, dk), F32)])
    sin_c = jnp.stack([jnp.concatenate([sin_e, sin_e], -1), jnp.zeros((b, n_grp, dk), F32)])
    cmp = _compress(f, pe_flat, cmp_w1.astype(BF16), w2cat, cos_c, sin_c)
    ratio = SEL_LEN // CMP_STRIDE
    cmp = (cmp.reshape(2, b, grp, n_grp // ratio, ratio, dk).transpose(0, 1, 2, 4, 3, 5)
           .reshape(2, b, grp, n_grp, dk))

    return _nsa_attention(q2d, ks_aug, vst, kw, vwt, cmp[0], cmp[1].transpose(0, 1, 3, 2), gates)


def _ret_proj_kernel(x_ref, g_ref, w_ref, pos_ref, inv_ref, qk_ref, vg_ref):
    heads, dk = RET_HEADS, RET_QK_DIM
    half = dk // 2
    a = _rms(x_ref[...], g_ref[...]).astype(BF16)
    cos, sin = _rope_cos_sin(pos_ref, inv_ref)
    for c in range(2 * heads):
        y = _dot(a, w_ref[:, c * dk:(c + 1) * dk])
        y1, y2 = y[:, :half], y[:, half:]
        scale = 1.0 if c < heads else dk ** -0.5
        qk_ref[:, c * dk:c * dk + half] = ((y1 * cos - y2 * sin) * scale).astype(qk_ref.dtype)
        qk_ref[:, c * dk + half:(c + 1) * dk] = ((y2 * cos + y1 * sin) * scale).astype(qk_ref.dtype)
    n_qk = 2 * heads * dk
    for c in range(vg_ref.shape[1] // RET_V_DIM):
        cols = slice(c * RET_V_DIM, (c + 1) * RET_V_DIM)
        vg_ref[:, cols] = _dot(a, w_ref[:, n_qk + c * RET_V_DIM:n_qk + (c + 1) * RET_V_DIM]).astype(vg_ref.dtype)


def _retention_proj(h2d, g_in, w, positions):
    n, d = h2d.shape
    n_qk = 2 * RET_HEADS * RET_QK_DIM
    n_vg = w.shape[1] - n_qk
    tm = ROW_TILE
    return pl.pallas_call(
        _ret_proj_kernel,
        grid=(n // tm,),
        in_specs=[pl.BlockSpec((tm, d), lambda i: (i, 0)),
                  _resident((1, d)),
                  _resident(w.shape),
                  pl.BlockSpec((1, 1, tm), lambda i: (i, 0, 0)), _resident((1, RET_QK_DIM // 2))],
        out_specs=[pl.BlockSpec((tm, n_qk), lambda i: (i, 0)),
                   pl.BlockSpec((tm, n_vg), lambda i: (i, 0))],
        out_shape=[jax.ShapeDtypeStruct((n, n_qk), BF16), jax.ShapeDtypeStruct((n, n_vg), BF16)],
        compiler_params=_cparams("parallel"),
        name="retention_proj",
    )(h2d, g_in.reshape(1, d), w, positions.reshape(n // tm, 1, tm),
      _rope_inv_freq(RET_QK_DIM // 2).reshape(1, -1))


def _ret_kernel(qk_ref, vg_ref, gn_ref, decay_ref, xi_ref, zeta_ref, gc_ref, wout_ref, g_out_ref, h_ref,
                o_ref, state_scr, y_scr):
    heads, dk, dv, c = RET_HEADS, RET_QK_DIM, RET_V_DIM, RET_CHUNK

    @pl.when(pl.program_id(1) == 0)
    def _():
        state_scr[...] = jnp.zeros(state_scr.shape, F32)

    for bb in range(RET_BATCH):
        for h in range(heads):
            q = qk_ref[bb, :, h * dk:(h + 1) * dk]
            k = qk_ref[bb, :, (heads + h) * dk:(heads + h + 1) * dk]
            v = vg_ref[bb, :, h * dv:(h + 1) * dv]
            s = lax.dot_general(q, k, (((1,), (1,)), ((), ())), preferred_element_type=F32)
            inner = _dot((s * decay_ref[h]).astype(BF16), v)
            state = state_scr[bb * heads + h]
            cross = _dot(q, state.astype(BF16)) * xi_ref[h]
            kz = (k.astype(F32) * zeta_ref[h]).T.astype(BF16)
            state_scr[bb * heads + h] = gc_ref[h] * state + _dot(kz, v)

            o = inner + cross
            mu = jnp.mean(o, axis=-1, keepdims=True)
            oc = o - mu
            var = jnp.mean(oc * oc, axis=-1, keepdims=True)
            on = oc * lax.rsqrt(var + LN_EPS) * gn_ref[:, h * dv:(h + 1) * dv]
            gate = vg_ref[bb, :, (heads + h) * dv:(heads + h + 1) * dv].astype(F32)
            y_scr[bb * c:(bb + 1) * c, h * dv:(h + 1) * dv] = (gate * _sigmoid(gate) * on).astype(y_scr.dtype)

    m = _dot(y_scr[...], wout_ref[...])
    for bb in range(RET_BATCH):
        o_ref[bb] = h_ref[bb] + _rms(m[bb * c:(bb + 1) * c], g_out_ref[...])


def _retention_core(qk, vg, gn_g, w_out, g_out, h2d, b, s):
    heads, dk, dv, c = RET_HEADS, RET_QK_DIM, RET_V_DIM, RET_CHUNK
    n_ch = s // c
    d = h2d.shape[1]
    bb = RET_BATCH
    assert b % bb == 0
    log_g = jnp.log1p(-(2.0 ** (-5.0 - jnp.arange(heads, dtype=F32))))
    ix = jnp.arange(c, dtype=F32)
    rel = ix[:, None] - ix[None, :]
    decay = jnp.where(rel >= 0, jnp.exp(log_g[:, None, None] * jnp.maximum(rel, 0.0)), 0.0)
    xi = jnp.exp(log_g[:, None] * (ix + 1.0))[:, :, None]
    zeta = jnp.exp(log_g[:, None] * (c - 1.0 - ix))[:, :, None]
    g_chunk = jnp.exp(log_g * c)[:, None, None]
    seq_blk = lambda width: pl.BlockSpec((bb, c, width), lambda bi, ci: (bi, ci, 0))
    out = pl.pallas_call(
        _ret_kernel,
        grid=(b // bb, n_ch),
        in_specs=[seq_blk(qk.shape[1]), seq_blk(vg.shape[1]),
                  _resident((1, heads * dv)),
                  _resident(decay.shape), _resident(xi.shape), _resident(zeta.shape),
                  _resident(g_chunk.shape), _resident(w_out.shape), _resident((1, d)),
                  seq_blk(d)],
        out_specs=seq_blk(d),
        out_shape=jax.ShapeDtypeStruct((b, s, d), F32),
        scratch_shapes=[pltpu.VMEM((bb * heads, dk, dv), F32), pltpu.VMEM((bb * c, heads * dv), BF16)],
        compiler_params=_cparams("parallel", "arbitrary"),
        name="retention_core",
    )(qk.reshape(b, s, -1), vg.reshape(b, s, -1), gn_g.reshape(1, -1), decay, xi, zeta, g_chunk,
      w_out, g_out.reshape(1, d), h2d.reshape(b, s, d))
    return out.reshape(b * s, d)


def _retention_layer(h2d, positions, g_in, w_in, gn_g, w_out, g_out):
    b, s = positions.shape
    qk, vg = _retention_proj(h2d, g_in, w_in.astype(BF16), positions)
    return _retention_core(qk, vg, gn_g, w_out.astype(BF16), g_out, h2d, b, s)


def kernel(x, positions, norm_g, ffn_w_gate, ffn_w_up, ffn_w_down, gm_w_in, gm_ln_g, gm_ln_b, gm_w_s, gm_b_s, gm_w_out, nsa_w_in, nsa_cmp_pe, nsa_cmp_w1, nsa_cmp_w2, nsa_w_out, ret_w_in, ret_gn_g, ret_w_out):
    b, s, d = x.shape
    h = x.reshape(b * s, d)
    ffn_wg, ffn_wu, ffn_wd = (w.astype(BF16) for w in (ffn_w_gate, ffn_w_up, ffn_w_down))
    gm_w_in_bf, gm_w_out_bf = gm_w_in.astype(BF16), gm_w_out.astype(BF16)
    for i in range(DEPTH):
        kind, j = i % N_MIXERS, i // N_MIXERS
        mixer = None
        if kind == 0:
            h = _gmlp_layer(h, norm_g[i, 0], gm_w_in_bf, gm_ln_g[j], gm_ln_b[j], gm_w_s,
                            gm_b_s[j], gm_w_out_bf, norm_g[i, 1], j)
        elif kind == 1:
            attn = _nsa_layer(h, positions, norm_g[i, 0], nsa_w_in[j], nsa_cmp_pe[j], nsa_cmp_w1[j],
                              nsa_cmp_w2[j])
            mixer = (attn, nsa_w_out[j].astype(BF16), norm_g[i, 1])
        else:
            h = _retention_layer(h, positions, norm_g[i, 0], ret_w_in[j], ret_gn_g[j], ret_w_out[j],
                                 norm_g[i, 1])
        h = _ffn(h, norm_g[i, 2], ffn_wg, ffn_wu, ffn_wd, norm_g[i, 3], i, mixer)
    return h.reshape(b, s, d)
```

```python
import functools
import math

import jax
import jax.numpy as jnp
from jax import lax
from jax.experimental import pallas as pl
from jax.experimental.pallas import tpu as pltpu

F32 = jnp.float32
BF16 = jnp.bfloat16

D_MODEL = 1024
DEPTH = 4
N_MIXERS = 3
RMS_EPS = 1e-6
LN_EPS = 1e-5
ROPE_THETA = 10000.0
D_FF = 2816

GM_CHUNK = 128
GM_WIDTH = 2 * D_MODEL
GM_GROUPS = 8
GM_GROUP_W = GM_WIDTH // GM_GROUPS

NSA_HEADS = 16
NSA_KV_GROUPS = 4
NSA_HPG = NSA_HEADS // NSA_KV_GROUPS
NSA_HEAD_DIM = 64
NSA_KV = NSA_KV_GROUPS * NSA_HEAD_DIM
CMP_LEN = 32
CMP_STRIDE = 16
CMP_HIDDEN = 4 * NSA_HEAD_DIM
SEL_LEN = 64
N_SELECT = 8
WIN = 512

RET_HEADS = 4
RET_QK_DIM = 256
RET_V_DIM = 512
RET_CHUNK = 128
RET_BATCH = 4

LANES = 128
BF16_SUBLANES = 16
LOG2_E = math.log2(math.e)
VMEM_LIMIT = 48 * 1024 * 1024
NEG_BIG = -1e30

ROW_TILE = 512
GM_ROW_TILE = 1024
GM_SUB_TILE = 256
FFN_COL_CHUNK = 256
FFN_ROW_TILE = 1024
FFN_SUB_TILE = 512
ATT_TQ = 256
ATT_TK = 128
WIN_PART = 128
SEL_CHUNK = 128


def _cparams(*sem):
    return pltpu.CompilerParams(dimension_semantics=sem, vmem_limit_bytes=VMEM_LIMIT)


def _resident(shape):
    nd = len(shape)
    return pl.BlockSpec(shape, lambda *_: (0,) * nd, pipeline_mode=pl.Buffered(1))


def _resident_layer(stacked_shape, layer):
    nd = len(stacked_shape)
    return pl.BlockSpec((None,) + tuple(stacked_shape[1:]), lambda *_: (layer,) + (0,) * (nd - 1),
                        pipeline_mode=pl.Buffered(1))


def _rms(x32, g):
    ms = jnp.mean(x32 * x32, axis=-1, keepdims=True)
    return x32 * lax.rsqrt(ms + RMS_EPS) * g


def _gelu_tanh(x):
    c = math.sqrt(2.0 / math.pi)
    half_x = 0.5 * x
    return half_x + half_x * jnp.tanh(x * (c + (c * 0.044715) * (x * x)))


def _sigmoid(x):
    return 1.0 / (1.0 + jnp.exp(-x))


def _dot(a, b):
    return jnp.dot(a, b, preferred_element_type=F32)


def _ffn_kernel(*refs, mixer_proj):
    if mixer_proj:
        m_ref, wm_ref, gm_ref, h_ref, g_in_ref, wg_ref, wu_ref, wd_ref, g_out_ref, o_ref, acc_ref = refs
    else:
        h_ref, g_in_ref, wg_ref, wu_ref, wd_ref, g_out_ref, o_ref, acc_ref = refs
    for sub in range(h_ref.shape[0] // FFN_SUB_TILE):
        tile = slice(sub * FFN_SUB_TILE, (sub + 1) * FFN_SUB_TILE)
        x = h_ref[tile, :]
        if mixer_proj:
            x = x + _rms(_dot(m_ref[tile, :], wm_ref[...]), gm_ref[...])
        a = _rms(x, g_in_ref[...]).astype(BF16)
        for c in range(D_FF // FFN_COL_CHUNK):
            cols = slice(c * FFN_COL_CHUNK, (c + 1) * FFN_COL_CHUNK)
            gate = _dot(a, wg_ref[:, cols])
            up = _dot(a, wu_ref[:, cols])
            act = (gate * _sigmoid(gate) * up).astype(BF16)
            part = _dot(act, wd_ref[cols, :])
            if c == 0:
                acc_ref[tile, :] = part
            else:
                acc_ref[tile, :] += part
        o_ref[tile, :] = x + _rms(acc_ref[tile, :], g_out_ref[...])


def _ffn(h2d, g_in, wg, wu, wd, g_out, layer, mixer=None):
    n, d = h2d.shape
    tm = FFN_ROW_TILE
    mixer_specs, mixer_args = [], []
    if mixer is not None:
        m2d, w_m, g_m = mixer
        mixer_specs = [pl.BlockSpec((tm, m2d.shape[1]), lambda i: (i, 0)), _resident(w_m.shape), _resident((1, d))]
        mixer_args = [m2d, w_m, g_m.reshape(1, d)]
    return pl.pallas_call(
        functools.partial(_ffn_kernel, mixer_proj=mixer is not None),
        grid=(n // tm,),
        in_specs=mixer_specs + [pl.BlockSpec((tm, d), lambda i: (i, 0)),
                  _resident((1, d)),
                  _resident_layer(wg.shape, layer),
                  _resident_layer(wu.shape, layer),
                  _resident_layer(wd.shape, layer),
                  _resident((1, d))],
        out_specs=pl.BlockSpec((tm, d), lambda i: (i, 0)),
        out_shape=jax.ShapeDtypeStruct((n, d), F32),
        scratch_shapes=[pltpu.VMEM((tm, d), F32)],
        compiler_params=_cparams("parallel"),
        name="swiglu_ffn",
    )(*mixer_args, h2d, g_in.reshape(1, d), wg, wu, wd, g_out.reshape(1, d))


def _gmlp_kernel(h_ref, g_in_ref, win_ref, lng_ref, lnb_ref, ws_ref, bst_ref, wout_ref,
                 g_out_ref, o_ref, vn_scr, y_scr):
    t_out = lax.broadcasted_iota(jnp.int32, (GM_CHUNK, GM_CHUNK), 0)
    t_in = lax.broadcasted_iota(jnp.int32, (GM_CHUNK, GM_CHUNK), 1)
    causal = t_in <= t_out
    bst = bst_ref[...]
    for sub in range(h_ref.shape[0] // GM_SUB_TILE):
        tile = slice(sub * GM_SUB_TILE, (sub + 1) * GM_SUB_TILE)
        x = h_ref[tile, :]
        a = _rms(x, g_in_ref[...]).astype(BF16)

        v = _gelu_tanh(_dot(a, win_ref[:, GM_WIDTH:]))
        mu = jnp.mean(v, axis=-1, keepdims=True)
        vc = v - mu
        var = jnp.mean(vc * vc, axis=-1, keepdims=True)
        vn_scr[tile, :] = (vc * lax.rsqrt(var + LN_EPS) * lng_ref[...] + lnb_ref[...]).astype(BF16)

        for grp in range(GM_GROUPS):
            cols = slice(grp * GM_GROUP_W, (grp + 1) * GM_GROUP_W)
            w_mix = jnp.where(causal, ws_ref[grp], 0.0).astype(BF16)
            bias = bst[:, grp:grp + 1]
            for c in range(GM_SUB_TILE // GM_CHUNK):
                lrows = slice(c * GM_CHUNK, (c + 1) * GM_CHUNK)
                rows = slice(sub * GM_SUB_TILE + c * GM_CHUNK, sub * GM_SUB_TILE + (c + 1) * GM_CHUNK)
                mixed = _dot(w_mix, vn_scr[rows, cols]) + bias
                u = _gelu_tanh(_dot(a[lrows], win_ref[:, cols]))
                y_scr[rows, cols] = (u * mixed).astype(BF16)

        m = _dot(y_scr[tile, :], wout_ref[...])
        o_ref[tile, :] = x + _rms(m, g_out_ref[...])


def _gmlp_layer(h2d, g_in, w_in, ln_g, ln_b, w_s, b_s, w_out, g_out, layer):
    n, d = h2d.shape
    tm = GM_ROW_TILE
    return pl.pallas_call(
        _gmlp_kernel,
        grid=(n // tm,),
        in_specs=[pl.BlockSpec((tm, d), lambda i: (i, 0)),
                  _resident((1, d)),
                  _resident_layer(w_in.shape, layer),
                  _resident((1, GM_WIDTH)),
                  _resident((1, GM_WIDTH)),
                  _resident_layer(w_s.shape, layer),
                  _resident((GM_CHUNK, GM_GROUPS)),
                  _resident_layer(w_out.shape, layer),
                  _resident((1, d))],
        out_specs=pl.BlockSpec((tm, d), lambda i: (i, 0)),
        out_shape=jax.ShapeDtypeStruct((n, d), F32),
        scratch_shapes=[pltpu.VMEM((tm, GM_WIDTH), BF16), pltpu.VMEM((tm, GM_WIDTH), BF16)],
        compiler_params=_cparams("parallel"),
        name="gmlp_layer",
    )(h2d, g_in.reshape(1, d), w_in, ln_g.reshape(1, -1), ln_b.reshape(1, -1),
      w_s, b_s.T, w_out, g_out.reshape(1, d))


def _rope_inv_freq(half):
    return ROPE_THETA ** (-jnp.arange(half, dtype=F32) / half)


def _rope_tables(positions, half):
    ang = positions.astype(F32)[..., None] * _rope_inv_freq(half)
    return jnp.cos(ang), jnp.sin(ang)


def _rope_cos_sin(pos_ref, inv_ref):
    tm = pos_ref.shape[-1]
    pos_cols = jnp.broadcast_to(pos_ref[0].astype(F32), (LANES, tm)).T
    ang = pos_cols * inv_ref[...]
    return jnp.cos(ang), jnp.sin(ang)


def _nsa_proj_kernel(x_ref, g_ref, w_ref, pos_ref, inv_ref,
                     q_ref, ks_ref, kw_ref, kvc_ref, vst_ref, vwt_ref, gate_ref, y_scr, *, tiles_per_seq):
    grp, dk, kv = NSA_KV_GROUPS, NSA_HEAD_DIM, NSA_KV
    qw = NSA_HEADS * dk
    tm = x_ref.shape[0]
    a = _rms(x_ref[...], g_ref[...]).astype(BF16)
    lane = lax.broadcasted_iota(jnp.int32, (tm, LANES), 1)
    first_half = (lane & (dk - 1)) < dk // 2
    cos, sin = _rope_cos_sin(pos_ref, inv_ref)
    sin_lo = jnp.where(first_half, -sin, 0.0)
    sin_hi = jnp.where(first_half, 0.0, sin)

    def rope_chunks(y):
        out = []
        for c in range(y.shape[1] // LANES):
            yc = y[:, c * LANES:(c + 1) * LANES]
            out.append(yc * cos + pltpu.roll(yc, LANES - dk // 2, 1) * sin_lo
                       + pltpu.roll(yc, dk // 2, 1) * sin_hi)
        return out

    def proj(k):
        return _dot(a, w_ref[:, qw + k * kv:qw + (k + 1) * kv])

    for c in range(qw // 512):
        for cc, chunk in enumerate(rope_chunks(_dot(a, w_ref[:, c * 512:(c + 1) * 512]))):
            q_ref[:, c * 512 + cc * LANES:c * 512 + (cc + 1) * LANES] = chunk.astype(q_ref.dtype)

    tok = (pl.program_id(0) % tiles_per_seq) * tm + lax.broadcasted_iota(jnp.int32, (tm, LANES), 0)
    onehot = jnp.where((tok >> (SEL_LEN.bit_length() - 1)) == lane - dk, 1.0, 0.0)
    ks_chunks = rope_chunks(proj(0))
    for gi in range(grp):
        chunk = ks_chunks[gi * dk // LANES]
        if (gi * dk) % LANES:
            chunk = pltpu.roll(chunk, LANES - (gi * dk) % LANES, 1)
        ks_ref[0, gi] = jnp.where(lane < dk, chunk, onehot).astype(ks_ref.dtype)

    def store_groups(ref_at, y):
        for gi in range(grp):
            ref_at(gi)[...] = y[:, gi * dk:(gi + 1) * dk].astype(kw_ref.dtype)

    store_groups(lambda gi: kw_ref.at[0, gi], jnp.concatenate(rope_chunks(proj(1)), axis=1))
    n_rows = tm // CMP_STRIDE
    for which in range(2):
        y = proj(2 + which)
        for c in range(kv // LANES):
            y_scr[c] = y[:, c * LANES:(c + 1) * LANES]
        for l in range(CMP_STRIDE):
            for c in range(kv // LANES):
                rows = y_scr[c, pl.ds(l, n_rows, stride=CMP_STRIDE), :]
                for sub in range(LANES // dk):
                    kvc_ref[which, 0, c * (LANES // dk) + sub, :, l * dk:(l + 1) * dk] = (
                        rows[:, sub * dk:(sub + 1) * dk].astype(kvc_ref.dtype))
    for ref, k in ((vst_ref, 4), (vwt_ref, 5)):
        y_t = proj(k).T
        for gi in range(grp):
            ref[0, gi] = y_t[gi * dk:(gi + 1) * dk].astype(ref.dtype)
    gates_t = _sigmoid(_dot(a, w_ref[:, qw + 6 * kv:])).T
    gate_ref[0] = gates_t[:gate_ref.shape[1]]


def _nsa_proj(h2d, g_in, w, positions, b, s):
    n, d = h2d.shape
    grp, dk = NSA_KV_GROUPS, NSA_HEAD_DIM
    tm = ROW_TILE
    per_b = s // tm
    row = lambda i: (i, 0)
    tok_blk = lambda i: (i // per_b, 0, i % per_b, 0)
    tr_blk = lambda i: (i // per_b, 0, 0, i % per_b)
    return pl.pallas_call(
        functools.partial(_nsa_proj_kernel, tiles_per_seq=per_b),
        grid=(n // tm,),
        in_specs=[pl.BlockSpec((tm, d), row), _resident((1, d)), _resident(w.shape),
                  pl.BlockSpec((1, 1, tm), lambda i: (i, 0, 0)), _resident((1, LANES))],
        out_specs=[pl.BlockSpec((tm, NSA_HEADS * dk), row),
                   pl.BlockSpec((1, grp, tm, LANES), tok_blk),
                   pl.BlockSpec((1, grp, tm, dk), tok_blk),
                   pl.BlockSpec((2, 1, grp, tm // CMP_STRIDE, CMP_STRIDE * dk),
                                lambda i: (0, i // per_b, 0, i % per_b, 0)),
                   pl.BlockSpec((1, grp, dk, tm), tr_blk),
                   pl.BlockSpec((1, grp, dk, tm), tr_blk),
                   pl.BlockSpec((1, 3 * NSA_HEADS, tm), lambda i: (i // per_b, 0, i % per_b))],
        out_shape=[jax.ShapeDtypeStruct((n, NSA_HEADS * dk), BF16),
                   jax.ShapeDtypeStruct((b, grp, s, LANES), BF16),
                   jax.ShapeDtypeStruct((b, grp, s, dk), BF16),
                   jax.ShapeDtypeStruct((2, b, grp, s // CMP_STRIDE, CMP_STRIDE * dk), BF16),
                   jax.ShapeDtypeStruct((b, grp, dk, s), BF16),
                   jax.ShapeDtypeStruct((b, grp, dk, s), BF16),
                   jax.ShapeDtypeStruct((b, 3 * NSA_HEADS, s), F32)],
        scratch_shapes=[pltpu.VMEM((NSA_KV // LANES, tm, LANES), F32)],
        compiler_params=_cparams("parallel"),
        name="nsa_proj",
    )(h2d, g_in.reshape(1, d), w, positions.reshape(n // tm, 1, tm),
      jnp.tile(_rope_inv_freq(dk // 2), LANES // (dk // 2)).reshape(1, LANES))


def _cmp_kernel(f_ref, pe_ref, w1_ref, w2_ref, cos_ref, sin_ref, o_ref):
    half = CMP_LEN * NSA_HEAD_DIM // 2
    for kv in range(2):
        x = f_ref[kv, 0, 0].astype(F32)
        pe = pe_ref[kv]
        first = _dot((x + pe[:, :half]).astype(BF16), w1_ref[kv, :half, :])
        second = _dot((x + pe[:, half:]).astype(BF16), w1_ref[kv, half:, :])
        hid = _gelu_tanh(first + pltpu.roll(second, second.shape[0] - 1, 0))
        y = _dot(hid.astype(BF16), w2_ref[kv])
        out = y[:, :NSA_HEAD_DIM] * cos_ref[kv, 0] + y[:, NSA_HEAD_DIM:] * sin_ref[kv, 0]
        o_ref[kv, 0, 0] = out.astype(o_ref.dtype)


def _compress(f, pe_flat, w1, w2cat, cos_t, sin_t):
    _, b, g, n_grp, width = f.shape
    dk = NSA_HEAD_DIM
    return pl.pallas_call(
        _cmp_kernel,
        grid=(b, g),
        in_specs=[pl.BlockSpec((2, 1, 1, n_grp, width), lambda i, j: (0, i, j, 0, 0)),
                  _resident(pe_flat.shape), _resident(w1.shape), _resident(w2cat.shape),
                  pl.BlockSpec((2, 1, n_grp, dk), lambda i, j: (0, i, 0, 0)),
                  pl.BlockSpec((2, 1, n_grp, dk), lambda i, j: (0, i, 0, 0))],
        out_specs=pl.BlockSpec((2, 1, 1, n_grp, dk), lambda i, j: (0, i, j, 0, 0)),
        out_shape=jax.ShapeDtypeStruct((2, b, g, n_grp, dk), BF16),
        compiler_params=_cparams("parallel", "parallel"),
        name="nsa_compress",
    )(f, pe_flat, w1, w2cat, cos_t, sin_t)


def _nsa_attn_kernel(q_ref, ks_ref, vst_ref, kw_ref, vwt_ref, kc_ref, vct_ref, gate_ref,
                     o_ref, os_scr, score_scr):
    tq, tk, hpg, dk = ATT_TQ, ATT_TK, NSA_HPG, NSA_HEAD_DIM
    ratio = SEL_LEN // CMP_STRIDE
    ratio_shift = ratio.bit_length() - 1
    sel_shift = SEL_LEN.bit_length() - 1
    n_sel = LANES // ratio
    i = pl.program_id(2)
    t0 = i * tq

    def tile_heads(x):
        return jnp.concatenate([x] * hpg, axis=1)

    q_t = q_ref[...].astype(F32).T
    qt = jnp.concatenate([q_t[h * dk:(h + 1) * dk] for h in range(hpg)], axis=1)
    qt = (qt * (dk ** -0.5 * LOG2_E)).astype(BF16)

    def softmax_pv(parts):
        maxes, accs = [], []
        for s, vt in parts:
            m_part = jnp.max(s, axis=0, keepdims=True)
            p = jnp.exp2(s - m_part).astype(BF16)
            vt_ones = jnp.concatenate([vt, jnp.ones((BF16_SUBLANES, vt.shape[1]), BF16)], axis=0)
            maxes.append(m_part)
            accs.append(_dot(vt_ones, p))
        m = functools.reduce(jnp.maximum, maxes)
        acc = sum(jnp.exp2(m_part - m) * a for m_part, a in zip(maxes, accs))
        return acc[:dk] * (1.0 / jnp.maximum(acc[dk:dk + 1], 1e-30))

    n_win = WIN + tq
    k_lo = pl.multiple_of(jnp.maximum(t0 - WIN, 0), tk)
    n_half = WIN_PART
    row_minus_col = (lax.broadcasted_iota(jnp.int32, (n_half, tq), 0)
                     - lax.broadcasted_iota(jnp.int32, (n_half, tq), 1))
    parts_w = []
    for c in range(n_win // WIN_PART):
        k0 = pl.multiple_of(k_lo + c * n_half, tk)
        back = lax.bitcast_convert_type((t0 - k0) - row_minus_col, jnp.uint32)
        bias_w = jnp.where(back < jnp.uint32(WIN), 0.0, NEG_BIG)
        s_w = _dot(kw_ref[0, 0, pl.ds(k0, n_half), :], qt) + tile_heads(bias_w)
        parts_w.append((s_w, vwt_ref[0, 0, :, pl.ds(k0, n_half)]))
    o_w = softmax_pv(parts_w)

    n_cmp = kc_ref.shape[2]
    row_c = lax.broadcasted_iota(jnp.int32, (n_cmp, tq), 0)
    tok_c = t0 + lax.broadcasted_iota(jnp.int32, (n_cmp, tq), 1)
    n_idx = ((row_c & (n_sel - 1)) << ratio_shift) + (row_c >> (n_sel.bit_length() - 1))
    bias_c = jnp.where(n_idx * CMP_STRIDE + (CMP_LEN - 1) <= tok_c, 0.0, NEG_BIG)
    s_c = _dot(kc_ref[0, 0], qt) + tile_heads(bias_c)
    e_c = jnp.exp2(s_c - jnp.max(s_c, axis=0, keepdims=True))
    tok_row = t0 + (lax.broadcasted_iota(jnp.int32, (1, hpg * tq), 1) & (tq - 1))
    inv_l = jnp.where(tok_row >= CMP_LEN - 1,
                      1.0 / jnp.maximum(jnp.sum(e_c, axis=0, keepdims=True), 1e-30), 0.0)
    p_c = e_c * inv_l
    o_c = _dot(vct_ref[0, 0], p_c.astype(BF16))

    p_sum = p_c[:, 0:tq]
    for h in range(1, hpg):
        p_sum = p_sum + p_c[:, h * tq:(h + 1) * tq]
    imp = p_sum[0:n_sel]
    for r in range(1, ratio):
        imp = imp + p_sum[r * n_sel:(r + 1) * n_sel]

    blk = lax.broadcasted_iota(jnp.int32, (n_sel, tq), 0)
    cur = (t0 + lax.broadcasted_iota(jnp.int32, (n_sel, tq), 1)) >> sel_shift
    forced = (blk == 0) | (blk == cur) | (blk == cur - 1)
    score = jnp.where(blk <= cur, jnp.where(forced, jnp.inf, imp), -jnp.inf)
    score_scr[...] = score
    grp_rows = 8
    row_in_grp = lax.broadcasted_iota(jnp.int32, (grp_rows, tq), 0)
    groups = [score[a:a + grp_rows] for a in range(0, n_sel, grp_rows)]
    ranks = [jnp.zeros((grp_rows, tq), F32) for _ in groups]
    for rival in range(n_sel):
        other = score_scr[rival:rival + 1, :]
        for a, mine in enumerate(groups):
            lo = a * grp_rows
            if rival < lo:
                beats = other >= mine
            elif rival >= lo + grp_rows:
                beats = other > mine
            else:
                beats = (other > mine) | ((other == mine) & (row_in_grp + lo > rival))
            ranks[a] = ranks[a] + jnp.where(beats, 1.0, 0.0)
    rank = jnp.concatenate(ranks, axis=0)
    sel_bias = jnp.where(rank < float(N_SELECT), 0.0, NEG_BIG).astype(BF16)

    pad_rows = ks_ref.shape[3] - dk - n_sel
    qt_aug = jnp.concatenate([qt, tile_heads(sel_bias), jnp.zeros((pad_rows, hpg * tq), BF16)], axis=0)
    for v in range(ks_ref.shape[2] // tq):
        @pl.when(i == v)
        def _():
            n_keys = (v + 1) * tq
            parts = []
            for start in range(0, n_keys, SEL_CHUNK):
                size = min(SEL_CHUNK, n_keys - start)
                rows = slice(start, start + size)
                s = _dot(ks_ref[0, 0, rows, :], qt_aug)
                if start + size > n_keys - tq:
                    key_p = start + lax.broadcasted_iota(jnp.int32, (size, tq), 0)
                    tok_p = t0 + lax.broadcasted_iota(jnp.int32, (size, tq), 1)
                    s = s + tile_heads(jnp.where(key_p <= tok_p, 0.0, NEG_BIG))
                parts.append((s, vst_ref[0, 0, :, rows]))
            os_scr[...] = softmax_pv(parts)
    o_s = os_scr[...]

    row0 = pl.program_id(1) * (3 * hpg)
    gate = lambda c: jnp.concatenate([gate_ref[0, pl.ds(row0 + 3 * h + c, 1), :] for h in range(hpg)], axis=1)
    out_t = gate(0) * o_c + gate(1) * o_s + gate(2) * o_w
    stacked = jnp.concatenate([out_t[:, h * tq:(h + 1) * tq] for h in range(hpg)], axis=0)
    o_ref[...] = stacked.T.astype(o_ref.dtype)


def _nsa_attention(q2d, ks_aug, vst, kw, vwt, kc, vct, gates):
    b, g, s, dk = kw.shape
    hpg = NSA_HPG
    n_cmp = kc.shape[2]
    tq = ATT_TQ
    nt = s // tq
    assert s % tq == 0 and tq % SEL_CHUNK == 0 and s >= WIN + tq
    k_spec = pl.BlockSpec((1, 1, s, dk), lambda bi, gi, i: (bi, gi, 0, 0))
    vt_spec = pl.BlockSpec((1, 1, dk, s), lambda bi, gi, i: (bi, gi, 0, 0))
    return pl.pallas_call(
        _nsa_attn_kernel,
        grid=(b, g, nt),
        in_specs=[pl.BlockSpec((tq, hpg * dk), lambda bi, gi, i: (bi * nt + i, gi)),
                  pl.BlockSpec((1, 1, s, ks_aug.shape[3]), lambda bi, gi, i: (bi, gi, 0, 0)),
                  vt_spec, k_spec, vt_spec,
                  pl.BlockSpec((1, 1, n_cmp, dk), lambda bi, gi, i: (bi, gi, 0, 0)),
                  pl.BlockSpec((1, 1, dk, n_cmp), lambda bi, gi, i: (bi, gi, 0, 0)),
                  pl.BlockSpec((1, gates.shape[1], tq), lambda bi, gi, i: (bi, 0, i))],
        out_specs=pl.BlockSpec((tq, hpg * dk), lambda bi, gi, i: (bi * nt + i, gi)),
        out_shape=jax.ShapeDtypeStruct((b * s, g * hpg * dk), BF16),
        scratch_shapes=[pltpu.VMEM((dk, hpg * tq), F32), pltpu.VMEM((n_cmp * CMP_STRIDE // SEL_LEN, tq), F32)],
        compiler_params=_cparams("parallel", "parallel", "parallel"),
        name="nsa_attention",
    )(q2d, ks_aug, vst, kw, vwt, kc, vct, gates)


def _nsa_layer(h2d, positions, g_in, w_in, cmp_pe, cmp_w1, cmp_w2):
    b, s = positions.shape
    n, d = h2d.shape
    heads, grp, hpg, dk = NSA_HEADS, NSA_KV_GROUPS, NSA_HPG, NSA_HEAD_DIM
    half = dk // 2
    qw = heads * dk
    col = lambda k: w_in[:, qw + k * NSA_KV: qw + (k + 1) * NSA_KV]
    w_gate = jnp.pad(w_in[:, qw + 6 * NSA_KV:], ((0, 0), (0, LANES - 3 * heads)))
    w_all = jnp.concatenate([w_in[:, :qw], col(2), col(4), col(0), col(1), col(3), col(5), w_gate],
                            axis=1).astype(BF16)

    q2d, ks_aug, kw, f, vst, vwt, gates = _nsa_proj(h2d, g_in, w_all, positions, b, s)

    n_grp = s // CMP_STRIDE
    pe_flat = cmp_pe.reshape(2, 1, CMP_LEN * dk)
    rot = lambda w: jnp.concatenate([-w[..., half:], w[..., :half]], axis=-1)
    w2cat = jnp.concatenate([cmp_w2, rot(cmp_w2)], axis=-1).astype(BF16)
    n_cmp = (s - CMP_LEN) // CMP_STRIDE + 1
    pad = ((0, 0), (0, n_grp - n_cmp), (0, 0))
    cos, sin = _rope_tables(positions[:, CMP_LEN - 1::CMP_STRIDE], half)
    cos_e = jnp.pad(cos, pad, constant_values=1.0)
    sin_e = jnp.pad(sin, pad)
    cos_c = jnp.stack([jnp.concatenate([cos_e, cos_e], -1), jnp.ones((b, n_grp, dk), F32)])
    sin_c = jnp.stack([jnp.concatenate([sin_e, sin_e], -1), jnp.zeros((b, n_grp, dk), F32)])
    cmp = _compress(f, pe_flat, cmp_w1.astype(BF16), w2cat, cos_c, sin_c)
    ratio = SEL_LEN // CMP_STRIDE
    cmp = (cmp.reshape(2, b, grp, n_grp // ratio, ratio, dk).transpose(0, 1, 2, 4, 3, 5)
           .reshape(2, b, grp, n_grp, dk))

    return _nsa_attention(q2d, ks_aug, vst, kw, vwt, cmp[0], cmp[1].transpose(0, 1, 3, 2), gates)


def _ret_proj_kernel(x_ref, g_ref, w_ref, pos_ref, inv_ref, qk_ref, vg_ref):
    heads, dk = RET_HEADS, RET_QK_DIM
    half = dk // 2
    a = _rms(x_ref[...], g_ref[...]).astype(BF16)
    cos, sin = _rope_cos_sin(pos_ref, inv_ref)
    for c in range(2 * heads):
        y = _dot(a, w_ref[:, c * dk:(c + 1) * dk])
        y1, y2 = y[:, :half], y[:, half:]
        scale = 1.0 if c < heads else dk ** -0.5
        qk_ref[:, c * dk:c * dk + half] = ((y1 * cos - y2 * sin) * scale).astype(qk_ref.dtype)
        qk_ref[:, c * dk + half:(c + 1) * dk] = ((y2 * cos + y1 * sin) * scale).astype(qk_ref.dtype)
    n_qk = 2 * heads * dk
    for c in range(vg_ref.shape[1] // RET_V_DIM):
        cols = slice(c * RET_V_DIM, (c + 1) * RET_V_DIM)
        vg_ref[:, cols] = _dot(a, w_ref[:, n_qk + c * RET_V_DIM:n_qk + (c + 1) * RET_V_DIM]).astype(vg_ref.dtype)


def _retention_proj(h2d, g_in, w, positions):
    n, d = h2d.shape
    n_qk = 2 * RET_HEADS * RET_QK_DIM
    n_vg = w.shape[1] - n_qk
    tm = ROW_TILE
    return pl.pallas_call(
        _ret_proj_kernel,
        grid=(n // tm,),
        in_specs=[pl.BlockSpec((tm, d), lambda i: (i, 0)),
                  _resident((1, d)),
                  _resident(w.shape),
                  pl.BlockSpec((1, 1, tm), lambda i: (i, 0, 0)), _resident((1, RET_QK_DIM // 2))],
        out_specs=[pl.BlockSpec((tm, n_qk), lambda i: (i, 0)),
                   pl.BlockSpec((tm, n_vg), lambda i: (i, 0))],
        out_shape=[jax.ShapeDtypeStruct((n, n_qk), BF16), jax.ShapeDtypeStruct((n, n_vg), BF16)],
        compiler_params=_cparams("parallel"),
        name="retention_proj",
    )(h2d, g_in.reshape(1, d), w, positions.reshape(n // tm, 1, tm),
      _rope_inv_freq(RET_QK_DIM // 2).reshape(1, -1))


def _ret_kernel(qk_ref, vg_ref, gn_ref, decay_ref, xi_ref, zeta_ref, gc_ref, wout_ref, g_out_ref, h_ref,
                o_ref, state_scr, y_scr):
    heads, dk, dv, c = RET_HEADS, RET_QK_DIM, RET_V_DIM, RET_CHUNK

    @pl.when(pl.program_id(1) == 0)
    def _():
        state_scr[...] = jnp.zeros(state_scr.shape, F32)

    for bb in range(RET_BATCH):
        for h in range(heads):
            q = qk_ref[bb, :, h * dk:(h + 1) * dk]
            k = qk_ref[bb, :, (heads + h) * dk:(heads + h + 1) * dk]
            v = vg_ref[bb, :, h * dv:(h + 1) * dv]
            s = lax.dot_general(q, k, (((1,), (1,)), ((), ())), preferred_element_type=F32)
            inner = _dot((s * decay_ref[h]).astype(BF16), v)
            state = state_scr[bb * heads + h]
            cross = _dot(q, state.astype(BF16)) * xi_ref[h]
            kz = (k.astype(F32) * zeta_ref[h]).T.astype(BF16)
            state_scr[bb * heads + h] = gc_ref[h] * state + _dot(kz, v)

            o = inner + cross
            mu = jnp.mean(o, axis=-1, keepdims=True)
            oc = o - mu
            var = jnp.mean(oc * oc, axis=-1, keepdims=True)
            on = oc * lax.rsqrt(var + LN_EPS) * gn_ref[:, h * dv:(h + 1) * dv]
            gate = vg_ref[bb, :, (heads + h) * dv:(heads + h + 1) * dv].astype(F32)
            y_scr[bb * c:(bb + 1) * c, h * dv:(h + 1) * dv] = (gate * _sigmoid(gate) * on).astype(y_scr.dtype)

    m = _dot(y_scr[...], wout_ref[...])
    for bb in range(RET_BATCH):
        o_ref[bb] = h_ref[bb] + _rms(m[bb * c:(bb + 1) * c], g_out_ref[...])


def _retention_core(qk, vg, gn_g, w_out, g_out, h2d, b, s):
    heads, dk, dv, c = RET_HEADS, RET_QK_DIM, RET_V_DIM, RET_CHUNK
    n_ch = s // c
    d = h2d.shape[1]
    bb = RET_BATCH
    assert b % bb == 0
    log_g = jnp.log1p(-(2.0 ** (-5.0 - jnp.arange(heads, dtype=F32))))
    ix = jnp.arange(c, dtype=F32)
    rel = ix[:, None] - ix[None, :]
    decay = jnp.where(rel >= 0, jnp.exp(log_g[:, None, None] * jnp.maximum(rel, 0.0)), 0.0)
    xi = jnp.exp(log_g[:, None] * (ix + 1.0))[:, :, None]
    zeta = jnp.exp(log_g[:, None] * (c - 1.0 - ix))[:, :, None]
    g_chunk = jnp.exp(log_g * c)[:, None, None]
    seq_blk = lambda width: pl.BlockSpec((bb, c, width), lambda bi, ci: (bi, ci, 0))
    out = pl.pallas_call(
        _ret_kernel,
        grid=(b // bb, n_ch),
        in_specs=[seq_blk(qk.shape[1]), seq_blk(vg.shape[1]),
                  _resident((1, heads * dv)),
                  _resident(decay.shape), _resident(xi.shape), _resident(zeta.shape),
                  _resident(g_chunk.shape), _resident(w_out.shape), _resident((1, d)),
                  seq_blk(d)],
        out_specs=seq_blk(d),
        out_shape=jax.ShapeDtypeStruct((b, s, d), F32),
        scratch_shapes=[pltpu.VMEM((bb * heads, dk, dv), F32), pltpu.VMEM((bb * c, heads * dv), BF16)],
        compiler_params=_cparams("parallel", "arbitrary"),
        name="retention_core",
    )(qk.reshape(b, s, -1), vg.reshape(b, s, -1), gn_g.reshape(1, -1), decay, xi, zeta, g_chunk,
      w_out, g_out.reshape(1, d), h2d.reshape(b, s, d))
    return out.reshape(b * s, d)


def _retention_layer(h2d, positions, g_in, w_in, gn_g, w_out, g_out):
    b, s = positions.shape
    qk, vg = _retention_proj(h2d, g_in, w_in.astype(BF16), positions)
    return _retention_core(qk, vg, gn_g, w_out.astype(BF16), g_out, h2d, b, s)


def kernel(x, positions, norm_g, ffn_w_gate, ffn_w_up, ffn_w_down, gm_w_in, gm_ln_g, gm_ln_b, gm_w_s, gm_b_s, gm_w_out, nsa_w_in, nsa_cmp_pe, nsa_cmp_w1, nsa_cmp_w2, nsa_w_out, ret_w_in, ret_gn_g, ret_w_out):
    b, s, d = x.shape
    h = x.reshape(b * s, d)
    ffn_wg, ffn_wu, ffn_wd = (w.astype(BF16) for w in (ffn_w_gate, ffn_w_up, ffn_w_down))
    gm_w_in_bf, gm_w_out_bf = gm_w_in.astype(BF16), gm_w_out.astype(BF16)
    for i in range(DEPTH):
        kind, j = i % N_MIXERS, i // N_MIXERS
        mixer = None
        if kind == 0:
            h = _gmlp_layer(h, norm_g[i, 0], gm_w_in_bf, gm_ln_g[j], gm_ln_b[j], gm_w_s,
                            gm_b_s[j], gm_w_out_bf, norm_g[i, 1], j)
        elif kind == 1:
            attn = _nsa_layer(h, positions, norm_g[i, 0], nsa_w_in[j], nsa_cmp_pe[j], nsa_cmp_w1[j],
                              nsa_cmp_w2[j])
            mixer = (attn, nsa_w_out[j].astype(BF16), norm_g[i, 1])
        else:
            h = _retention_layer(h, positions, norm_g[i, 0], ret_w_in[j], ret_gn_g[j], ret_w_out[j],
                                 norm_g[i, 1])
        h = _ffn(h, norm_g[i, 2], ffn_wg, ffn_wu, ffn_wd, norm_g[i, 3], i, mixer)
    return h.reshape(b, s, d)
```

```python
import functools
import math

import jax
import jax.numpy as jnp
from jax import lax
from jax.experimental import pallas as pl
from jax.experimental.pallas import tpu as pltpu

F32 = jnp.float32
BF16 = jnp.bfloat16

D_MODEL = 1024
DEPTH = 4
N_MIXERS = 3
RMS_EPS = 1e-6
LN_EPS = 1e-5
ROPE_THETA = 10000.0
D_FF = 2816

GM_CHUNK = 128
GM_WIDTH = 2 * D_MODEL
GM_GROUPS = 8
GM_GROUP_W = GM_WIDTH // GM_GROUPS

NSA_HEADS = 16
NSA_KV_GROUPS = 4
NSA_HPG = NSA_HEADS // NSA_KV_GROUPS
NSA_HEAD_DIM = 64
NSA_KV = NSA_KV_GROUPS * NSA_HEAD_DIM
CMP_LEN = 32
CMP_STRIDE = 16
CMP_HIDDEN = 4 * NSA_HEAD_DIM
SEL_LEN = 64
N_SELECT = 8
WIN = 512

RET_HEADS = 4
RET_QK_DIM = 256
RET_V_DIM = 512
RET_CHUNK = 128
RET_BATCH = 4

LANES = 128
BF16_SUBLANES = 16
LOG2_E = math.log2(math.e)
VMEM_LIMIT = 48 * 1024 * 1024
NEG_BIG = -1e30

ROW_TILE = 512
GM_ROW_TILE = 1024
GM_V_CHUNK = 512
GM_SUB_TILE = 256
FFN_COL_CHUNK = 256
FFN_ROW_TILE = 1024
FFN_SUB_TILE = 512
ATT_TQ = 256
ATT_TK = 128
WIN_PART = 128
SEL_CHUNK = 128


def _cparams(*sem):
    return pltpu.CompilerParams(dimension_semantics=sem, vmem_limit_bytes=VMEM_LIMIT)


def _resident(shape):
    nd = len(shape)
    return pl.BlockSpec(shape, lambda *_: (0,) * nd, pipeline_mode=pl.Buffered(1))


def _resident_layer(stacked_shape, layer):
    nd = len(stacked_shape)
    return pl.BlockSpec((None,) + tuple(stacked_shape[1:]), lambda *_: (layer,) + (0,) * (nd - 1),
                        pipeline_mode=pl.Buffered(1))


def _rms(x32, g):
    ms = jnp.mean(x32 * x32, axis=-1, keepdims=True)
    return x32 * lax.rsqrt(ms + RMS_EPS) * g


def _gelu_tanh(x):
    c = math.sqrt(2.0 / math.pi)
    half_x = 0.5 * x
    return half_x + half_x * jnp.tanh(x * (c + (c * 0.044715) * (x * x)))


def _sigmoid(x):
    return 1.0 / (1.0 + jnp.exp(-x))


def _dot(a, b):
    return jnp.dot(a, b, preferred_element_type=F32)


def _ffn_kernel(*refs, mixer_proj):
    if mixer_proj:
        m_ref, wm_ref, gm_ref, h_ref, g_in_ref, wg_ref, wu_ref, wd_ref, g_out_ref, o_ref, acc_ref = refs
    else:
        h_ref, g_in_ref, wg_ref, wu_ref, wd_ref, g_out_ref, o_ref, acc_ref = refs
    for sub in range(h_ref.shape[0] // FFN_SUB_TILE):
        tile = slice(sub * FFN_SUB_TILE, (sub + 1) * FFN_SUB_TILE)
        x = h_ref[tile, :]
        if mixer_proj:
            x = x + _rms(_dot(m_ref[tile, :], wm_ref[...]), gm_ref[...])
        a = _rms(x, g_in_ref[...]).astype(BF16)
        for c in range(D_FF // FFN_COL_CHUNK):
            cols = slice(c * FFN_COL_CHUNK, (c + 1) * FFN_COL_CHUNK)
            gate = _dot(a, wg_ref[:, cols])
            up = _dot(a, wu_ref[:, cols])
            act = (gate * _sigmoid(gate) * up).astype(BF16)
            part = _dot(act, wd_ref[cols, :])
            if c == 0:
                acc_ref[tile, :] = part
            else:
                acc_ref[tile, :] += part
        o_ref[tile, :] = x + _rms(acc_ref[tile, :], g_out_ref[...])


def _ffn(h2d, g_in, wg, wu, wd, g_out, layer, mixer=None):
    n, d = h2d.shape
    tm = FFN_ROW_TILE
    mixer_specs, mixer_args = [], []
    if mixer is not None:
        m2d, w_m, g_m = mixer
        mixer_specs = [pl.BlockSpec((tm, m2d.shape[1]), lambda i: (i, 0)), _resident(w_m.shape), _resident((1, d))]
        mixer_args = [m2d, w_m, g_m.reshape(1, d)]
    return pl.pallas_call(
        functools.partial(_ffn_kernel, mixer_proj=mixer is not None),
        grid=(n // tm,),
        in_specs=mixer_specs + [pl.BlockSpec((tm, d), lambda i: (i, 0)),
                  _resident((1, d)),
                  _resident_layer(wg.shape, layer),
                  _resident_layer(wu.shape, layer),
                  _resident_layer(wd.shape, layer),
                  _resident((1, d))],
        out_specs=pl.BlockSpec((tm, d), lambda i: (i, 0)),
        out_shape=jax.ShapeDtypeStruct((n, d), F32),
        scratch_shapes=[pltpu.VMEM((tm, d), F32)],
        compiler_params=_cparams("parallel"),
        name="swiglu_ffn",
    )(*mixer_args, h2d, g_in.reshape(1, d), wg, wu, wd, g_out.reshape(1, d))


def _gmlp_kernel(h_ref, g_in_ref, win_ref, lng_ref, lnb_ref, ws_ref, bst_ref, wout_ref,
                 g_out_ref, o_ref, vn_scr, y_scr):
    t_out = lax.broadcasted_iota(jnp.int32, (GM_CHUNK, GM_CHUNK), 0)
    t_in = lax.broadcasted_iota(jnp.int32, (GM_CHUNK, GM_CHUNK), 1)
    causal = t_in <= t_out
    bst = bst_ref[...]
    for sub in range(h_ref.shape[0] // GM_SUB_TILE):
        tile = slice(sub * GM_SUB_TILE, (sub + 1) * GM_SUB_TILE)
        x = h_ref[tile, :]
        a = _rms(x, g_in_ref[...]).astype(BF16)

        v = jnp.concatenate(
            [_gelu_tanh(_dot(a, win_ref[:, GM_WIDTH + c * GM_V_CHUNK:GM_WIDTH + (c + 1) * GM_V_CHUNK]))
             for c in range(GM_WIDTH // GM_V_CHUNK)], axis=1)
        mu = jnp.mean(v, axis=-1, keepdims=True)
        vc = v - mu
        var = jnp.mean(vc * vc, axis=-1, keepdims=True)
        vn_scr[tile, :] = (vc * lax.rsqrt(var + LN_EPS) * lng_ref[...] + lnb_ref[...]).astype(BF16)

        for grp in range(GM_GROUPS):
            cols = slice(grp * GM_GROUP_W, (grp + 1) * GM_GROUP_W)
            w_mix = jnp.where(causal, ws_ref[grp], 0.0).astype(BF16)
            bias = bst[:, grp:grp + 1]
            for c in range(GM_SUB_TILE // GM_CHUNK):
                lrows = slice(c * GM_CHUNK, (c + 1) * GM_CHUNK)
                rows = slice(sub * GM_SUB_TILE + c * GM_CHUNK, sub * GM_SUB_TILE + (c + 1) * GM_CHUNK)
                mixed = _dot(w_mix, vn_scr[rows, cols]) + bias
                u = _gelu_tanh(_dot(a[lrows], win_ref[:, cols]))
                y_scr[rows, cols] = (u * mixed).astype(BF16)

        m = _dot(y_scr[tile, :], wout_ref[...])
        o_ref[tile, :] = x + _rms(m, g_out_ref[...])


def _gmlp_layer(h2d, g_in, w_in, ln_g, ln_b, w_s, b_s, w_out, g_out, layer):
    n, d = h2d.shape
    tm = GM_ROW_TILE
    return pl.pallas_call(
        _gmlp_kernel,
        grid=(n // tm,),
        in_specs=[pl.BlockSpec((tm, d), lambda i: (i, 0)),
                  _resident((1, d)),
                  _resident_layer(w_in.shape, layer),
                  _resident((1, GM_WIDTH)),
                  _resident((1, GM_WIDTH)),
                  _resident_layer(w_s.shape, layer),
                  _resident((GM_CHUNK, GM_GROUPS)),
                  _resident_layer(w_out.shape, layer),
                  _resident((1, d))],
        out_specs=pl.BlockSpec((tm, d), lambda i: (i, 0)),
        out_shape=jax.ShapeDtypeStruct((n, d), F32),
        scratch_shapes=[pltpu.VMEM((tm, GM_WIDTH), BF16), pltpu.VMEM((tm, GM_WIDTH), BF16)],
        compiler_params=_cparams("parallel"),
        name="gmlp_layer",
    )(h2d, g_in.reshape(1, d), w_in, ln_g.reshape(1, -1), ln_b.reshape(1, -1),
      w_s, b_s.T, w_out, g_out.reshape(1, d))


def _rope_inv_freq(half):
    return ROPE_THETA ** (-jnp.arange(half, dtype=F32) / half)


def _rope_tables(positions, half):
    ang = positions.astype(F32)[..., None] * _rope_inv_freq(half)
    return jnp.cos(ang), jnp.sin(ang)


def _rope_cos_sin(pos_ref, inv_ref):
    tm = pos_ref.shape[-1]
    pos_cols = jnp.broadcast_to(pos_ref[0].astype(F32), (LANES, tm)).T
    ang = pos_cols * inv_ref[...]
    return jnp.cos(ang), jnp.sin(ang)


def _nsa_proj_kernel(x_ref, g_ref, w_ref, pos_ref, inv_ref,
                     q_ref, ks_ref, kw_ref, kvc_ref, vst_ref, vwt_ref, gate_ref, y_scr, *, tiles_per_seq):
    grp, dk, kv = NSA_KV_GROUPS, NSA_HEAD_DIM, NSA_KV
    qw = NSA_HEADS * dk
    tm = x_ref.shape[0]
    a = _rms(x_ref[...], g_ref[...]).astype(BF16)
    lane = lax.broadcasted_iota(jnp.int32, (tm, LANES), 1)
    first_half = (lane & (dk - 1)) < dk // 2
    cos, sin = _rope_cos_sin(pos_ref, inv_ref)
    sin_lo = jnp.where(first_half, -sin, 0.0)
    sin_hi = jnp.where(first_half, 0.0, sin)

    def rope_chunks(y):
        out = []
        for c in range(y.shape[1] // LANES):
            yc = y[:, c * LANES:(c + 1) * LANES]
            out.append(yc * cos + pltpu.roll(yc, LANES - dk // 2, 1) * sin_lo
                       + pltpu.roll(yc, dk // 2, 1) * sin_hi)
        return out

    def proj(k):
        return _dot(a, w_ref[:, qw + k * kv:qw + (k + 1) * kv])

    for c in range(qw // 512):
        for cc, chunk in enumerate(rope_chunks(_dot(a, w_ref[:, c * 512:(c + 1) * 512]))):
            q_ref[:, c * 512 + cc * LANES:c * 512 + (cc + 1) * LANES] = chunk.astype(q_ref.dtype)

    tok = (pl.program_id(0) % tiles_per_seq) * tm + lax.broadcasted_iota(jnp.int32, (tm, LANES), 0)
    onehot = jnp.where((tok >> (SEL_LEN.bit_length() - 1)) == lane - dk, 1.0, 0.0)
    ks_chunks = rope_chunks(proj(0))
    for gi in range(grp):
        chunk = ks_chunks[gi * dk // LANES]
        if (gi * dk) % LANES:
            chunk = pltpu.roll(chunk, LANES - (gi * dk) % LANES, 1)
        ks_ref[0, gi] = jnp.where(lane < dk, chunk, onehot).astype(ks_ref.dtype)

    def store_groups(ref_at, y):
        for gi in range(grp):
            ref_at(gi)[...] = y[:, gi * dk:(gi + 1) * dk].astype(kw_ref.dtype)

    store_groups(lambda gi: kw_ref.at[0, gi], jnp.concatenate(rope_chunks(proj(1)), axis=1))
    n_rows = tm // CMP_STRIDE
    for which in range(2):
        y = proj(2 + which)
        for c in range(kv // LANES):
            y_scr[c] = y[:, c * LANES:(c + 1) * LANES]
        for l in range(CMP_STRIDE):
            for c in range(kv // LANES):
                rows = y_scr[c, pl.ds(l, n_rows, stride=CMP_STRIDE), :]
                for sub in range(LANES // dk):
                    kvc_ref[which, 0, c * (LANES // dk) + sub, :, l * dk:(l + 1) * dk] = (
                        rows[:, sub * dk:(sub + 1) * dk].astype(kvc_ref.dtype))
    for ref, k in ((vst_ref, 4), (vwt_ref, 5)):
        y_t = proj(k).T
        for gi in range(grp):
            ref[0, gi] = y_t[gi * dk:(gi + 1) * dk].astype(ref.dtype)
    gates_t = _sigmoid(_dot(a, w_ref[:, qw + 6 * kv:])).T
    gate_ref[0] = gates_t[:gate_ref.shape[1]]


def _nsa_proj(h2d, g_in, w, positions, b, s):
    n, d = h2d.shape
    grp, dk = NSA_KV_GROUPS, NSA_HEAD_DIM
    tm = ROW_TILE
    per_b = s // tm
    row = lambda i: (i, 0)
    tok_blk = lambda i: (i // per_b, 0, i % per_b, 0)
    tr_blk = lambda i: (i // per_b, 0, 0, i % per_b)
    return pl.pallas_call(
        functools.partial(_nsa_proj_kernel, tiles_per_seq=per_b),
        grid=(n // tm,),
        in_specs=[pl.BlockSpec((tm, d), row), _resident((1, d)), _resident(w.shape),
                  pl.BlockSpec((1, 1, tm), lambda i: (i, 0, 0)), _resident((1, LANES))],
        out_specs=[pl.BlockSpec((tm, NSA_HEADS * dk), row),
                   pl.BlockSpec((1, grp, tm, LANES), tok_blk),
                   pl.BlockSpec((1, grp, tm, dk), tok_blk),
                   pl.BlockSpec((2, 1, grp, tm // CMP_STRIDE, CMP_STRIDE * dk),
                                lambda i: (0, i // per_b, 0, i % per_b, 0)),
                   pl.BlockSpec((1, grp, dk, tm), tr_blk),
                   pl.BlockSpec((1, grp, dk, tm), tr_blk),
                   pl.BlockSpec((1, 3 * NSA_HEADS, tm), lambda i: (i // per_b, 0, i % per_b))],
        out_shape=[jax.ShapeDtypeStruct((n, NSA_HEADS * dk), BF16),
                   jax.ShapeDtypeStruct((b, grp, s, LANES), BF16),
                   jax.ShapeDtypeStruct((b, grp, s, dk), BF16),
                   jax.ShapeDtypeStruct((2, b, grp, s // CMP_STRIDE, CMP_STRIDE * dk), BF16),
                   jax.ShapeDtypeStruct((b, grp, dk, s), BF16),
                   jax.ShapeDtypeStruct((b, grp, dk, s), BF16),
                   jax.ShapeDtypeStruct((b, 3 * NSA_HEADS, s), F32)],
        scratch_shapes=[pltpu.VMEM((NSA_KV // LANES, tm, LANES), F32)],
        compiler_params=_cparams("parallel"),
        name="nsa_proj",
    )(h2d, g_in.reshape(1, d), w, positions.reshape(n // tm, 1, tm),
      jnp.tile(_rope_inv_freq(dk // 2), LANES // (dk // 2)).reshape(1, LANES))


def _cmp_kernel(f_ref, pe_ref, w1_ref, w2_ref, cos_ref, sin_ref, o_ref):
    half = CMP_LEN * NSA_HEAD_DIM // 2
    for kv in range(2):
        x = f_ref[kv, 0, 0].astype(F32)
        pe = pe_ref[kv]
        first = _dot((x + pe[:, :half]).astype(BF16), w1_ref[kv, :half, :])
        second = _dot((x + pe[:, half:]).astype(BF16), w1_ref[kv, half:, :])
        hid = _gelu_tanh(first + pltpu.roll(second, second.shape[0] - 1, 0))
        y = _dot(hid.astype(BF16), w2_ref[kv])
        out = y[:, :NSA_HEAD_DIM] * cos_ref[kv, 0] + y[:, NSA_HEAD_DIM:] * sin_ref[kv, 0]
        o_ref[kv, 0, 0] = out.astype(o_ref.dtype)


def _compress(f, pe_flat, w1, w2cat, cos_t, sin_t):
    _, b, g, n_grp, width = f.shape
    dk = NSA_HEAD_DIM
    return pl.pallas_call(
        _cmp_kernel,
        grid=(b, g),
        in_specs=[pl.BlockSpec((2, 1, 1, n_grp, width), lambda i, j: (0, i, j, 0, 0)),
                  _resident(pe_flat.shape), _resident(w1.shape), _resident(w2cat.shape),
                  pl.BlockSpec((2, 1, n_grp, dk), lambda i, j: (0, i, 0, 0)),
                  pl.BlockSpec((2, 1, n_grp, dk), lambda i, j: (0, i, 0, 0))],
        out_specs=pl.BlockSpec((2, 1, 1, n_grp, dk), lambda i, j: (0, i, j, 0, 0)),
        out_shape=jax.ShapeDtypeStruct((2, b, g, n_grp, dk), BF16),
        compiler_params=_cparams("parallel", "parallel"),
        name="nsa_compress",
    )(f, pe_flat, w1, w2cat, cos_t, sin_t)


def _nsa_attn_kernel(q_ref, ks_ref, vst_ref, kw_ref, vwt_ref, kc_ref, vct_ref, gate_ref,
                     o_ref, os_scr, score_scr):
    tq, tk, hpg, dk = ATT_TQ, ATT_TK, NSA_HPG, NSA_HEAD_DIM
    ratio = SEL_LEN // CMP_STRIDE
    ratio_shift = ratio.bit_length() - 1
    sel_shift = SEL_LEN.bit_length() - 1
    n_sel = LANES // ratio
    i = pl.program_id(2)
    t0 = i * tq

    def tile_heads(x):
        return jnp.concatenate([x] * hpg, axis=1)

    q_t = q_ref[...].astype(F32).T
    qt = jnp.concatenate([q_t[h * dk:(h + 1) * dk] for h in range(hpg)], axis=1)
    qt = (qt * (dk ** -0.5 * LOG2_E)).astype(BF16)

    def softmax_pv(parts):
        maxes, accs = [], []
        for s, vt in parts:
            m_part = jnp.max(s, axis=0, keepdims=True)
            p = jnp.exp2(s - m_part).astype(BF16)
            vt_ones = jnp.concatenate([vt, jnp.ones((BF16_SUBLANES, vt.shape[1]), BF16)], axis=0)
            maxes.append(m_part)
            accs.append(_dot(vt_ones, p))
        m = functools.reduce(jnp.maximum, maxes)
        acc = sum(jnp.exp2(m_part - m) * a for m_part, a in zip(maxes, accs))
        return acc[:dk] * (1.0 / jnp.maximum(acc[dk:dk + 1], 1e-30))

    n_win = WIN + tq
    k_lo = pl.multiple_of(jnp.maximum(t0 - WIN, 0), tk)
    n_half = WIN_PART
    row_minus_col = (lax.broadcasted_iota(jnp.int32, (n_half, tq), 0)
                     - lax.broadcasted_iota(jnp.int32, (n_half, tq), 1))
    parts_w = []
    for c in range(n_win // WIN_PART):
        k0 = pl.multiple_of(k_lo + c * n_half, tk)
        back = lax.bitcast_convert_type((t0 - k0) - row_minus_col, jnp.uint32)
        bias_w = jnp.where(back < jnp.uint32(WIN), 0.0, NEG_BIG)
        s_w = _dot(kw_ref[0, 0, pl.ds(k0, n_half), :], qt) + tile_heads(bias_w)
        parts_w.append((s_w, vwt_ref[0, 0, :, pl.ds(k0, n_half)]))
    o_w = softmax_pv(parts_w)

    n_cmp = kc_ref.shape[2]
    row_c = lax.broadcasted_iota(jnp.int32, (n_cmp, tq), 0)
    tok_c = t0 + lax.broadcasted_iota(jnp.int32, (n_cmp, tq), 1)
    n_idx = ((row_c & (n_sel - 1)) << ratio_shift) + (row_c >> (n_sel.bit_length() - 1))
    bias_c = jnp.where(n_idx * CMP_STRIDE + (CMP_LEN - 1) <= tok_c, 0.0, NEG_BIG)
    s_c = _dot(kc_ref[0, 0], qt) + tile_heads(bias_c)
    e_c = jnp.exp2(s_c - jnp.max(s_c, axis=0, keepdims=True))
    tok_row = t0 + (lax.broadcasted_iota(jnp.int32, (1, hpg * tq), 1) & (tq - 1))
    inv_l = jnp.where(tok_row >= CMP_LEN - 1,
                      1.0 / jnp.maximum(jnp.sum(e_c, axis=0, keepdims=True), 1e-30), 0.0)
    p_c = e_c * inv_l
    o_c = _dot(vct_ref[0, 0], p_c.astype(BF16))

    p_sum = p_c[:, 0:tq]
    for h in range(1, hpg):
        p_sum = p_sum + p_c[:, h * tq:(h + 1) * tq]
    imp = p_sum[0:n_sel]
    for r in range(1, ratio):
        imp = imp + p_sum[r * n_sel:(r + 1) * n_sel]

    blk = lax.broadcasted_iota(jnp.int32, (n_sel, tq), 0)
    cur = (t0 + lax.broadcasted_iota(jnp.int32, (n_sel, tq), 1)) >> sel_shift
    forced = (blk == 0) | (blk == cur) | (blk == cur - 1)
    score = jnp.where(blk <= cur, jnp.where(forced, jnp.inf, imp), -jnp.inf)
    score_scr[...] = score
    grp_rows = 8
    row_in_grp = lax.broadcasted_iota(jnp.int32, (grp_rows, tq), 0)
    groups = [score[a:a + grp_rows] for a in range(0, n_sel, grp_rows)]
    ranks = [jnp.zeros((grp_rows, tq), F32) for _ in groups]
    for rival in range(n_sel):
        other = score_scr[rival:rival + 1, :]
        for a, mine in enumerate(groups):
            lo = a * grp_rows
            if rival < lo:
                beats = other >= mine
            elif rival >= lo + grp_rows:
                beats = other > mine
            else:
                beats = (other > mine) | ((other == mine) & (row_in_grp + lo > rival))
            ranks[a] = ranks[a] + jnp.where(beats, 1.0, 0.0)
    rank = jnp.concatenate(ranks, axis=0)
    sel_bias = jnp.where(rank < float(N_SELECT), 0.0, NEG_BIG).astype(BF16)

    pad_rows = ks_ref.shape[3] - dk - n_sel
    qt_aug = jnp.concatenate([qt, tile_heads(sel_bias), jnp.zeros((pad_rows, hpg * tq), BF16)], axis=0)
    for v in range(ks_ref.shape[2] // tq):
        @pl.when(i == v)
        def _():
            n_keys = (v + 1) * tq
            parts = []
            for start in range(0, n_keys, SEL_CHUNK):
                size = min(SEL_CHUNK, n_keys - start)
                rows = slice(start, start + size)
                s = _dot(ks_ref[0, 0, rows, :], qt_aug)
                if start + size > n_keys - tq:
                    key_p = start + lax.broadcasted_iota(jnp.int32, (size, tq), 0)
                    tok_p = t0 + lax.broadcasted_iota(jnp.int32, (size, tq), 1)
                    s = s + tile_heads(jnp.where(key_p <= tok_p, 0.0, NEG_BIG))
                parts.append((s, vst_ref[0, 0, :, rows]))
            os_scr[...] = softmax_pv(parts)
    o_s = os_scr[...]

    row0 = pl.program_id(1) * (3 * hpg)
    gate = lambda c: jnp.concatenate([gate_ref[0, pl.ds(row0 + 3 * h + c, 1), :] for h in range(hpg)], axis=1)
    out_t = gate(0) * o_c + gate(1) * o_s + gate(2) * o_w
    stacked = jnp.concatenate([out_t[:, h * tq:(h + 1) * tq] for h in range(hpg)], axis=0)
    o_ref[...] = stacked.T.astype(o_ref.dtype)


def _nsa_attention(q2d, ks_aug, vst, kw, vwt, kc, vct, gates):
    b, g, s, dk = kw.shape
    hpg = NSA_HPG
    n_cmp = kc.shape[2]
    tq = ATT_TQ
    nt = s // tq
    assert s % tq == 0 and tq % SEL_CHUNK == 0 and s >= WIN + tq
    k_spec = pl.BlockSpec((1, 1, s, dk), lambda bi, gi, i: (bi, gi, 0, 0))
    vt_spec = pl.BlockSpec((1, 1, dk, s), lambda bi, gi, i: (bi, gi, 0, 0))
    return pl.pallas_call(
        _nsa_attn_kernel,
        grid=(b, g, nt),
        in_specs=[pl.BlockSpec((tq, hpg * dk), lambda bi, gi, i: (bi * nt + i, gi)),
                  pl.BlockSpec((1, 1, s, ks_aug.shape[3]), lambda bi, gi, i: (bi, gi, 0, 0)),
                  vt_spec, k_spec, vt_spec,
                  pl.BlockSpec((1, 1, n_cmp, dk), lambda bi, gi, i: (bi, gi, 0, 0)),
                  pl.BlockSpec((1, 1, dk, n_cmp), lambda bi, gi, i: (bi, gi, 0, 0)),
                  pl.BlockSpec((1, gates.shape[1], tq), lambda bi, gi, i: (bi, 0, i))],
        out_specs=pl.BlockSpec((tq, hpg * dk), lambda bi, gi, i: (bi * nt + i, gi)),
        out_shape=jax.ShapeDtypeStruct((b * s, g * hpg * dk), BF16),
        scratch_shapes=[pltpu.VMEM((dk, hpg * tq), F32), pltpu.VMEM((n_cmp * CMP_STRIDE // SEL_LEN, tq), F32)],
        compiler_params=_cparams("parallel", "parallel", "parallel"),
        name="nsa_attention",
    )(q2d, ks_aug, vst, kw, vwt, kc, vct, gates)


def _nsa_layer(h2d, positions, g_in, w_in, cmp_pe, cmp_w1, cmp_w2):
    b, s = positions.shape
    n, d = h2d.shape
    heads, grp, hpg, dk = NSA_HEADS, NSA_KV_GROUPS, NSA_HPG, NSA_HEAD_DIM
    half = dk // 2
    qw = heads * dk
    col = lambda k: w_in[:, qw + k * NSA_KV: qw + (k + 1) * NSA_KV]
    w_gate = jnp.pad(w_in[:, qw + 6 * NSA_KV:], ((0, 0), (0, LANES - 3 * heads)))
    w_all = jnp.concatenate([w_in[:, :qw], col(2), col(4), col(0), col(1), col(3), col(5), w_gate],
                            axis=1).astype(BF16)

    q2d, ks_aug, kw, f, vst, vwt, gates = _nsa_proj(h2d, g_in, w_all, positions, b, s)

    n_grp = s // CMP_STRIDE
    pe_flat = cmp_pe.reshape(2, 1, CMP_LEN * dk)
    rot = lambda w: jnp.concatenate([-w[..., half:], w[..., :half]], axis=-1)
    w2cat = jnp.concatenate([cmp_w2, rot(cmp_w2)], axis=-1).astype(BF16)
    n_cmp = (s - CMP_LEN) // CMP_STRIDE + 1
    pad = ((0, 0), (0, n_grp - n_cmp), (0, 0))
    cos, sin = _rope_tables(positions[:, CMP_LEN - 1::CMP_STRIDE], half)
    cos_e = jnp.pad(cos, pad, constant_values=1.0)
    sin_e = jnp.pad(sin, pad)
    cos_c = jnp.stack([jnp.concatenate([cos_e, cos_e], -1), jnp.ones((b, n_grp, dk), F32)])
    sin_c = jnp.stack([jnp.concatenate([sin_e, sin_e], -1), jnp.zeros((b, n_grp, dk), F32)])
    cmp = _compress(f, pe_flat, cmp_w1.astype(BF16), w2cat, cos_c, sin_c)
    ratio = SEL_LEN // CMP_STRIDE
    cmp = (cmp.reshape(2, b, grp, n_grp // ratio, ratio, dk).transpose(0, 1, 2, 4, 3, 5)
           .reshape(2, b, grp, n_grp, dk))

    return _nsa_attention(q2d, ks_aug, vst, kw, vwt, cmp[0], cmp[1].transpose(0, 1, 3, 2), gates)


def _ret_proj_kernel(x_ref, g_ref, w_ref, pos_ref, inv_ref, qk_ref, vg_ref):
    heads, dk = RET_HEADS, RET_QK_DIM
    half = dk // 2
    a = _rms(x_ref[...], g_ref[...]).astype(BF16)
    cos, sin = _rope_cos_sin(pos_ref, inv_ref)
    for c in range(2 * heads):
        y = _dot(a, w_ref[:, c * dk:(c + 1) * dk])
        y1, y2 = y[:, :half], y[:, half:]
        scale = 1.0 if c < heads else dk ** -0.5
        qk_ref[:, c * dk:c * dk + half] = ((y1 * cos - y2 * sin) * scale).astype(qk_ref.dtype)
        qk_ref[:, c * dk + half:(c + 1) * dk] = ((y2 * cos + y1 * sin) * scale).astype(qk_ref.dtype)
    n_qk = 2 * heads * dk
    for c in range(vg_ref.shape[1] // RET_V_DIM):
        cols = slice(c * RET_V_DIM, (c + 1) * RET_V_DIM)
        vg_ref[:, cols] = _dot(a, w_ref[:, n_qk + c * RET_V_DIM:n_qk + (c + 1) * RET_V_DIM]).astype(vg_ref.dtype)


def _retention_proj(h2d, g_in, w, positions):
    n, d = h2d.shape
    n_qk = 2 * RET_HEADS * RET_QK_DIM
    n_vg = w.shape[1] - n_qk
    tm = ROW_TILE
    return pl.pallas_call(
        _ret_proj_kernel,
        grid=(n // tm,),
        in_specs=[pl.BlockSpec((tm, d), lambda i: (i, 0)),
                  _resident((1, d)),
                  _resident(w.shape),
                  pl.BlockSpec((1, 1, tm), lambda i: (i, 0, 0)), _resident((1, RET_QK_DIM // 2))],
        out_specs=[pl.BlockSpec((tm, n_qk), lambda i: (i, 0)),
                   pl.BlockSpec((tm, n_vg), lambda i: (i, 0))],
        out_shape=[jax.ShapeDtypeStruct((n, n_qk), BF16), jax.ShapeDtypeStruct((n, n_vg), BF16)],
        compiler_params=_cparams("parallel"),
        name="retention_proj",
    )(h2d, g_in.reshape(1, d), w, positions.reshape(n // tm, 1, tm),
      _rope_inv_freq(RET_QK_DIM // 2).reshape(1, -1))


def _ret_kernel(qk_ref, vg_ref, gn_ref, decay_ref, xi_ref, zeta_ref, gc_ref, wout_ref, g_out_ref, h_ref,
                o_ref, state_scr, y_scr):
    heads, dk, dv, c = RET_HEADS, RET_QK_DIM, RET_V_DIM, RET_CHUNK

    @pl.when(pl.program_id(1) == 0)
    def _():
        state_scr[...] = jnp.zeros(state_scr.shape, F32)

    for bb in range(RET_BATCH):
        for h in range(heads):
            q = qk_ref[bb, :, h * dk:(h + 1) * dk]
            k = qk_ref[bb, :, (heads + h) * dk:(heads + h + 1) * dk]
            v = vg_ref[bb, :, h * dv:(h + 1) * dv]
            s = lax.dot_general(q, k, (((1,), (1,)), ((), ())), preferred_element_type=F32)
            inner = _dot((s * decay_ref[h]).astype(BF16), v)
            state = state_scr[bb * heads + h]
            cross = _dot(q, state.astype(BF16)) * xi_ref[h]
            kz = (k.astype(F32) * zeta_ref[h]).T.astype(BF16)
            state_scr[bb * heads + h] = gc_ref[h] * state + _dot(kz, v)

            o = inner + cross
            mu = jnp.mean(o, axis=-1, keepdims=True)
            oc = o - mu
            var = jnp.mean(oc * oc, axis=-1, keepdims=True)
            on = oc * lax.rsqrt(var + LN_EPS) * gn_ref[:, h * dv:(h + 1) * dv]
            gate = vg_ref[bb, :, (heads + h) * dv:(heads + h + 1) * dv].astype(F32)
            y_scr[bb * c:(bb + 1) * c, h * dv:(h + 1) * dv] = (gate * _sigmoid(gate) * on).astype(y_scr.dtype)

    m = _dot(y_scr[...], wout_ref[...])
    for bb in range(RET_BATCH):
        o_ref[bb] = h_ref[bb] + _rms(m[bb * c:(bb + 1) * c], g_out_ref[...])


def _retention_core(qk, vg, gn_g, w_out, g_out, h2d, b, s):
    heads, dk, dv, c = RET_HEADS, RET_QK_DIM, RET_V_DIM, RET_CHUNK
    n_ch = s // c
    d = h2d.shape[1]
    bb = RET_BATCH
    assert b % bb == 0
    log_g = jnp.log1p(-(2.0 ** (-5.0 - jnp.arange(heads, dtype=F32))))
    ix = jnp.arange(c, dtype=F32)
    rel = ix[:, None] - ix[None, :]
    decay = jnp.where(rel >= 0, jnp.exp(log_g[:, None, None] * jnp.maximum(rel, 0.0)), 0.0)
    xi = jnp.exp(log_g[:, None] * (ix + 1.0))[:, :, None]
    zeta = jnp.exp(log_g[:, None] * (c - 1.0 - ix))[:, :, None]
    g_chunk = jnp.exp(log_g * c)[:, None, None]
    seq_blk = lambda width: pl.BlockSpec((bb, c, width), lambda bi, ci: (bi, ci, 0))
    out = pl.pallas_call(
        _ret_kernel,
        grid=(b // bb, n_ch),
        in_specs=[seq_blk(qk.shape[1]), seq_blk(vg.shape[1]),
                  _resident((1, heads * dv)),
                  _resident(decay.shape), _resident(xi.shape), _resident(zeta.shape),
                  _resident(g_chunk.shape), _resident(w_out.shape), _resident((1, d)),
                  seq_blk(d)],
        out_specs=seq_blk(d),
        out_shape=jax.ShapeDtypeStruct((b, s, d), F32),
        scratch_shapes=[pltpu.VMEM((bb * heads, dk, dv), F32), pltpu.VMEM((bb * c, heads * dv), BF16)],
        compiler_params=_cparams("parallel", "arbitrary"),
        name="retention_core",
    )(qk.reshape(b, s, -1), vg.reshape(b, s, -1), gn_g.reshape(1, -1), decay, xi, zeta, g_chunk,
      w_out, g_out.reshape(1, d), h2d.reshape(b, s, d))
    return out.reshape(b * s, d)


def _retention_layer(h2d, positions, g_in, w_in, gn_g, w_out, g_out):
    b, s = positions.shape
    qk, vg = _retention_proj(h2d, g_in, w_in.astype(BF16), positions)
    return _retention_core(qk, vg, gn_g, w_out.astype(BF16), g_out, h2d, b, s)


def kernel(x, positions, norm_g, ffn_w_gate, ffn_w_up, ffn_w_down, gm_w_in, gm_ln_g, gm_ln_b, gm_w_s, gm_b_s, gm_w_out, nsa_w_in, nsa_cmp_pe, nsa_cmp_w1, nsa_cmp_w2, nsa_w_out, ret_w_in, ret_gn_g, ret_w_out):
    b, s, d = x.shape
    h = x.reshape(b * s, d)
    ffn_wg, ffn_wu, ffn_wd = (w.astype(BF16) for w in (ffn_w_gate, ffn_w_up, ffn_w_down))
    gm_w_in_bf, gm_w_out_bf = gm_w_in.astype(BF16), gm_w_out.astype(BF16)
    for i in range(DEPTH):
        kind, j = i % N_MIXERS, i // N_MIXERS
        mixer = None
        if kind == 0:
            h = _gmlp_layer(h, norm_g[i, 0], gm_w_in_bf, gm_ln_g[j], gm_ln_b[j], gm_w_s,
                            gm_b_s[j], gm_w_out_bf, norm_g[i, 1], j)
        elif kind == 1:
            attn = _nsa_layer(h, positions, norm_g[i, 0], nsa_w_in[j], nsa_cmp_pe[j], nsa_cmp_w1[j],
                              nsa_cmp_w2[j])
            mixer = (attn, nsa_w_out[j].astype(BF16), norm_g[i, 1])
        else:
            h = _retention_layer(h, positions, norm_g[i, 0], ret_w_in[j], ret_gn_g[j], ret_w_out[j],
                                 norm_g[i, 1])
        h = _ffn(h, norm_g[i, 2], ffn_wg, ffn_wu, ffn_wd, norm_g[i, 3], i, mixer)
    return h.reshape(b, s, d)
```
